```python
import math
import jax
import jax.numpy as jnp
from jax import lax
import numpy as np

D_MODEL = 1024
BATCH = 32
SEQ = 2048
DEPTH = 4

N_MIXERS = 3
N_MLA_LAYERS = (DEPTH + 2) // 3
N_NSA_LAYERS = (DEPTH + 1) // 3
N_S5_LAYERS = DEPTH // 3
NORM_EPS = 1e-6
N_MOD = 6
Q_BLOCK = 128

MLA_HEADS = 8
MLA_Q_LORA = 512
MLA_KV_LORA = 256
MLA_NOPE = 128
MLA_ROPE = 64
MLA_V = 128
MLA_QK = MLA_NOPE + MLA_ROPE
ROPE_THETA = 10000.0

NSA_HEADS = 8
NSA_GROUPS = 2
NSA_HPG = NSA_HEADS // NSA_GROUPS
NSA_DK = 128
NSA_DV = 128
CMP_BLOCK = 32
CMP_STRIDE = 16
CMP_RATIO = CMP_BLOCK // CMP_STRIDE
CMP_HIDDEN = 256
SLC_BLOCK = 64
SLC_TOPN = 16
N_LOCAL = 2
FORCE_BONUS = 1e4
WINDOW = 512
SLC_Q_CHUNK = 16
NSA_SPLITS = (NSA_HEADS * NSA_DK,
              NSA_GROUPS * NSA_DK, NSA_GROUPS * NSA_DV,
              NSA_GROUPS * NSA_DK, NSA_GROUPS * NSA_DV,
              NSA_GROUPS * NSA_DK, NSA_GROUPS * NSA_DV,
              3 * NSA_HEADS)
NSA_PROJ = NSA_HEADS * NSA_DK + 3 * NSA_GROUPS * (NSA_DK + NSA_DV) + 3 * NSA_HEADS

REL_BUCKETS = 32
REL_MAX_DIST = 128

S5_GROUP = 16
S5_STATE = 64
S5_NGROUPS = D_MODEL // S5_GROUP
DT_MIN = 1e-3
DT_MAX = 1e-1

D_FF = 2816
CONV_WIDTH = 3

kernel_name = 'hybrid_mla_nsa_s5_conv_glu_trunk'


def _split(t, sizes):
    return jnp.split(t, np.cumsum(sizes)[:-1].tolist(), axis=-1)


def rms_norm(x, g):
    xf = x.astype(jnp.float32)
    y = xf * lax.rsqrt(jnp.mean(xf * xf, axis=-1, keepdims=True) + NORM_EPS)
    return (y * g.astype(jnp.float32)).astype(x.dtype)


def masked_softmax(s, mask):
    s = jnp.where(mask, s, -jnp.inf)
    m = jnp.max(s, axis=-1, keepdims=True)
    m = jnp.where(jnp.isfinite(m), m, 0.0)
    p = jnp.exp(s - m)
    return p / jnp.maximum(jnp.sum(p, axis=-1, keepdims=True), 1e-30)


def rope(t, pos):
    half = t.shape[-1] // 2
    inv = ROPE_THETA ** (-jnp.arange(half, dtype=jnp.float32) / half)
    ang = pos.astype(jnp.float32)[:, None] * inv[None, :]
    cos = jnp.cos(ang)[None, :, None, :]
    sin = jnp.sin(ang)[None, :, None, :]
    t1, t2 = t[..., :half], t[..., half:]
    return jnp.concatenate([t1 * cos - t2 * sin, t1 * sin + t2 * cos], axis=-1).astype(t.dtype)


def t5_bucket(dist):
    n = jnp.maximum(dist, 0)
    max_exact = REL_BUCKETS // 2
    nf = jnp.maximum(n, 1).astype(jnp.float32)
    large = max_exact + (jnp.log(nf / max_exact) / math.log(REL_MAX_DIST / max_exact)
                         * (REL_BUCKETS - max_exact)).astype(jnp.int32)
    large = jnp.minimum(large, REL_BUCKETS - 1)
    return jnp.where(n < max_exact, n, large)


def mla_mixer(h, w_dkv, g_q, g_kv, w_uq, w_ukv, w_o):
    bsz, seq, _ = h.shape
    c_q, c_kv, k_rope = _split(h @ w_dkv, (MLA_Q_LORA, MLA_KV_LORA, MLA_ROPE))
    c_q = rms_norm(c_q, g_q)
    c_kv = rms_norm(c_kv, g_kv)
    q = (c_q @ w_uq).reshape(bsz, seq, MLA_HEADS, MLA_QK)
    kv = (c_kv @ w_ukv).reshape(bsz, seq, MLA_HEADS, MLA_NOPE + MLA_V)
    pos = jnp.arange(seq)
    q = jnp.concatenate([q[..., :MLA_NOPE], rope(q[..., MLA_NOPE:], pos)], axis=-1)
    k_r = jnp.broadcast_to(rope(k_rope[:, :, None, :], pos), (bsz, seq, MLA_HEADS, MLA_ROPE))
    k = jnp.concatenate([kv[..., :MLA_NOPE], k_r], axis=-1).transpose(0, 2, 1, 3)
    v = kv[..., MLA_NOPE:].transpose(0, 2, 1, 3)
    nb = seq // Q_BLOCK
    q_blocks = q.reshape(bsz, nb, Q_BLOCK, MLA_HEADS, MLA_QK).transpose(1, 0, 3, 2, 4)
    scale = MLA_QK ** -0.5

    def attend(args):
        qb, i = args
        qpos = i * Q_BLOCK + jnp.arange(Q_BLOCK)
        s = jnp.einsum('bhqd,bhkd->bhqk', qb, k).astype(jnp.float32) * scale
        p = masked_softmax(s, pos[None, :] <= qpos[:, None])
        return jnp.einsum('bhqk,bhkd->bhqd', p.astype(v.dtype), v)

    o = lax.map(attend, (q_blocks, jnp.arange(nb)))
    o = o.transpose(1, 0, 3, 2, 4).reshape(bsz, seq, MLA_HEADS * MLA_V)
    return o @ w_o


def _compress(t, pe, w1, w2):
    bsz, ng, seq, d = t.shape
    n_chunk = seq // CMP_STRIDE
    nc = n_chunk - CMP_RATIO + 1
    ch = t.reshape(bsz, ng, n_chunk, CMP_STRIDE, d)
    blocks = jnp.concatenate([ch[:, :, r:r + nc] for r in range(CMP_RATIO)], axis=3)
    flat = (blocks + pe).reshape(bsz, ng, nc, CMP_BLOCK * d)
    return jax.nn.gelu(flat @ w1) @ w2


def nsa_mixer(h, w_in, pe_k, w1_k, w2_k, pe_v, w1_v, w2_v, rel_bias, w_o):
    bsz, seq, _ = h.shape
    G, HP = NSA_GROUPS, NSA_HPG
    q, k_c, v_c, k_s, v_s, k_w, v_w, gates = _split(h @ w_in, NSA_SPLITS)
    q = q.reshape(bsz, seq, G, HP, NSA_DK).transpose(0, 2, 3, 1, 4)

    def heads_kv(t):
        return t.reshape(bsz, seq, G, -1).transpose(0, 2, 1, 3)

    k_c, v_c, k_s, v_s, k_w, v_w = (heads_kv(t) for t in (k_c, v_c, k_s, v_s, k_w, v_w))
    pos = jnp.arange(seq)
    scale = NSA_DK ** -0.5
    rel_g = rel_bias.reshape(G, HP, REL_BUCKETS)

    kc = _compress(k_c, pe_k, w1_k, w2_k)
    vc = _compress(v_c, pe_v, w1_v, w2_v)
    nc = kc.shape[2]
    cmp_start = jnp.arange(nc) * CMP_STRIDE
    cmp_end = cmp_start + CMP_BLOCK - 1
    dist_c = pos[:, None] - cmp_end[None, :]
    bias_c = rel_bias[:, t5_bucket(dist_c)].reshape(G, HP, seq, nc)
    s_c = jnp.einsum('bghqd,bgkd->bghqk', q, kc).astype(jnp.float32) * scale + bias_c
    p_cmp = masked_softmax(s_c, dist_c >= 0)
    o_cmp = jnp.einsum('bghqk,bgkd->bghqd', p_cmp.astype(vc.dtype), vc)

    ns = seq // SLC_BLOCK
    n_sel = min(SLC_TOPN, ns)
    blk = jnp.arange(ns)
    blk_start = blk * SLC_BLOCK
    overlap = ((cmp_start[:, None] < blk_start[None, :] + SLC_BLOCK)
               & (cmp_start[:, None] + CMP_BLOCK > blk_start[None, :])).astype(jnp.float32)
    imp = jnp.einsum('bghqk,kj->bgqj', p_cmp, overlap)
    back = (pos // SLC_BLOCK)[:, None] - blk[None, :]
    forced = (blk[None, :] == 0) | ((back >= 0) & (back < N_LOCAL))
    causal_blk = blk_start[None, :] <= pos[:, None]
    score = jnp.where(causal_blk, imp + jnp.where(forced, FORCE_BONUS, 0.0), -jnp.inf)
    top_val, top_idx = lax.top_k(score, n_sel)
    top_ok = jnp.isfinite(top_val)

    ks_blocks = k_s.reshape(bsz, G, ns, SLC_BLOCK, NSA_DK)
    vs_blocks = v_s.reshape(bsz, G, ns, SLC_BLOCK, NSA_DV)
    gather = jax.vmap(jax.vmap(lambda blocks, ids: blocks[ids]))
    nq = seq // SLC_Q_CHUNK
    q_ch = q.reshape(bsz, G, HP, nq, SLC_Q_CHUNK, NSA_DK).transpose(3, 0, 1, 2, 4, 5)
    idx_ch = top_idx.reshape(bsz, G, nq, SLC_Q_CHUNK, n_sel).transpose(2, 0, 1, 3, 4)
    ok_ch = top_ok.reshape(bsz, G, nq, SLC_Q_CHUNK, n_sel).transpose(2, 0, 1, 3, 4)
    pos_ch = pos.reshape(nq, SLC_Q_CHUNK)
    n_keys = n_sel * SLC_BLOCK

    def select_chunk(args):
        qc, idc, okc, qpos = args
        kg = gather(ks_blocks, idc).reshape(bsz, G, SLC_Q_CHUNK, n_keys, NSA_DK)
        vg = gather(vs_blocks, idc).reshape(bsz, G, SLC_Q_CHUNK, n_keys, NSA_DV)
        kpos = (idc[..., None] * SLC_BLOCK + jnp.arange(SLC_BLOCK)).reshape(bsz, G, SLC_Q_CHUNK, n_keys)
        mask = jnp.repeat(okc, SLC_BLOCK, axis=-1) & (kpos <= qpos[:, None])
        bucket = t5_bucket(qpos[:, None] - kpos)
        bias = rel_g[jnp.arange(G)[:, None, None, None], jnp.arange(HP)[None, :, None, None],
                     bucket[:, :, None]]
        s = jnp.einsum('bghqd,bgqkd->bghqk', qc, kg).astype(jnp.float32) * scale + bias
        p = masked_softmax(s, mask[:, :, None])
        return jnp.einsum('bghqk,bgqkd->bghqd', p.astype(vg.dtype), vg)

    o_slc = lax.map(select_chunk, (q_ch, idx_ch, ok_ch, pos_ch))
    o_slc = o_slc.transpose(1, 2, 3, 0, 4, 5).reshape(bsz, G, HP, seq, NSA_DV)

    span = Q_BLOCK + WINDOW
    kwp = jnp.pad(k_w, ((0, 0), (0, 0), (WINDOW, 0), (0, 0)))
    vwp = jnp.pad(v_w, ((0, 0), (0, 0), (WINDOW, 0), (0, 0)))
    nb = seq // Q_BLOCK
    q_blocks = q.reshape(bsz, G, HP, nb, Q_BLOCK, NSA_DK).transpose(3, 0, 1, 2, 4, 5)

    def window_block(args):
        qb, i = args
        start = i * Q_BLOCK
        kb = lax.dynamic_slice_in_dim(kwp, start, span, axis=2)
        vb = lax.dynamic_slice_in_dim(vwp, start, span, axis=2)
        qpos = start + jnp.arange(Q_BLOCK)
        kpos = start - WINDOW + jnp.arange(span)
        dist = qpos[:, None] - kpos[None, :]
        mask = (dist >= 0) & (dist < WINDOW) & (kpos >= 0)[None, :]
        bias = rel_bias[:, t5_bucket(dist)].reshape(G, HP, Q_BLOCK, span)
        s = jnp.einsum('bghqd,bgkd->bghqk', qb, kb).astype(jnp.float32) * scale + bias
        p = masked_softmax(s, mask)
        return jnp.einsum('bghqk,bgkd->bghqd', p.astype(vb.dtype), vb)

    o_win = lax.map(window_block, (q_blocks, jnp.arange(nb)))
    o_win = o_win.transpose(1, 2, 3, 0, 4, 5).reshape(bsz, G, HP, seq, NSA_DV)

    g = jax.nn.sigmoid(gates.astype(jnp.float32)).reshape(bsz, seq, 3, G, HP)
    g = g.transpose(2, 0, 3, 4, 1)[..., None].astype(h.dtype)
    o = g[0] * o_cmp + g[1] * o_slc + g[2] * o_win
    o = o.transpose(0, 3, 1, 2, 4).reshape(bsz, seq, NSA_HEADS * NSA_DV)
    return o @ w_o


def _ssm_combine(e1, e2):
    a1r, a1i, b1r, b1i = e1
    a2r, a2i, b2r, b2i = e2
    return (a2r * a1r - a2i * a1i,
            a2r * a1i + a2i * a1r,
            a2r * b1r - a2i * b1i + b2r,
            a2r * b1i + a2i * b1r + b2i)


def s5_mixer(h, lam_re, lam_im, log_dt, b_re, b_im, c_re, c_im, d_skip, w_glu):
    bsz, seq, dm = h.shape
    f32 = jnp.float32
    lam_re, lam_im = lam_re.astype(f32), lam_im.astype(f32)
    b_re, b_im, c_re, c_im = b_re.astype(f32), b_im.astype(f32), c_re.astype(f32), c_im.astype(f32)
    u = h.astype(f32).reshape(bsz, seq, S5_NGROUPS, S5_GROUP)
    dt = jnp.exp(log_dt.astype(f32))[:, None]
    mag = jnp.exp(lam_re * dt)
    lb_re = mag * jnp.cos(lam_im * dt)
    lb_im = mag * jnp.sin(lam_im * dt)
    den = lam_re * lam_re + lam_im * lam_im
    f_re = ((lb_re - 1.0) * lam_re + lb_im * lam_im) / den
    f_im = (lb_im * lam_re - (lb_re - 1.0) * lam_im) / den
    bb_re = f_re[..., None] * b_re - f_im[..., None] * b_im
    bb_im = f_re[..., None] * b_im + f_im[..., None] * b_re
    bu_re = jnp.einsum('bsgi,gpi->sbgp', u, bb_re)
    bu_im = jnp.einsum('bsgi,gpi->sbgp', u, bb_im)
    a_shape = (seq, 1, S5_NGROUPS, S5_STATE)
    a_re = jnp.broadcast_to(lb_re[None, None], a_shape)
    a_im = jnp.broadcast_to(lb_im[None, None], a_shape)
    _, _, x_re, x_im = lax.associative_scan(_ssm_combine, (a_re, a_im, bu_re, bu_im), axis=0)
    y = (jnp.einsum('sbgp,gip->bsgi', x_re, c_re) - jnp.einsum('sbgp,gip->bsgi', x_im, c_im)
         + d_skip.astype(f32) * u)
    y = jax.nn.gelu(y.reshape(bsz, seq, dm)).astype(h.dtype)
    za, zb = jnp.split(y @ w_glu, 2, axis=-1)
    return za * jax.nn.sigmoid(zb)


def conv_glu_ffn(h, w_up, conv_w, conv_b, w_down):
    z = h @ w_up
    ch = z.shape[-1]
    z = lax.conv_general_dilated(z, conv_w[:, None, :].astype(z.dtype), window_strides=(1,),
                                 padding=[(CONV_WIDTH - 1, 0)],
                                 dimension_numbers=('NWC', 'WIO', 'NWC'),
                                 feature_group_count=ch) + conv_b
    val, gate = jnp.split(z, 2, axis=-1)
    return (val * jax.nn.silu(gate)) @ w_down


def _nrm(k, shape, scale):
    return jax.random.normal(k, shape, jnp.float32) * scale


def setup_inputs(seed: int = 0) -> dict:
    key = jax.random.key(seed)
    ks = jax.random.split(key, 40)
    D = D_MODEL
    NA, NB, NC = N_MLA_LAYERS, N_NSA_LAYERS, N_S5_LAYERS
    return {
        'x': _nrm(ks[0], (BATCH, SEQ, D), 1.0),
        'c': _nrm(ks[1], (BATCH, D), 1.0),
        'ada_w': _nrm(ks[2], (DEPTH, D, N_MOD * D), 0.5 * D ** -0.5),
        'ada_b': _nrm(ks[3], (DEPTH, N_MOD * D), 0.02),
        'norm1_g': 1.0 + _nrm(ks[4], (DEPTH, D), 0.02),
        'norm2_g': 1.0 + _nrm(ks[5], (DEPTH, D), 0.02),
        'final_g': 1.0 + _nrm(ks[6], (D,), 0.02),
        'rel_bias': _nrm(ks[7], (NSA_HEADS, REL_BUCKETS), 0.5),
        'mla_w_dkv': _nrm(ks[8], (NA, D, MLA_Q_LORA + MLA_KV_LORA + MLA_ROPE), D ** -0.5),
        'mla_g_q': 1.0 + _nrm(ks[9], (NA, MLA_Q_LORA), 0.02),
        'mla_g_kv': 1.0 + _nrm(ks[10], (NA, MLA_KV_LORA), 0.02),
        'mla_w_uq': _nrm(ks[11], (NA, MLA_Q_LORA, MLA_HEADS * MLA_QK), MLA_Q_LORA ** -0.5),
        'mla_w_ukv': _nrm(ks[12], (NA, MLA_KV_LORA, MLA_HEADS * (MLA_NOPE + MLA_V)), MLA_KV_LORA ** -0.5),
        'mla_w_o': _nrm(ks[13], (NA, MLA_HEADS * MLA_V, D), (MLA_HEADS * MLA_V) ** -0.5),
        'nsa_w_in': _nrm(ks[14], (NB, D, NSA_PROJ), D ** -0.5),
        'nsa_pe_k': _nrm(ks[15], (NB, CMP_BLOCK, NSA_DK), 0.1),
        'nsa_w1_k': _nrm(ks[16], (NB, CMP_BLOCK * NSA_DK, CMP_HIDDEN), (CMP_BLOCK * NSA_DK) ** -0.5),
        'nsa_w2_k': _nrm(ks[17], (NB, CMP_HIDDEN, NSA_DK), CMP_HIDDEN ** -0.5),
        'nsa_pe_v': _nrm(ks[18], (NB, CMP_BLOCK, NSA_DV), 0.1),
        'nsa_w1_v': _nrm(ks[19], (NB, CMP_BLOCK * NSA_DV, CMP_HIDDEN), (CMP_BLOCK * NSA_DV) ** -0.5),
        'nsa_w2_v': _nrm(ks[20], (NB, CMP_HIDDEN, NSA_DV), CMP_HIDDEN ** -0.5),
        'nsa_w_o': _nrm(ks[21], (NB, NSA_HEADS * NSA_DV, D), (NSA_HEADS * NSA_DV) ** -0.5),
        's5_lam_re': -0.5 + _nrm(ks[22], (NC, S5_NGROUPS, S5_STATE), 0.01),
        's5_lam_im': jnp.broadcast_to(math.pi * jnp.arange(S5_STATE, dtype=jnp.float32),
                                      (NC, S5_NGROUPS, S5_STATE)),
        's5_log_dt': jax.random.uniform(ks[23], (NC, S5_NGROUPS), jnp.float32,
                                        math.log(DT_MIN), math.log(DT_MAX)),
        's5_b_re': _nrm(ks[24], (NC, S5_NGROUPS, S5_STATE, S5_GROUP), (2 * S5_GROUP) ** -0.5),
        's5_b_im': _nrm(ks[25], (NC, S5_NGROUPS, S5_STATE, S5_GROUP), (2 * S5_GROUP) ** -0.5),
        's5_c_re': _nrm(ks[26], (NC, S5_NGROUPS, S5_GROUP, S5_STATE), S5_STATE ** -0.5),
        's5_c_im': _nrm(ks[27], (NC, S5_NGROUPS, S5_GROUP, S5_STATE), S5_STATE ** -0.5),
        's5_d': _nrm(ks[28], (NC, S5_NGROUPS, S5_GROUP), 1.0),
        's5_w_glu': _nrm(ks[29], (NC, D, 2 * D), D ** -0.5),
        'ffn_w_up': _nrm(ks[30], (DEPTH, D, 2 * D_FF), D ** -0.5),
        'ffn_conv_w': _nrm(ks[31], (DEPTH, CONV_WIDTH, 2 * D_FF), CONV_WIDTH ** -0.5),
        'ffn_conv_b': _nrm(ks[32], (DEPTH, 2 * D_FF), 0.02),
        'ffn_w_down': _nrm(ks[33], (DEPTH, D_FF, D), D_FF ** -0.5),
    }


def reference(x, c, ada_w, ada_b, norm1_g, norm2_g, final_g, rel_bias,
              mla_w_dkv, mla_g_q, mla_g_kv, mla_w_uq, mla_w_ukv, mla_w_o,
              nsa_w_in, nsa_pe_k, nsa_w1_k, nsa_w2_k, nsa_pe_v, nsa_w1_v, nsa_w2_v, nsa_w_o,
              s5_lam_re, s5_lam_im, s5_log_dt, s5_b_re, s5_b_im, s5_c_re, s5_c_im, s5_d, s5_w_glu,
              ffn_w_up, ffn_conv_w, ffn_conv_b, ffn_w_down):
    bsz = x.shape[0]
    c_act = jax.nn.silu(c)
    for i in range(DEPTH):
        mod = (c_act @ ada_w[i] + ada_b[i]).reshape(bsz, N_MOD, D_MODEL)[:, :, None, :]
        shift1, scale1, gate1 = mod[:, 0], mod[:, 1], mod[:, 2]
        shift2, scale2, gate2 = mod[:, 3], mod[:, 4], mod[:, 5]
        h = rms_norm(x, norm1_g[i]) * (1.0 + scale1) + shift1
        kind, j = i % N_MIXERS, i // N_MIXERS
        if kind == 0:
            y = mla_mixer(h, mla_w_dkv[j], mla_g_q[j], mla_g_kv[j], mla_w_uq[j], mla_w_ukv[j], mla_w_o[j])
        elif kind == 1:
            y = nsa_mixer(h, nsa_w_in[j], nsa_pe_k[j], nsa_w1_k[j], nsa_w2_k[j],
                          nsa_pe_v[j], nsa_w1_v[j], nsa_w2_v[j], rel_bias, nsa_w_o[j])
        else:
            y = s5_mixer(h, s5_lam_re[j], s5_lam_im[j], s5_log_dt[j], s5_b_re[j], s5_b_im[j],
                         s5_c_re[j], s5_c_im[j], s5_d[j], s5_w_glu[j])
        x = x + gate1 * y
        h = rms_norm(x, norm2_g[i]) * (1.0 + scale2) + shift2
        x = x + gate2 * conv_glu_ffn(h, ffn_w_up[i], ffn_conv_w[i], ffn_conv_b[i], ffn_w_down[i])
    return rms_norm(x, final_g)
```

```python
import functools
import math

import jax
import jax.numpy as jnp
from jax import lax
from jax.experimental import pallas as pl
from jax.experimental.pallas import tpu as pltpu

F32 = jnp.float32
BF16 = jnp.bfloat16

NORM_EPS = 1e-6
N_MOD = 6
N_MIXERS = 3

MLA_HEADS = 8
MLA_Q_LORA = 512
MLA_KV_LORA = 256
MLA_NOPE = 128
MLA_ROPE = 64
MLA_V = 128
MLA_QK = MLA_NOPE + MLA_ROPE
ROPE_THETA = 10000.0

NSA_HEADS = 8
NSA_GROUPS = 2
NSA_HPG = NSA_HEADS // NSA_GROUPS
NSA_DK = 128
NSA_DV = 128
CMP_BLOCK = 32
CMP_STRIDE = 16
CMP_RATIO = CMP_BLOCK // CMP_STRIDE
CMP_HIDDEN = 256
SLC_BLOCK = 64
SLC_TOPN = 16
N_LOCAL = 2
FORCE_BONUS = 1e4
WINDOW = 512
REL_BUCKETS = 32
REL_MAX_DIST = 128

S5_GROUP = 16
S5_STATE = 64
CONV_WIDTH = 3

LANE = 128
NEG = -1e30
VMEM_LIMIT = 56 * 1024 * 1024

ATT_TILE = 128
MLA_TQ = 256
S5_BB = 8
S5_SLAB = 8


def _cparams(n_axes):
    return pltpu.CompilerParams(
        dimension_semantics=("arbitrary",) * n_axes,
        vmem_limit_bytes=VMEM_LIMIT)


def _const_spec(shape):
    zeros = (0,) * len(shape)
    return pl.BlockSpec(shape, lambda *_: zeros, pipeline_mode=pl.Buffered(1))


def _dot(a, b):
    return jnp.dot(a, b, preferred_element_type=F32)


def _dot_nt(a, b):
    return lax.dot_general(a, b, (((1,), (1,)), ((), ())), preferred_element_type=F32)


def _rms(x, g):
    return x * lax.rsqrt(jnp.mean(x * x, axis=-1, keepdims=True) + NORM_EPS) * g


def _norm_mod(x, g, shift, scale):
    return _rms(x, g) * (1.0 + scale) + shift


def _gelu_tanh(x):
    c = math.sqrt(2.0 / math.pi)
    return 0.5 * x * (1.0 + jnp.tanh(c * (x + 0.044715 * (x * x * x))))


def _sigmoid(x):
    return 1.0 / (1.0 + jnp.exp(-x))


def _t5_bucket(dist):
    n = jnp.maximum(dist, 0)
    max_exact = REL_BUCKETS // 2
    nf = jnp.maximum(n, 1).astype(F32)
    large = max_exact + (jnp.log(nf / max_exact) / math.log(REL_MAX_DIST / max_exact)
                         * (REL_BUCKETS - max_exact)).astype(jnp.int32)
    large = jnp.minimum(large, REL_BUCKETS - 1)
    return jnp.where(n < max_exact, n, large)


def _ada_kernel(c_ref, w_ref, b_ref, o_ref):
    c = c_ref[...]
    c_act = c * _sigmoid(c)
    o_ref[0] = jnp.dot(c_act, w_ref[0], preferred_element_type=F32,
                       precision=lax.Precision.HIGHEST) + b_ref[0]


def _ada_mod(c, ada_w, ada_b):
    depth, d, n = ada_w.shape
    bsz = c.shape[0]
    tn = 1024
    out = pl.pallas_call(
        _ada_kernel,
        out_shape=jax.ShapeDtypeStruct((depth, bsz, n), F32),
        grid=(depth, n // tn),
        in_specs=[pl.BlockSpec((bsz, d), lambda i, j: (0, 0)),
                  pl.BlockSpec((1, d, tn), lambda i, j: (i, 0, j)),
                  pl.BlockSpec((1, 1, tn), lambda i, j: (i, 0, j))],
        out_specs=pl.BlockSpec((1, bsz, tn), lambda i, j: (i, 0, j)),
        compiler_params=_cparams(2),
        name="ada_mod",
    )(c, ada_w, ada_b.reshape(depth, 1, n))
    return out.reshape(depth, bsz, N_MOD, d)


def _post_ffn_kernel(x_ref, o_ref, mod_ref, wpost_ref, g2_ref, wup_ref, cw_ref, cb_ref,
                     wdown_ref, gf_ref, out_ref, act_ref, carry_ref,
                     *, glu, final, fc):
    tm = x_ref.shape[1]
    d_ff = wdown_ref.shape[0]
    x = x_ref[0]
    mod = mod_ref[0]
    y = _dot(o_ref[0], wpost_ref[...])
    if glu:
        d = x.shape[-1]
        y = y[:, :d] * _sigmoid(y[:, d:])
    x1 = x + mod[2:3] * y
    h = _norm_mod(x1, g2_ref[...], mod[3:4], mod[4:5]).astype(BF16)

    @pl.when(pl.program_id(1) == 0)
    def _():
        carry_ref[...] = jnp.zeros_like(carry_ref)

    def conv(z, col):
        zfull = jnp.concatenate([carry_ref[:, col:col + fc], z], axis=0)
        carry_ref[:, col:col + fc] = z[tm - 8:, :]
        w = cw_ref[:, col:col + fc]
        return (w[0:1] * zfull[6:6 + tm] + w[1:2] * zfull[7:7 + tm] + w[2:3] * z
                + cb_ref[:, col:col + fc])

    for f in range(0, d_ff, fc):
        val = conv(_dot(h, wup_ref[:, f:f + fc]), f)
        gate = conv(_dot(h, wup_ref[:, d_ff + f:d_ff + f + fc]), d_ff + f)
        act_ref[:, f:f + fc] = (val * (gate * _sigmoid(gate))).astype(BF16)

    x2 = x1 + mod[5:6] * _dot(act_ref[...], wdown_ref[...])
    if final:
        x2 = _rms(x2, gf_ref[...])
    out_ref[0] = x2


def _post_ffn(x, o, mod, w_post, g2, w_up, conv_w, conv_b, w_down, final_g, *, glu, final):
    bsz, seq, d = x.shape
    d_ff = w_down.shape[0]
    tm = 512
    fc = 256
    row = lambda b, i: (b, i, 0)
    return pl.pallas_call(
        functools.partial(_post_ffn_kernel, glu=glu, final=final, fc=fc),
        out_shape=jax.ShapeDtypeStruct((bsz, seq, d), F32),
        grid=(bsz, seq // tm),
        in_specs=[pl.BlockSpec((1, tm, d), row),
                  pl.BlockSpec((1, tm, d), row),
                  pl.BlockSpec((1, N_MOD, d), lambda b, i: (b, 0, 0)),
                  _const_spec(w_post.shape),
                  _const_spec((1, d)),
                  _const_spec(w_up.shape),
                  _const_spec(conv_w.shape),
                  _const_spec((1, 2 * d_ff)),
                  _const_spec(w_down.shape),
                  _const_spec((1, d))],
        out_specs=pl.BlockSpec((1, tm, d), row),
        scratch_shapes=[pltpu.VMEM((tm, d_ff), BF16),
                        pltpu.VMEM((8, 2 * d_ff), F32)],
        compiler_params=_cparams(2),
        name="post_ffn",
    )(x, o, mod, w_post, g2.reshape(1, d), w_up, conv_w, conv_b.reshape(1, -1), w_down,
      final_g.reshape(1, d))


def _rope_tables(seq, reps):
    half = MLA_ROPE // 2
    inv = ROPE_THETA ** (-jnp.arange(half, dtype=F32) / half)
    ang = jnp.arange(seq).astype(F32)[:, None] * inv[None, :]
    cos, sin = jnp.cos(ang), jnp.sin(ang)
    cos64 = jnp.concatenate([cos, cos], axis=1)
    sin64 = jnp.concatenate([-sin, sin], axis=1)
    return jnp.tile(cos64, (1, reps)), jnp.tile(sin64, (1, reps))


def _mla_proj_kernel(x_ref, mod_ref, g1_ref, wdkv_ref, gq_ref, gkv_ref, wuq_ref, wukv_ref,
                     cos_ref, sin_ref, qn_ref, qr_ref, kn_ref, kr_ref, v_ref):
    nq = MLA_HEADS * MLA_NOPE
    nr = MLA_HEADS * MLA_ROPE
    scale = MLA_QK ** -0.5
    mod = mod_ref[0]
    h = _norm_mod(x_ref[0], g1_ref[...], mod[0:1], mod[1:2]).astype(BF16)
    dkv = _dot(h, wdkv_ref[...])
    c_q = _rms(dkv[:, :MLA_Q_LORA], gq_ref[...]).astype(BF16)
    c_kv = _rms(dkv[:, MLA_Q_LORA:MLA_Q_LORA + MLA_KV_LORA], gkv_ref[...]).astype(BF16)
    cos = cos_ref[...]
    sin = sin_ref[...]
    r0 = MLA_Q_LORA + MLA_KV_LORA
    kr_ref[0] = (dkv[:, r0:r0 + LANE] * cos[:, :LANE]
                 + dkv[:, r0 + LANE:r0 + 2 * LANE] * sin[:, :LANE]).astype(BF16)
    q = _dot(c_q, wuq_ref[...])
    qn_ref[0] = (q[:, :nq] * scale).astype(BF16)
    qr_ref[0] = ((q[:, nq:nq + nr] * cos + q[:, nq + nr:] * sin) * scale).astype(BF16)
    kv = _dot(c_kv, wukv_ref[...])
    kn_ref[0] = kv[:, :nq].astype(BF16)
    v_ref[0] = kv[:, nq:].astype(BF16)


def _mla_attn_kernel(qn_ref, qr_ref, kn_ref, kr_ref, v_ref, o_ref):
    seq = qn_ref.shape[1]
    tq = MLA_TQ
    odd = pl.program_id(1) % 2
    lane = lax.broadcasted_iota(jnp.int32, (tq, LANE), 1)
    keep = (lane >= MLA_ROPE) == (odd == 1)
    row = lax.broadcasted_iota(jnp.int32, (tq, tq), 0)
    col = lax.broadcasted_iota(jnp.int32, (tq, tq), 1)
    for i in range(seq // tq):
        lo, hi = i * tq, (i + 1) * tq
        qr = jnp.where(keep, qr_ref[0, lo:hi, :], jnp.zeros((), BF16))
        q = jnp.concatenate([qn_ref[0, lo:hi, :], qr], axis=1)
        k_d = jnp.concatenate([kn_ref[0, lo:hi, :], kr_ref[0, lo:hi, :]], axis=1)
        s_d = jnp.where(col <= row, _dot_nt(q, k_d), NEG)
        m = jnp.max(s_d, axis=1, keepdims=True)
        if i > 0:
            k_o = jnp.concatenate([kn_ref[0, :lo, :], kr_ref[0, :lo, :]], axis=1)
            s_o = _dot_nt(q, k_o)
            m = jnp.maximum(m, jnp.max(s_o, axis=1, keepdims=True))
            p_o = jnp.exp(s_o - m)
        p_d = jnp.exp(s_d - m)
        l = jnp.sum(p_d, axis=1, keepdims=True)
        o = _dot(p_d.astype(BF16), v_ref[0, lo:hi, :])
        if i > 0:
            l = l + jnp.sum(p_o, axis=1, keepdims=True)
            o = o + _dot(p_o.astype(BF16), v_ref[0, :lo, :])
        o_ref[0, lo:hi, :] = (o / l).astype(BF16)


def _mla_mixer(x, mod, g1, w_dkv, g_q, g_kv, w_uq, w_ukv):
    bsz, seq, d = x.shape
    hds = MLA_HEADS
    half = MLA_ROPE // 2
    swap = jnp.concatenate([jnp.arange(half, MLA_ROPE), jnp.arange(half)])
    r0 = MLA_Q_LORA + MLA_KV_LORA
    kr = w_dkv[:, r0:]
    wdkv = jnp.concatenate([w_dkv[:, :r0], kr, kr, kr[:, swap], kr[:, swap]], axis=1).astype(BF16)
    wq = w_uq.reshape(MLA_Q_LORA, hds, MLA_QK)
    wq_r = wq[:, :, MLA_NOPE:]
    wuq = jnp.concatenate([wq[:, :, :MLA_NOPE].reshape(MLA_Q_LORA, -1),
                           wq_r.reshape(MLA_Q_LORA, -1),
                           wq_r[:, :, swap].reshape(MLA_Q_LORA, -1)], axis=1).astype(BF16)
    wkv = w_ukv.reshape(MLA_KV_LORA, hds, MLA_NOPE + MLA_V)
    wukv = jnp.concatenate([wkv[:, :, :MLA_NOPE].reshape(MLA_KV_LORA, -1),
                            wkv[:, :, MLA_NOPE:].reshape(MLA_KV_LORA, -1)], axis=1).astype(BF16)
    cos, sin = _rope_tables(seq, hds)
    nq, nr = hds * MLA_NOPE, hds * MLA_ROPE
    tm = 512
    row = lambda i, b: (b, i, 0)
    qn, qr, kn, kr2, v = pl.pallas_call(
        _mla_proj_kernel,
        out_shape=[jax.ShapeDtypeStruct((bsz, seq, nq), BF16),
                   jax.ShapeDtypeStruct((bsz, seq, nr), BF16),
                   jax.ShapeDtypeStruct((bsz, seq, nq), BF16),
                   jax.ShapeDtypeStruct((bsz, seq, LANE), BF16),
                   jax.ShapeDtypeStruct((bsz, seq, nq), BF16)],
        grid=(seq // tm, bsz),
        in_specs=[pl.BlockSpec((1, tm, d), row),
                  pl.BlockSpec((1, N_MOD, d), lambda i, b: (b, 0, 0)),
                  _const_spec((1, d)),
                  _const_spec(wdkv.shape),
                  _const_spec((1, MLA_Q_LORA)),
                  _const_spec((1, MLA_KV_LORA)),
                  _const_spec(wuq.shape),
                  _const_spec(wukv.shape),
                  pl.BlockSpec((tm, nr), lambda i, b: (i, 0)),
                  pl.BlockSpec((tm, nr), lambda i, b: (i, 0))],
        out_specs=[pl.BlockSpec((1, tm, nq), row),
                   pl.BlockSpec((1, tm, nr), row),
                   pl.BlockSpec((1, tm, nq), row),
                   pl.BlockSpec((1, tm, LANE), row),
                   pl.BlockSpec((1, tm, nq), row)],
        compiler_params=_cparams(2),
        name="mla_proj",
    )(x, mod, g1.reshape(1, d), wdkv, g_q.reshape(1, -1), g_kv.reshape(1, -1), wuq, wukv, cos, sin)

    head = lambda b, h: (b, 0, h)
    return pl.pallas_call(
        _mla_attn_kernel,
        out_shape=jax.ShapeDtypeStruct((bsz, seq, hds * MLA_V), BF16),
        grid=(bsz, hds),
        in_specs=[pl.BlockSpec((1, seq, LANE), head),
                  pl.BlockSpec((1, seq, LANE), lambda b, h: (b, 0, h // 2)),
                  pl.BlockSpec((1, seq, LANE), head),
                  pl.BlockSpec((1, seq, LANE), lambda b, h: (b, 0, 0)),
                  pl.BlockSpec((1, seq, LANE), head)],
        out_specs=pl.BlockSpec((1, seq, LANE), head),
        compiler_params=_cparams(2),
        name="mla_attn",
    )(qn, qr, kn, kr2, v)


def _nsa_proj_kernel(x_ref, mod_ref, g1_ref, w_ref, q_ref, kc_ref, vc_ref, ks_ref, vs_ref,
                     kw_ref, vw_ref, gate_ref):
    nq = NSA_HEADS * NSA_DK
    nkv = NSA_GROUPS * NSA_DK
    mod = mod_ref[0]
    h = _norm_mod(x_ref[0], g1_ref[...], mod[0:1], mod[1:2]).astype(BF16)
    p = _dot(h, w_ref[...])
    q_ref[0] = (p[:, :nq] * (NSA_DK ** -0.5)).astype(BF16)
    for n, ref in enumerate((kc_ref, vc_ref, ks_ref, vs_ref, kw_ref, vw_ref)):
        ref[0] = p[:, nq + n * nkv:nq + (n + 1) * nkv].astype(BF16)
    gate_ref[0] = _sigmoid(p[:, nq + 6 * nkv:])


def _nsa_compress_kernel(fk_ref, fv_ref, pek_ref, pev_ref, w1k_ref, w1v_ref, w2k_ref, w2v_ref,
                         ok_ref, ov_ref):
    for f_ref, pe_ref, w1_ref, w2_ref, o_ref in ((fk_ref, pek_ref, w1k_ref, w2k_ref, ok_ref),
                                                 (fv_ref, pev_ref, w1v_ref, w2v_ref, ov_ref)):
        flat = (f_ref[...].astype(F32) + pe_ref[...]).astype(BF16)
        hid = _gelu_tanh(_dot(flat, w1_ref[...])).astype(BF16)
        o_ref[...] = _dot(hid, w2_ref[...]).astype(BF16)


def _nsa_select_kernel(rb_ref, q_ref, kc_ref, vc_ref, bucket_ref, ovl_ref, bonus_ref,
                       ocmp_ref, sel_ref, bias_ref):
    g = pl.program_id(0)
    seq = q_ref.shape[1]
    n_blk = bonus_ref.shape[0]
    bucket = bucket_ref[...]

    @pl.when(pl.program_id(1) == 0)
    def _():
        for hp in range(NSA_HPG):
            acc = jnp.zeros(bucket.shape, F32)
            for r in range(REL_BUCKETS):
                acc = jnp.where(bucket == r, rb_ref[g * NSA_HPG + hp, r], acc)
            bias_ref[hp] = acc

    valid = bucket >= 0
    kc = kc_ref[0, 0]
    vc = vc_ref[0, 0]
    psum = jnp.zeros(bucket.shape, F32)
    for hp in range(NSA_HPG):
        s = _dot_nt(q_ref[0, :, hp * NSA_DK:(hp + 1) * NSA_DK], kc) + bias_ref[hp]
        s = jnp.where(valid, s, NEG)
        m = jnp.max(s, axis=1, keepdims=True)
        m = jnp.where(m > 0.5 * NEG, m, 0.0)
        p = jnp.exp(s - m)
        p = p / jnp.maximum(jnp.sum(p, axis=1, keepdims=True), 1e-30)
        ocmp_ref[0, :, hp * NSA_DV:(hp + 1) * NSA_DV] = _dot(p.astype(BF16), vc).astype(BF16)
        psum = psum + p

    imp = lax.dot_general(ovl_ref[...], psum, (((1,), (1,)), ((), ())),
                          preferred_element_type=F32, precision=lax.Precision.HIGHEST)
    bonus = bonus_ref[...]
    score = jnp.where(bonus > 0.5 * NEG, imp + bonus, NEG)
    blk = lax.broadcasted_iota(jnp.int32, (n_blk, seq), 0)
    rank = jnp.zeros((n_blk, seq), F32)
    for i in range(n_blk):
        row = score[i:i + 1, :]
        rank = rank + jnp.where(blk > i, jnp.where(row >= score, 1.0, 0.0),
                                jnp.where(row > score, 1.0, 0.0))
    selneg = jnp.where(rank < float(SLC_TOPN), jnp.where(score > 0.5 * NEG, 0.0, NEG), NEG)
    pad = jnp.zeros((LANE - n_blk, seq), F32)
    sel_ref[0, 0] = jnp.concatenate([selneg, pad], axis=0).T.astype(BF16)


def _nsa_attn_kernel(rb_ref, q_ref, sel_ref, ks_ref, vs_ref, kw_ref, vw_ref, et_ref, ocmp_ref,
                     gate_ref, bkt_ref, o_ref, kaug_ref, bias_ref, m_ref, l_ref, acc_ref):
    g = pl.program_id(1)
    i = pl.program_id(2)
    t = ATT_TILE
    hp_n = NSA_HPG
    rows = hp_n * t

    @pl.when(i == 0)
    def _():
        kaug_ref[:, :NSA_DK] = ks_ref[0]
        kaug_ref[:, NSA_DK:] = et_ref[...]
        for kind in range(2):
            bkt = bkt_ref[kind]
            for hp in range(hp_n):
                hd = g * hp_n + hp
                acc = jnp.zeros((t, t), F32)
                for r in range(REL_BUCKETS - 1):
                    acc = jnp.where(bkt == r, rb_ref[hd, r] - rb_ref[hd, REL_BUCKETS - 1], acc)
                bias_ref[kind, hp * t:(hp + 1) * t, :] = acc

    r_io = lax.broadcasted_iota(jnp.int32, (t, t), 0)
    c_io = lax.broadcasted_iota(jnp.int32, (t, t), 1)
    causal = jnp.concatenate([c_io <= r_io] * hp_n, axis=0)
    edge = jnp.concatenate([c_io > r_io] * hp_n, axis=0)

    q_heads = [q_ref[0, :, hp * NSA_DK:(hp + 1) * NSA_DK] for hp in range(hp_n)]
    q_win = jnp.concatenate(q_heads, axis=0)
    sel = sel_ref[0, 0]
    q_sel = jnp.concatenate([jnp.concatenate([qh, sel], axis=1) for qh in q_heads], axis=0)

    def reset():
        m_ref[...] = jnp.full(m_ref.shape, NEG, F32)
        l_ref[...] = jnp.zeros(l_ref.shape, F32)
        acc_ref[...] = jnp.zeros(acc_ref.shape, F32)

    def step(q, k_ref, v_ref, j, bias_kind=None, mask=None):
        off = pl.multiple_of(j * t, t)
        s = _dot_nt(q, k_ref[pl.ds(off, t), :])
        if bias_kind is not None:
            s = s + bias_ref[bias_kind]
        if mask is not None:
            s = jnp.where(mask, s, NEG)
        m_old = m_ref[...]
        m_new = jnp.maximum(m_old, jnp.max(s, axis=1, keepdims=True))
        alpha = jnp.exp(m_old - m_new)
        p = jnp.exp(s - m_new)
        l_ref[...] = alpha * l_ref[...] + jnp.sum(p, axis=1, keepdims=True)
        acc_ref[...] = alpha * acc_ref[...] + _dot(p.astype(BF16), v_ref[pl.ds(off, t), :])
        m_ref[...] = m_new

    def result():
        return acc_ref[...] / l_ref[...]

    reset()
    step(q_sel, kaug_ref, vs_ref.at[0], i, bias_kind=0, mask=causal)

    @pl.when(i >= 1)
    def _():
        step(q_sel, kaug_ref, vs_ref.at[0], i - 1, bias_kind=1)

    def far(j, carry):
        step(q_sel, kaug_ref, vs_ref.at[0], j)
        return carry
    lax.fori_loop(0, jnp.maximum(i - 1, 0), far, 0)
    o_slc = result()

    reset()
    step(q_win, kw_ref.at[0], vw_ref.at[0], i, bias_kind=0, mask=causal)

    @pl.when(i >= 1)
    def _():
        step(q_win, kw_ref.at[0], vw_ref.at[0], i - 1, bias_kind=1)
    n_far = WINDOW // t - 2
    for back in range(2, 2 + n_far):
        @pl.when(i >= back)
        def _(back=back):
            step(q_win, kw_ref.at[0], vw_ref.at[0], i - back)

    @pl.when(i >= WINDOW // t)
    def _():
        step(q_win, kw_ref.at[0], vw_ref.at[0], i - WINDOW // t, mask=edge)
    o_win = result()

    gate = gate_ref[0]
    for hp in range(hp_n):
        hd = g * hp_n + hp
        lane = lax.broadcasted_iota(jnp.int32, gate.shape, 1)

        def gcol(branch):
            return jnp.sum(jnp.where(lane == branch * NSA_HEADS + hd, gate, 0.0),
                           axis=1, keepdims=True)
        sl = slice(hp * t, (hp + 1) * t)
        o = (gcol(0) * ocmp_ref[0, :, hp * NSA_DV:(hp + 1) * NSA_DV].astype(F32)
             + gcol(1) * o_slc[sl] + gcol(2) * o_win[sl])
        o_ref[0, :, hp * NSA_DV:(hp + 1) * NSA_DV] = o.astype(BF16)


def _nsa_mixer(x, mod, g1, w_in, pe_k, w1_k, w2_k, pe_v, w1_v, w2_v, rel_bias):
    bsz, seq, d = x.shape
    grp, hpg = NSA_GROUPS, NSA_HPG
    nq, nkv = NSA_HEADS * NSA_DK, NSA_GROUPS * NSA_DK
    n_gate = 3 * NSA_HEADS
    w = jnp.pad(w_in, ((0, 0), (0, LANE - n_gate))).astype(BF16)
    tm = 512
    row = lambda i, b: (b, i, 0)
    kv_shape = jax.ShapeDtypeStruct((bsz, seq, nkv), BF16)
    q, kc, vc, ks, vs, kw, vw, gates = pl.pallas_call(
        _nsa_proj_kernel,
        out_shape=[jax.ShapeDtypeStruct((bsz, seq, nq), BF16)] + [kv_shape] * 6
                  + [jax.ShapeDtypeStruct((bsz, seq, LANE), F32)],
        grid=(seq // tm, bsz),
        in_specs=[pl.BlockSpec((1, tm, d), row),
                  pl.BlockSpec((1, N_MOD, d), lambda i, b: (b, 0, 0)),
                  _const_spec((1, d)),
                  _const_spec(w.shape)],
        out_specs=[pl.BlockSpec((1, tm, nq), row)] + [pl.BlockSpec((1, tm, nkv), row)] * 6
                  + [pl.BlockSpec((1, tm, LANE), row)],
        compiler_params=_cparams(2),
        name="nsa_proj",
    )(x, mod, g1.reshape(1, d), w)

    n_chunk = seq // CMP_STRIDE
    n_cmp = n_chunk - CMP_RATIO + 1

    def blocks(t):
        ch = t.reshape(bsz, n_chunk, CMP_STRIDE, grp, NSA_DK).transpose(0, 3, 1, 2, 4)
        ch = ch.reshape(bsz, grp, n_chunk, CMP_STRIDE * NSA_DK)
        ch = jnp.pad(ch, ((0, 0), (0, 0), (0, CMP_RATIO - 1), (0, 0)))
        flat = jnp.concatenate([ch[:, :, r:r + n_chunk] for r in range(CMP_RATIO)], axis=3)
        return flat.reshape(bsz * grp * n_chunk, CMP_BLOCK * NSA_DK)

    fk, fv = blocks(kc), blocks(vc)
    tr = min(1024, bsz * grp * n_chunk)
    wide = CMP_BLOCK * NSA_DK
    kcmp, vcmp = pl.pallas_call(
        _nsa_compress_kernel,
        out_shape=[jax.ShapeDtypeStruct((bsz * grp * n_chunk, NSA_DK), BF16)] * 2,
        grid=(bsz * grp * n_chunk // tr,),
        in_specs=[pl.BlockSpec((tr, wide), lambda i: (i, 0)),
                  pl.BlockSpec((tr, wide), lambda i: (i, 0)),
                  _const_spec((1, wide)), _const_spec((1, wide)),
                  _const_spec((wide, CMP_HIDDEN)), _const_spec((wide, CMP_HIDDEN)),
                  _const_spec((CMP_HIDDEN, NSA_DK)), _const_spec((CMP_HIDDEN, NSA_DV))],
        out_specs=[pl.BlockSpec((tr, NSA_DK), lambda i: (i, 0))] * 2,
        compiler_params=_cparams(1),
        name="nsa_compress",
    )(fk, fv, pe_k.reshape(1, wide), pe_v.reshape(1, wide), w1_k.astype(BF16), w1_v.astype(BF16),
      w2_k.astype(BF16), w2_v.astype(BF16))
    kcmp = kcmp.reshape(bsz, grp, n_chunk, NSA_DK)
    vcmp = vcmp.reshape(bsz, grp, n_chunk, NSA_DV)

    pos = jnp.arange(seq)
    cmp_start = jnp.arange(n_chunk) * CMP_STRIDE
    dist_c = pos[:, None] - (cmp_start + CMP_BLOCK - 1)[None, :]
    ok_c = (dist_c >= 0) & (jnp.arange(n_chunk) < n_cmp)[None, :]
    bucket_c = jnp.where(ok_c, _t5_bucket(dist_c), -1).astype(jnp.int32)
    n_blk = seq // SLC_BLOCK
    blk = jnp.arange(n_blk)
    blk_start = blk * SLC_BLOCK
    overlap_t = ((cmp_start[None, :] < blk_start[:, None] + SLC_BLOCK)
                 & (cmp_start[None, :] + CMP_BLOCK > blk_start[:, None])
                 & (jnp.arange(n_chunk) < n_cmp)[None, :]).astype(F32)
    back = (pos // SLC_BLOCK)[None, :] - blk[:, None]
    forced = (blk[:, None] == 0) | ((back >= 0) & (back < N_LOCAL))
    causal_blk = blk_start[:, None] <= pos[None, :]
    bonus_t = jnp.where(causal_blk, jnp.where(forced, FORCE_BONUS, 0.0), NEG).astype(F32)

    ocmp, selneg = pl.pallas_call(
        _nsa_select_kernel,
        out_shape=[jax.ShapeDtypeStruct((bsz, seq, nq), BF16),
                   jax.ShapeDtypeStruct((bsz, grp, seq, LANE), BF16)],
        grid=(grp, bsz),
        in_specs=[pl.BlockSpec(memory_space=pltpu.SMEM),
                  pl.BlockSpec((1, seq, hpg * NSA_DK), lambda g, b: (b, 0, g)),
                  pl.BlockSpec((1, 1, n_chunk, NSA_DK), lambda g, b: (b, g, 0, 0)),
                  pl.BlockSpec((1, 1, n_chunk, NSA_DV), lambda g, b: (b, g, 0, 0)),
                  _const_spec(bucket_c.shape),
                  _const_spec(overlap_t.shape),
                  _const_spec(bonus_t.shape)],
        out_specs=[pl.BlockSpec((1, seq, hpg * NSA_DV), lambda g, b: (b, 0, g)),
                   pl.BlockSpec((1, 1, seq, LANE), lambda g, b: (b, g, 0, 0))],
        scratch_shapes=[pltpu.VMEM((hpg, seq, n_chunk), F32)],
        compiler_params=_cparams(2),
        name="nsa_select",
    )(rel_bias, q, kcmp, vcmp, bucket_c, overlap_t, bonus_t)

    t = ATT_TILE
    rr = jnp.arange(t)
    near = jnp.stack([_t5_bucket(k * t + rr[:, None] - rr[None, :]) for k in range(2)]).astype(jnp.int32)
    e_t = (jnp.arange(seq)[:, None] // SLC_BLOCK == jnp.arange(LANE)[None, :]).astype(BF16)
    grp_blk = lambda b, g, i: (b, 0, g)
    q_blk = lambda b, g, i: (b, i, g)
    rows = hpg * t
    return pl.pallas_call(
        _nsa_attn_kernel,
        out_shape=jax.ShapeDtypeStruct((bsz, seq, nq), BF16),
        grid=(bsz, grp, seq // t),
        in_specs=[pl.BlockSpec(memory_space=pltpu.SMEM),
                  pl.BlockSpec((1, t, hpg * NSA_DK), q_blk),
                  pl.BlockSpec((1, 1, t, LANE), lambda b, g, i: (b, g, i, 0)),
                  pl.BlockSpec((1, seq, NSA_DK), grp_blk),
                  pl.BlockSpec((1, seq, NSA_DV), grp_blk),
                  pl.BlockSpec((1, seq, NSA_DK), grp_blk),
                  pl.BlockSpec((1, seq, NSA_DV), grp_blk),
                  _const_spec(e_t.shape),
                  pl.BlockSpec((1, t, hpg * NSA_DV), q_blk),
                  pl.BlockSpec((1, t, LANE), lambda b, g, i: (b, i, 0)),
                  _const_spec(near.shape)],
        out_specs=pl.BlockSpec((1, t, hpg * NSA_DV), q_blk),
        scratch_shapes=[pltpu.VMEM((seq, NSA_DK + LANE), BF16),
                        pltpu.VMEM((2, rows, t), F32),
                        pltpu.VMEM((rows, 1), F32),
                        pltpu.VMEM((rows, 1), F32),
                        pltpu.VMEM((rows, NSA_DV), F32)],
        compiler_params=_cparams(3),
        name="nsa_attn",
    )(rel_bias, q, selneg, ks, vs, kw, vw, e_t, ocmp, gates, near)


def _s5_kernel(x_ref, mod_ref, g1_ref, perm_ref, permt_ref, bm_ref, cm_ref, are_ref, aim_ref,
               dskip_ref, y_ref, bu_ref, state_ref):
    bb, ts, d = x_ref.shape
    rows = bb * ts
    n_slab = d // LANE
    wid = S5_SLAB * S5_STATE

    @pl.when(pl.program_id(1) == 0)
    def _():
        state_ref[...] = jnp.zeros_like(state_ref)

    mod = mod_ref[...]
    u = (_rms(x_ref[...], g1_ref[...]) * (1.0 + mod[:, 1:2, :]) + mod[:, 0:1, :]).reshape(rows, d)
    u_tb = _dot(perm_ref[...], u.astype(BF16)).astype(BF16)
    for k in range(n_slab):
        bu_ref[:, 2 * k * wid:2 * (k + 1) * wid] = _dot(u_tb[:, k * LANE:(k + 1) * LANE], bm_ref[k])

    for k in range(n_slab):
        c_re, c_im = 2 * k * wid, (2 * k + 1) * wid
        a_re = jnp.broadcast_to(are_ref[k], (bb, wid))
        a_im = jnp.broadcast_to(aim_ref[k], (bb, wid))

        def body(t, carry, c_re=c_re, c_im=c_im, a_re=a_re, a_im=a_im):
            xr, xi = carry
            r0 = pl.multiple_of(t * bb, bb)
            nr = a_re * xr - a_im * xi + bu_ref[pl.ds(r0, bb), c_re:c_re + wid]
            ni = a_re * xi + a_im * xr + bu_ref[pl.ds(r0, bb), c_im:c_im + wid]
            bu_ref[pl.ds(r0, bb), c_re:c_re + wid] = nr
            bu_ref[pl.ds(r0, bb), c_im:c_im + wid] = ni
            return nr, ni

        xr, xi = lax.fori_loop(0, ts, body, (state_ref[:, c_re:c_re + wid],
                                             state_ref[:, c_im:c_im + wid]), unroll=4)
        state_ref[:, c_re:c_re + wid] = xr
        state_ref[:, c_im:c_im + wid] = xi

    y_tb = jnp.concatenate(
        [_dot(bu_ref[:, 2 * k * wid:2 * (k + 1) * wid].astype(BF16), cm_ref[k]).astype(BF16)
         for k in range(n_slab)], axis=1)
    y = _dot(permt_ref[...], y_tb) + dskip_ref[...] * u
    y_ref[...] = _gelu_tanh(y).astype(BF16).reshape(bb, ts, d)


def _s5_mixer(x, mod, g1, lam_re, lam_im, log_dt, b_re, b_im, c_re, c_im, d_skip):
    bsz, seq, d = x.shape
    n_slab = d // LANE
    dt = jnp.exp(log_dt)[:, None]
    mag = jnp.exp(lam_re * dt)
    lb_re = mag * jnp.cos(lam_im * dt)
    lb_im = mag * jnp.sin(lam_im * dt)
    den = lam_re * lam_re + lam_im * lam_im
    f_re = ((lb_re - 1.0) * lam_re + lb_im * lam_im) / den
    f_im = (lb_im * lam_re - (lb_re - 1.0) * lam_im) / den
    bb_re = f_re[..., None] * b_re - f_im[..., None] * b_im
    bb_im = f_re[..., None] * b_im + f_im[..., None] * b_re
    eye = jnp.eye(S5_SLAB, dtype=F32)
    wid = S5_SLAB * S5_STATE

    def in_mat(b):
        b = b.reshape(n_slab, S5_SLAB, S5_STATE, S5_GROUP)
        return jnp.einsum('kgpi,gh->kgihp', b, eye).reshape(n_slab, LANE, wid)

    def out_mat(c):
        c = c.reshape(n_slab, S5_SLAB, S5_GROUP, S5_STATE)
        return jnp.einsum('kgip,gh->kgphi', c, eye).reshape(n_slab, wid, LANE)

    bm = jnp.concatenate([in_mat(bb_re), in_mat(bb_im)], axis=2).astype(BF16)
    cm = jnp.concatenate([out_mat(c_re), -out_mat(c_im)], axis=1).astype(BF16)
    a_re = lb_re.reshape(n_slab, 1, wid)
    a_im = lb_im.reshape(n_slab, 1, wid)

    bb, ts = S5_BB, 64
    rows = bb * ts
    r = jnp.arange(rows)
    perm = ((r % bb)[:, None] * ts + (r // bb)[:, None] == r[None, :]).astype(BF16)
    blk = lambda i, t: (i, t, 0)
    return pl.pallas_call(
        _s5_kernel,
        out_shape=jax.ShapeDtypeStruct((bsz, seq, d), BF16),
        grid=(bsz // bb, seq // ts),
        in_specs=[pl.BlockSpec((bb, ts, d), blk),
                  pl.BlockSpec((bb, N_MOD, d), lambda i, t: (i, 0, 0)),
                  _const_spec((1, d)),
                  _const_spec((rows, rows)), _const_spec((rows, rows)),
                  _const_spec(bm.shape), _const_spec(cm.shape),
                  _const_spec(a_re.shape), _const_spec(a_im.shape),
                  _const_spec((1, d))],
        out_specs=pl.BlockSpec((bb, ts, d), blk),
        scratch_shapes=[pltpu.VMEM((rows, 2 * wid * n_slab), F32),
                        pltpu.VMEM((bb, 2 * wid * n_slab), F32)],
        compiler_params=_cparams(2),
        name="s5_scan",
    )(x, mod, g1.reshape(1, d), perm, perm.T, bm, cm, a_re, a_im, d_skip.reshape(1, d))


def kernel(x, c, ada_w, ada_b, norm1_g, norm2_g, final_g, rel_bias, mla_w_dkv, mla_g_q, mla_g_kv, mla_w_uq, mla_w_ukv, mla_w_o, nsa_w_in, nsa_pe_k, nsa_w1_k, nsa_w2_k, nsa_pe_v, nsa_w1_v, nsa_w2_v, nsa_w_o, s5_lam_re, s5_lam_im, s5_log_dt, s5_b_re, s5_b_im, s5_c_re, s5_c_im, s5_d, s5_w_glu, ffn_w_up, ffn_conv_w, ffn_conv_b, ffn_w_down):
    depth = ada_w.shape[0]
    mods = _ada_mod(c, ada_w, ada_b)
    for i in range(depth):
        mod = mods[i]
        kind, j = i % N_MIXERS, i // N_MIXERS
        if kind == 0:
            o = _mla_mixer(x, mod, norm1_g[i], mla_w_dkv[j], mla_g_q[j], mla_g_kv[j],
                           mla_w_uq[j], mla_w_ukv[j])
            w_post, glu = mla_w_o[j], False
        elif kind == 1:
            o = _nsa_mixer(x, mod, norm1_g[i], nsa_w_in[j], nsa_pe_k[j], nsa_w1_k[j], nsa_w2_k[j],
                           nsa_pe_v[j], nsa_w1_v[j], nsa_w2_v[j], rel_bias)
            w_post, glu = nsa_w_o[j], False
        else:
            o = _s5_mixer(x, mod, norm1_g[i], s5_lam_re[j], s5_lam_im[j], s5_log_dt[j],
                          s5_b_re[j], s5_b_im[j], s5_c_re[j], s5_c_im[j], s5_d[j])
            w_post, glu = s5_w_glu[j], True
        x = _post_ffn(x, o, mod, w_post.astype(BF16), norm2_g[i], ffn_w_up[i].astype(BF16),
                      ffn_conv_w[i], ffn_conv_b[i], ffn_w_down[i].astype(BF16), final_g,
                      glu=glu, final=(i == depth - 1))
    return x
```

```python
import functools
import math

import jax
import jax.numpy as jnp
from jax import lax
from jax.experimental import pallas as pl
from jax.experimental.pallas import tpu as pltpu

F32 = jnp.float32
BF16 = jnp.bfloat16

NORM_EPS = 1e-6
N_MOD = 6
N_MIXERS = 3

MLA_HEADS = 8
MLA_Q_LORA = 512
MLA_KV_LORA = 256
MLA_NOPE = 128
MLA_ROPE = 64
MLA_V = 128
MLA_QK = MLA_NOPE + MLA_ROPE
ROPE_THETA = 10000.0

NSA_HEADS = 8
NSA_GROUPS = 2
NSA_HPG = NSA_HEADS // NSA_GROUPS
NSA_DK = 128
NSA_DV = 128
CMP_BLOCK = 32
CMP_STRIDE = 16
CMP_RATIO = CMP_BLOCK // CMP_STRIDE
CMP_HIDDEN = 256
SLC_BLOCK = 64
SLC_TOPN = 16
N_LOCAL = 2
FORCE_BONUS = 1e4
WINDOW = 512
REL_BUCKETS = 32
REL_MAX_DIST = 128

S5_GROUP = 16
S5_STATE = 64
CONV_WIDTH = 3

LANE = 128
NEG = -1e30
VMEM_LIMIT = 56 * 1024 * 1024

ATT_TILE = 128
MLA_TQ = 256
S5_BB = 8
S5_SLAB = 8


def _cparams(n_axes):
    return pltpu.CompilerParams(
        dimension_semantics=("arbitrary",) * n_axes,
        vmem_limit_bytes=VMEM_LIMIT)


def _const_spec(shape):
    zeros = (0,) * len(shape)
    return pl.BlockSpec(shape, lambda *_: zeros, pipeline_mode=pl.Buffered(1))


def _dot(a, b):
    return jnp.dot(a, b, preferred_element_type=F32)


def _dot_nt(a, b):
    return lax.dot_general(a, b, (((1,), (1,)), ((), ())), preferred_element_type=F32)


def _rms(x, g):
    return x * lax.rsqrt(jnp.mean(x * x, axis=-1, keepdims=True) + NORM_EPS) * g


def _norm_mod(x, g, shift, scale):
    return _rms(x, g) * (1.0 + scale) + shift


def _gelu_tanh(x):
    c = math.sqrt(2.0 / math.pi)
    return 0.5 * x * (1.0 + jnp.tanh(c * (x + 0.044715 * (x * x * x))))


def _sigmoid(x):
    return 1.0 / (1.0 + jnp.exp(-x))


def _t5_bucket(dist):
    n = jnp.maximum(dist, 0)
    max_exact = REL_BUCKETS // 2
    nf = jnp.maximum(n, 1).astype(F32)
    large = max_exact + (jnp.log(nf / max_exact) / math.log(REL_MAX_DIST / max_exact)
                         * (REL_BUCKETS - max_exact)).astype(jnp.int32)
    large = jnp.minimum(large, REL_BUCKETS - 1)
    return jnp.where(n < max_exact, n, large)


def _ada_kernel(c_ref, w_ref, b_ref, o_ref):
    c = c_ref[...]
    c_act = c * _sigmoid(c)
    o_ref[0] = jnp.dot(c_act, w_ref[0], preferred_element_type=F32,
                       precision=lax.Precision.HIGHEST) + b_ref[0]


def _ada_mod(c, ada_w, ada_b):
    depth, d, n = ada_w.shape
    bsz = c.shape[0]
    tn = 1024
    out = pl.pallas_call(
        _ada_kernel,
        out_shape=jax.ShapeDtypeStruct((depth, bsz, n), F32),
        grid=(depth, n // tn),
        in_specs=[pl.BlockSpec((bsz, d), lambda i, j: (0, 0)),
                  pl.BlockSpec((1, d, tn), lambda i, j: (i, 0, j)),
                  pl.BlockSpec((1, 1, tn), lambda i, j: (i, 0, j))],
        out_specs=pl.BlockSpec((1, bsz, tn), lambda i, j: (i, 0, j)),
        compiler_params=_cparams(2),
        name="ada_mod",
    )(c, ada_w, ada_b.reshape(depth, 1, n))
    return out.reshape(depth, bsz, N_MOD, d)


def _post_ffn_kernel(x_ref, o_ref, mod_ref, wpost_ref, g2_ref, wup_ref, cw_ref, cb_ref,
                     wdown_ref, gf_ref, out_ref, act_ref, carry_ref,
                     *, glu, final, fc):
    tm = x_ref.shape[1]
    d_ff = wdown_ref.shape[0]
    x = x_ref[0]
    mod = mod_ref[0]
    y = _dot(o_ref[0], wpost_ref[...])
    if glu:
        d = x.shape[-1]
        y = y[:, :d] * _sigmoid(y[:, d:])
    x1 = x + mod[2:3] * y
    h = _norm_mod(x1, g2_ref[...], mod[3:4], mod[4:5]).astype(BF16)

    @pl.when(pl.program_id(1) == 0)
    def _():
        carry_ref[...] = jnp.zeros_like(carry_ref)

    def conv(z, col):
        zfull = jnp.concatenate([carry_ref[:, col:col + fc], z], axis=0)
        carry_ref[:, col:col + fc] = z[tm - 8:, :]
        w = cw_ref[:, col:col + fc]
        return (w[0:1] * zfull[6:6 + tm] + w[1:2] * zfull[7:7 + tm] + w[2:3] * z
                + cb_ref[:, col:col + fc])

    for f in range(0, d_ff, fc):
        val = conv(_dot(h, wup_ref[:, f:f + fc]), f)
        gate = conv(_dot(h, wup_ref[:, d_ff + f:d_ff + f + fc]), d_ff + f)
        act_ref[:, f:f + fc] = (val * (gate * _sigmoid(gate))).astype(BF16)

    x2 = x1 + mod[5:6] * _dot(act_ref[...], wdown_ref[...])
    if final:
        x2 = _rms(x2, gf_ref[...])
    out_ref[0] = x2


def _post_ffn(x, o, mod, w_post, g2, w_up, conv_w, conv_b, w_down, final_g, *, glu, final):
    bsz, seq, d = x.shape
    d_ff = w_down.shape[0]
    tm = 512
    fc = 256
    row = lambda b, i: (b, i, 0)
    return pl.pallas_call(
        functools.partial(_post_ffn_kernel, glu=glu, final=final, fc=fc),
        out_shape=jax.ShapeDtypeStruct((bsz, seq, d), F32),
        grid=(bsz, seq // tm),
        in_specs=[pl.BlockSpec((1, tm, d), row),
                  pl.BlockSpec((1, tm, d), row),
                  pl.BlockSpec((1, N_MOD, d), lambda b, i: (b, 0, 0)),
                  _const_spec(w_post.shape),
                  _const_spec((1, d)),
                  _const_spec(w_up.shape),
                  _const_spec(conv_w.shape),
                  _const_spec((1, 2 * d_ff)),
                  _const_spec(w_down.shape),
                  _const_spec((1, d))],
        out_specs=pl.BlockSpec((1, tm, d), row),
        scratch_shapes=[pltpu.VMEM((tm, d_ff), BF16),
                        pltpu.VMEM((8, 2 * d_ff), F32)],
        compiler_params=_cparams(2),
        name="post_ffn",
    )(x, o, mod, w_post, g2.reshape(1, d), w_up, conv_w, conv_b.reshape(1, -1), w_down,
      final_g.reshape(1, d))


def _rope_tables(seq, reps):
    half = MLA_ROPE // 2
    inv = ROPE_THETA ** (-jnp.arange(half, dtype=F32) / half)
    ang = jnp.arange(seq).astype(F32)[:, None] * inv[None, :]
    cos, sin = jnp.cos(ang), jnp.sin(ang)
    cos64 = jnp.concatenate([cos, cos], axis=1)
    sin64 = jnp.concatenate([-sin, sin], axis=1)
    return jnp.tile(cos64, (1, reps)), jnp.tile(sin64, (1, reps))


def _mla_proj_kernel(x_ref, mod_ref, g1_ref, wdkv_ref, gq_ref, gkv_ref, wuq_ref, wukv_ref,
                     cos_ref, sin_ref, qn_ref, qr_ref, kn_ref, kr_ref, v_ref):
    nq = MLA_HEADS * MLA_NOPE
    nr = MLA_HEADS * MLA_ROPE
    scale = MLA_QK ** -0.5
    mod = mod_ref[0]
    h = _norm_mod(x_ref[0], g1_ref[...], mod[0:1], mod[1:2]).astype(BF16)
    dkv = _dot(h, wdkv_ref[...])
    c_q = _rms(dkv[:, :MLA_Q_LORA], gq_ref[...]).astype(BF16)
    c_kv = _rms(dkv[:, MLA_Q_LORA:MLA_Q_LORA + MLA_KV_LORA], gkv_ref[...]).astype(BF16)
    cos = cos_ref[...]
    sin = sin_ref[...]
    r0 = MLA_Q_LORA + MLA_KV_LORA
    kr_ref[0] = (dkv[:, r0:r0 + LANE] * cos[:, :LANE]
                 + dkv[:, r0 + LANE:r0 + 2 * LANE] * sin[:, :LANE]).astype(BF16)
    q = _dot(c_q, wuq_ref[...])
    qn_ref[0] = (q[:, :nq] * scale).astype(BF16)
    qr_ref[0] = ((q[:, nq:nq + nr] * cos + q[:, nq + nr:] * sin) * scale).astype(BF16)
    kv = _dot(c_kv, wukv_ref[...])
    kn_ref[0] = kv[:, :nq].astype(BF16)
    v_ref[0] = kv[:, nq:].astype(BF16)


def _mla_attn_kernel(qn_ref, qr_ref, kn_ref, kr_ref, v_ref, o_ref):
    seq = qn_ref.shape[1]
    tq = MLA_TQ
    odd = pl.program_id(1) % 2
    lane = lax.broadcasted_iota(jnp.int32, (tq, LANE), 1)
    keep = (lane >= MLA_ROPE) == (odd == 1)
    row = lax.broadcasted_iota(jnp.int32, (tq, tq), 0)
    col = lax.broadcasted_iota(jnp.int32, (tq, tq), 1)
    for i in range(seq // tq):
        lo, hi = i * tq, (i + 1) * tq
        qr = jnp.where(keep, qr_ref[0, lo:hi, :], jnp.zeros((), BF16))
        q = jnp.concatenate([qn_ref[0, lo:hi, :], qr], axis=1)
        k_d = jnp.concatenate([kn_ref[0, lo:hi, :], kr_ref[0, lo:hi, :]], axis=1)
        s_d = jnp.where(col <= row, _dot_nt(q, k_d), NEG)
        m = jnp.max(s_d, axis=1, keepdims=True)
        if i > 0:
            k_o = jnp.concatenate([kn_ref[0, :lo, :], kr_ref[0, :lo, :]], axis=1)
            s_o = _dot_nt(q, k_o)
            m = jnp.maximum(m, jnp.max(s_o, axis=1, keepdims=True))
            p_o = jnp.exp(s_o - m)
        p_d = jnp.exp(s_d - m)
        l = jnp.sum(p_d, axis=1, keepdims=True)
        o = _dot(p_d.astype(BF16), v_ref[0, lo:hi, :])
        if i > 0:
            l = l + jnp.sum(p_o, axis=1, keepdims=True)
            o = o + _dot(p_o.astype(BF16), v_ref[0, :lo, :])
        o_ref[0, lo:hi, :] = (o / l).astype(BF16)


def _mla_mixer(x, mod, g1, w_dkv, g_q, g_kv, w_uq, w_ukv):
    bsz, seq, d = x.shape
    hds = MLA_HEADS
    half = MLA_ROPE // 2
    swap = jnp.concatenate([jnp.arange(half, MLA_ROPE), jnp.arange(half)])
    r0 = MLA_Q_LORA + MLA_KV_LORA
    kr = w_dkv[:, r0:]
    wdkv = jnp.concatenate([w_dkv[:, :r0], kr, kr, kr[:, swap], kr[:, swap]], axis=1).astype(BF16)
    wq = w_uq.reshape(MLA_Q_LORA, hds, MLA_QK)
    wq_r = wq[:, :, MLA_NOPE:]
    wuq = jnp.concatenate([wq[:, :, :MLA_NOPE].reshape(MLA_Q_LORA, -1),
                           wq_r.reshape(MLA_Q_LORA, -1),
                           wq_r[:, :, swap].reshape(MLA_Q_LORA, -1)], axis=1).astype(BF16)
    wkv = w_ukv.reshape(MLA_KV_LORA, hds, MLA_NOPE + MLA_V)
    wukv = jnp.concatenate([wkv[:, :, :MLA_NOPE].reshape(MLA_KV_LORA, -1),
                            wkv[:, :, MLA_NOPE:].reshape(MLA_KV_LORA, -1)], axis=1).astype(BF16)
    cos, sin = _rope_tables(seq, hds)
    nq, nr = hds * MLA_NOPE, hds * MLA_ROPE
    tm = 512
    row = lambda i, b: (b, i, 0)
    qn, qr, kn, kr2, v = pl.pallas_call(
        _mla_proj_kernel,
        out_shape=[jax.ShapeDtypeStruct((bsz, seq, nq), BF16),
                   jax.ShapeDtypeStruct((bsz, seq, nr), BF16),
                   jax.ShapeDtypeStruct((bsz, seq, nq), BF16),
                   jax.ShapeDtypeStruct((bsz, seq, LANE), BF16),
                   jax.ShapeDtypeStruct((bsz, seq, nq), BF16)],
        grid=(seq // tm, bsz),
        in_specs=[pl.BlockSpec((1, tm, d), row),
                  pl.BlockSpec((1, N_MOD, d), lambda i, b: (b, 0, 0)),
                  _const_spec((1, d)),
                  _const_spec(wdkv.shape),
                  _const_spec((1, MLA_Q_LORA)),
                  _const_spec((1, MLA_KV_LORA)),
                  _const_spec(wuq.shape),
                  _const_spec(wukv.shape),
                  pl.BlockSpec((tm, nr), lambda i, b: (i, 0)),
                  pl.BlockSpec((tm, nr), lambda i, b: (i, 0))],
        out_specs=[pl.BlockSpec((1, tm, nq), row),
                   pl.BlockSpec((1, tm, nr), row),
                   pl.BlockSpec((1, tm, nq), row),
                   pl.BlockSpec((1, tm, LANE), row),
                   pl.BlockSpec((1, tm, nq), row)],
        compiler_params=_cparams(2),
        name="mla_proj",
    )(x, mod, g1.reshape(1, d), wdkv, g_q.reshape(1, -1), g_kv.reshape(1, -1), wuq, wukv, cos, sin)

    head = lambda b, h: (b, 0, h)
    return pl.pallas_call(
        _mla_attn_kernel,
        out_shape=jax.ShapeDtypeStruct((bsz, seq, hds * MLA_V), BF16),
        grid=(bsz, hds),
        in_specs=[pl.BlockSpec((1, seq, LANE), head),
                  pl.BlockSpec((1, seq, LANE), lambda b, h: (b, 0, h // 2)),
                  pl.BlockSpec((1, seq, LANE), head),
                  pl.BlockSpec((1, seq, LANE), lambda b, h: (b, 0, 0)),
                  pl.BlockSpec((1, seq, LANE), head)],
        out_specs=pl.BlockSpec((1, seq, LANE), head),
        compiler_params=_cparams(2),
        name="mla_attn",
    )(qn, qr, kn, kr2, v)


def _nsa_proj_kernel(x_ref, mod_ref, g1_ref, w_ref, q_ref, kc_ref, vc_ref, ks_ref, vs_ref,
                     kw_ref, vw_ref, gate_ref):
    nq = NSA_HEADS * NSA_DK
    nkv = NSA_GROUPS * NSA_DK
    mod = mod_ref[0]
    h = _norm_mod(x_ref[0], g1_ref[...], mod[0:1], mod[1:2]).astype(BF16)
    p = _dot(h, w_ref[...])
    q_ref[0] = (p[:, :nq] * (NSA_DK ** -0.5)).astype(BF16)
    for n, ref in enumerate((kc_ref, vc_ref, ks_ref, vs_ref, kw_ref, vw_ref)):
        ref[0] = p[:, nq + n * nkv:nq + (n + 1) * nkv].astype(BF16)
    gate_ref[0] = _sigmoid(p[:, nq + 6 * nkv:])


def _nsa_compress_kernel(fk_ref, fv_ref, pek_ref, pev_ref, w1k_ref, w1v_ref, w2k_ref, w2v_ref,
                         ok_ref, ov_ref):
    for f_ref, pe_ref, w1_ref, w2_ref, o_ref in ((fk_ref, pek_ref, w1k_ref, w2k_ref, ok_ref),
                                                 (fv_ref, pev_ref, w1v_ref, w2v_ref, ov_ref)):
        flat = (f_ref[...].astype(F32) + pe_ref[...]).astype(BF16)
        hid = _gelu_tanh(_dot(flat, w1_ref[...])).astype(BF16)
        o_ref[...] = _dot(hid, w2_ref[...]).astype(BF16)


def _nsa_select_kernel(rb_ref, q_ref, kc_ref, vc_ref, bucket_ref, ovl_ref, bonus_ref,
                       ocmp_ref, sel_ref, bias_ref):
    g = pl.program_id(0)
    seq = q_ref.shape[1]
    n_blk = bonus_ref.shape[0]
    bucket = bucket_ref[...]

    @pl.when(pl.program_id(1) == 0)
    def _():
        for hp in range(NSA_HPG):
            acc = jnp.zeros(bucket.shape, F32)
            for r in range(REL_BUCKETS):
                acc = jnp.where(bucket == r, rb_ref[g * NSA_HPG + hp, r], acc)
            bias_ref[hp] = acc

    valid = bucket >= 0
    kc = kc_ref[0, 0]
    vc = vc_ref[0, 0]
    psum = jnp.zeros(bucket.shape, F32)
    for hp in range(NSA_HPG):
        s = _dot_nt(q_ref[0, :, hp * NSA_DK:(hp + 1) * NSA_DK], kc) + bias_ref[hp]
        s = jnp.where(valid, s, NEG)
        m = jnp.max(s, axis=1, keepdims=True)
        m = jnp.where(m > 0.5 * NEG, m, 0.0)
        p = jnp.exp(s - m)
        p = p / jnp.maximum(jnp.sum(p, axis=1, keepdims=True), 1e-30)
        ocmp_ref[0, :, hp * NSA_DV:(hp + 1) * NSA_DV] = _dot(p.astype(BF16), vc).astype(BF16)
        psum = psum + p

    imp = lax.dot_general(ovl_ref[...], psum, (((1,), (1,)), ((), ())),
                          preferred_element_type=F32, precision=lax.Precision.HIGHEST)
    bonus = bonus_ref[...]
    score = jnp.where(bonus > 0.5 * NEG, imp + bonus, NEG)
    blk = lax.broadcasted_iota(jnp.int32, (n_blk, seq), 0)
    rank = jnp.zeros((n_blk, seq), F32)
    for i in range(n_blk):
        row = score[i:i + 1, :]
        rank = rank + jnp.where(blk > i, jnp.where(row >= score, 1.0, 0.0),
                                jnp.where(row > score, 1.0, 0.0))
    selneg = jnp.where(rank < float(SLC_TOPN), jnp.where(score > 0.5 * NEG, 0.0, NEG), NEG)
    pad = jnp.zeros((LANE - n_blk, seq), F32)
    sel_ref[0, 0] = jnp.concatenate([selneg, pad], axis=0).T.astype(BF16)


def _softmax_pv(scores, values):
    m = functools.reduce(jnp.maximum, [jnp.max(s, axis=1, keepdims=True) for s in scores])
    ps = [jnp.exp(s - m) for s in scores]
    l = functools.reduce(jnp.add, [jnp.sum(p, axis=1, keepdims=True) for p in ps])
    o = functools.reduce(jnp.add, [_dot(p.astype(BF16), v) for p, v in zip(ps, values)])
    return o / l


def _nsa_attn_kernel(rb_ref, q_ref, sel_ref, ks_ref, vs_ref, kw_ref, vw_ref, et_ref, ocmp_ref,
                     gate_ref, bkt_ref, o_ref, kaug_ref, bias_ref):
    g = pl.program_id(0)
    t = ATT_TILE
    hp_n = NSA_HPG
    seq = q_ref.shape[1]

    @pl.when(pl.program_id(1) == 0)
    def _():
        kaug_ref[:, NSA_DK:] = et_ref[...]
        for kind in range(2):
            bkt = bkt_ref[kind]
            for hp in range(hp_n):
                hd = g * hp_n + hp
                acc = jnp.zeros((t, t), F32)
                for r in range(REL_BUCKETS - 1):
                    acc = jnp.where(bkt == r, rb_ref[hd, r] - rb_ref[hd, REL_BUCKETS - 1], acc)
                bias_ref[hp * t:(hp + 1) * t, (1 - kind) * t:(2 - kind) * t] = acc
        bias_ref[:, 2 * t:] = jnp.zeros((hp_n * t, t), F32)

    kaug_ref[:, :NSA_DK] = ks_ref[0]

    rows = hp_n * t
    r2 = lax.broadcasted_iota(jnp.int32, (rows, 2 * t), 0) & (t - 1)
    c2 = lax.broadcasted_iota(jnp.int32, (rows, 2 * t), 1)
    r1 = lax.broadcasted_iota(jnp.int32, (rows, t), 0) & (t - 1)
    c1 = lax.broadcasted_iota(jnp.int32, (rows, t), 1)
    near_ok = c2 <= r2 + t
    first_ok = c2 <= r2
    edge_ok = c1 > r1
    lane = lax.broadcasted_iota(jnp.int32, (t, LANE), 1)

    for i in range(seq // t):
        lo, hi = i * t, (i + 1) * t
        q_heads = [q_ref[0, lo:hi, hp * NSA_DK:(hp + 1) * NSA_DK] for hp in range(hp_n)]
        q_win = jnp.concatenate(q_heads, axis=0)
        sel = sel_ref[0, 0, lo:hi, :]
        q_sel = jnp.concatenate([jnp.concatenate([qh, sel], axis=1) for qh in q_heads], axis=0)
        if i == 0:
            n_lo, n_hi, bias, ok = 0, 2 * t, bias_ref[:, t:], first_ok
        else:
            n_lo, n_hi, bias, ok = lo - t, hi, bias_ref[:, :2 * t], near_ok

        scores = [jnp.where(ok, _dot_nt(q_sel, kaug_ref[n_lo:n_hi, :]) + bias, NEG)]
        values = [vs_ref[0, n_lo:n_hi, :]]
        if n_lo > 0:
            scores.append(_dot_nt(q_sel, kaug_ref[:n_lo, :]))
            values.append(vs_ref[0, :n_lo, :])
        o_slc = _softmax_pv(scores, values)

        scores = [jnp.where(ok, _dot_nt(q_win, kw_ref[0, n_lo:n_hi, :]) + bias, NEG)]
        values = [vw_ref[0, n_lo:n_hi, :]]
        w_lo = max(lo - WINDOW, 0)
        if n_lo > w_lo:
            s_m = _dot_nt(q_win, kw_ref[0, w_lo:n_lo, :])
            if lo - WINDOW >= 0:
                s_m = jnp.concatenate([jnp.where(edge_ok, s_m[:, :t], NEG), s_m[:, t:]], axis=1)
            scores.append(s_m)
            values.append(vw_ref[0, w_lo:n_lo, :])
        o_win = _softmax_pv(scores, values)

        gate = gate_ref[0, lo:hi, :]
        for hp in range(hp_n):
            hd = g * hp_n + hp

            def gcol(branch):
                return jnp.sum(jnp.where(lane == branch * NSA_HEADS + hd, gate, 0.0),
                               axis=1, keepdims=True)
            sl = slice(hp * t, (hp + 1) * t)
            o = (gcol(0) * ocmp_ref[0, lo:hi, hp * NSA_DV:(hp + 1) * NSA_DV].astype(F32)
                 + gcol(1) * o_slc[sl] + gcol(2) * o_win[sl])
            o_ref[0, lo:hi, hp * NSA_DV:(hp + 1) * NSA_DV] = o.astype(BF16)


def _nsa_mixer(x, mod, g1, w_in, pe_k, w1_k, w2_k, pe_v, w1_v, w2_v, rel_bias):
    bsz, seq, d = x.shape
    grp, hpg = NSA_GROUPS, NSA_HPG
    nq, nkv = NSA_HEADS * NSA_DK, NSA_GROUPS * NSA_DK
    n_gate = 3 * NSA_HEADS
    w = jnp.pad(w_in, ((0, 0), (0, LANE - n_gate))).astype(BF16)
    tm = 512
    row = lambda i, b: (b, i, 0)
    kv_shape = jax.ShapeDtypeStruct((bsz, seq, nkv), BF16)
    q, kc, vc, ks, vs, kw, vw, gates = pl.pallas_call(
        _nsa_proj_kernel,
        out_shape=[jax.ShapeDtypeStruct((bsz, seq, nq), BF16)] + [kv_shape] * 6
                  + [jax.ShapeDtypeStruct((bsz, seq, LANE), F32)],
        grid=(seq // tm, bsz),
        in_specs=[pl.BlockSpec((1, tm, d), row),
                  pl.BlockSpec((1, N_MOD, d), lambda i, b: (b, 0, 0)),
                  _const_spec((1, d)),
                  _const_spec(w.shape)],
        out_specs=[pl.BlockSpec((1, tm, nq), row)] + [pl.BlockSpec((1, tm, nkv), row)] * 6
                  + [pl.BlockSpec((1, tm, LANE), row)],
        compiler_params=_cparams(2),
        name="nsa_proj",
    )(x, mod, g1.reshape(1, d), w)

    n_chunk = seq // CMP_STRIDE
    n_cmp = n_chunk - CMP_RATIO + 1

    def blocks(t):
        ch = t.reshape(bsz, n_chunk, CMP_STRIDE, grp, NSA_DK).transpose(0, 3, 1, 2, 4)
        ch = ch.reshape(bsz, grp, n_chunk, CMP_STRIDE * NSA_DK)
        ch = jnp.pad(ch, ((0, 0), (0, 0), (0, CMP_RATIO - 1), (0, 0)))
        flat = jnp.concatenate([ch[:, :, r:r + n_chunk] for r in range(CMP_RATIO)], axis=3)
        return flat.reshape(bsz * grp * n_chunk, CMP_BLOCK * NSA_DK)

    fk, fv = blocks(kc), blocks(vc)
    tr = min(1024, bsz * grp * n_chunk)
    wide = CMP_BLOCK * NSA_DK
    kcmp, vcmp = pl.pallas_call(
        _nsa_compress_kernel,
        out_shape=[jax.ShapeDtypeStruct((bsz * grp * n_chunk, NSA_DK), BF16)] * 2,
        grid=(bsz * grp * n_chunk // tr,),
        in_specs=[pl.BlockSpec((tr, wide), lambda i: (i, 0)),
                  pl.BlockSpec((tr, wide), lambda i: (i, 0)),
                  _const_spec((1, wide)), _const_spec((1, wide)),
                  _const_spec((wide, CMP_HIDDEN)), _const_spec((wide, CMP_HIDDEN)),
                  _const_spec((CMP_HIDDEN, NSA_DK)), _const_spec((CMP_HIDDEN, NSA_DV))],
        out_specs=[pl.BlockSpec((tr, NSA_DK), lambda i: (i, 0))] * 2,
        compiler_params=_cparams(1),
        name="nsa_compress",
    )(fk, fv, pe_k.reshape(1, wide), pe_v.reshape(1, wide), w1_k.astype(BF16), w1_v.astype(BF16),
      w2_k.astype(BF16), w2_v.astype(BF16))
    kcmp = kcmp.reshape(bsz, grp, n_chunk, NSA_DK)
    vcmp = vcmp.reshape(bsz, grp, n_chunk, NSA_DV)

    pos = jnp.arange(seq)
    cmp_start = jnp.arange(n_chunk) * CMP_STRIDE
    dist_c = pos[:, None] - (cmp_start + CMP_BLOCK - 1)[None, :]
    ok_c = (dist_c >= 0) & (jnp.arange(n_chunk) < n_cmp)[None, :]
    bucket_c = jnp.where(ok_c, _t5_bucket(dist_c), -1).astype(jnp.int32)
    n_blk = seq // SLC_BLOCK
    blk = jnp.arange(n_blk)
    blk_start = blk * SLC_BLOCK
    overlap_t = ((cmp_start[None, :] < blk_start[:, None] + SLC_BLOCK)
                 & (cmp_start[None, :] + CMP_BLOCK > blk_start[:, None])
                 & (jnp.arange(n_chunk) < n_cmp)[None, :]).astype(F32)
    back = (pos // SLC_BLOCK)[None, :] - blk[:, None]
    forced = (blk[:, None] == 0) | ((back >= 0) & (back < N_LOCAL))
    causal_blk = blk_start[:, None] <= pos[None, :]
    bonus_t = jnp.where(causal_blk, jnp.where(forced, FORCE_BONUS, 0.0), NEG).astype(F32)

    ocmp, selneg = pl.pallas_call(
        _nsa_select_kernel,
        out_shape=[jax.ShapeDtypeStruct((bsz, seq, nq), BF16),
                   jax.ShapeDtypeStruct((bsz, grp, seq, LANE), BF16)],
        grid=(grp, bsz),
        in_specs=[pl.BlockSpec(memory_space=pltpu.SMEM),
                  pl.BlockSpec((1, seq, hpg * NSA_DK), lambda g, b: (b, 0, g)),
                  pl.BlockSpec((1, 1, n_chunk, NSA_DK), lambda g, b: (b, g, 0, 0)),
                  pl.BlockSpec((1, 1, n_chunk, NSA_DV), lambda g, b: (b, g, 0, 0)),
                  _const_spec(bucket_c.shape),
                  _const_spec(overlap_t.shape),
                  _const_spec(bonus_t.shape)],
        out_specs=[pl.BlockSpec((1, seq, hpg * NSA_DV), lambda g, b: (b, 0, g)),
                   pl.BlockSpec((1, 1, seq, LANE), lambda g, b: (b, g, 0, 0))],
        scratch_shapes=[pltpu.VMEM((hpg, seq, n_chunk), F32)],
        compiler_params=_cparams(2),
        name="nsa_select",
    )(rel_bias, q, kcmp, vcmp, bucket_c, overlap_t, bonus_t)

    t = ATT_TILE
    rr = jnp.arange(t)
    near = jnp.stack([_t5_bucket(k * t + rr[:, None] - rr[None, :]) for k in range(2)]).astype(jnp.int32)
    e_t = (jnp.arange(seq)[:, None] // SLC_BLOCK == jnp.arange(LANE)[None, :]).astype(BF16)
    grp_blk = lambda g, b: (b, 0, g)
    return pl.pallas_call(
        _nsa_attn_kernel,
        out_shape=jax.ShapeDtypeStruct((bsz, seq, nq), BF16),
        grid=(grp, bsz),
        in_specs=[pl.BlockSpec(memory_space=pltpu.SMEM),
                  pl.BlockSpec((1, seq, hpg * NSA_DK), grp_blk),
                  pl.BlockSpec((1, 1, seq, LANE), lambda g, b: (b, g, 0, 0)),
                  pl.BlockSpec((1, seq, NSA_DK), grp_blk),
                  pl.BlockSpec((1, seq, NSA_DV), grp_blk),
                  pl.BlockSpec((1, seq, NSA_DK), grp_blk),
                  pl.BlockSpec((1, seq, NSA_DV), grp_blk),
                  _const_spec(e_t.shape),
                  pl.BlockSpec((1, seq, hpg * NSA_DV), grp_blk),
                  pl.BlockSpec((1, seq, LANE), lambda g, b: (b, 0, 0)),
                  _const_spec(near.shape)],
        out_specs=pl.BlockSpec((1, seq, hpg * NSA_DV), grp_blk),
        scratch_shapes=[pltpu.VMEM((seq, NSA_DK + LANE), BF16),
                        pltpu.VMEM((hpg * t, 3 * t), F32)],
        compiler_params=_cparams(2),
        name="nsa_attn",
    )(rel_bias, q, selneg, ks, vs, kw, vw, e_t, ocmp, gates, near)


def _s5_kernel(x_ref, mod_ref, g1_ref, perm_ref, permt_ref, bm_ref, cm_ref, are_ref, aim_ref,
               dskip_ref, y_ref, bu_ref, state_ref):
    bb, ts, d = x_ref.shape
    rows = bb * ts
    n_slab = d // LANE
    wid = S5_SLAB * S5_STATE

    @pl.when(pl.program_id(1) == 0)
    def _():
        state_ref[...] = jnp.zeros_like(state_ref)

    mod = mod_ref[...]
    u = (_rms(x_ref[...], g1_ref[...]) * (1.0 + mod[:, 1:2, :]) + mod[:, 0:1, :]).reshape(rows, d)
    u_tb = _dot(perm_ref[...], u.astype(BF16)).astype(BF16)
    for k in range(n_slab):
        bu_ref[:, 2 * k * wid:2 * (k + 1) * wid] = _dot(u_tb[:, k * LANE:(k + 1) * LANE], bm_ref[k])

    for k in range(n_slab):
        c_re, c_im = 2 * k * wid, (2 * k + 1) * wid
        a_re = jnp.broadcast_to(are_ref[k], (bb, wid))
        a_im = jnp.broadcast_to(aim_ref[k], (bb, wid))

        def body(t, carry, c_re=c_re, c_im=c_im, a_re=a_re, a_im=a_im):
            xr, xi = carry
            r0 = pl.multiple_of(t * bb, bb)
            nr = a_re * xr - a_im * xi + bu_ref[pl.ds(r0, bb), c_re:c_re + wid]
            ni = a_re * xi + a_im * xr + bu_ref[pl.ds(r0, bb), c_im:c_im + wid]
            bu_ref[pl.ds(r0, bb), c_re:c_re + wid] = nr
            bu_ref[pl.ds(r0, bb), c_im:c_im + wid] = ni
            return nr, ni

        xr, xi = lax.fori_loop(0, ts, body, (state_ref[:, c_re:c_re + wid],
                                             state_ref[:, c_im:c_im + wid]), unroll=4)
        state_ref[:, c_re:c_re + wid] = xr
        state_ref[:, c_im:c_im + wid] = xi

    y_tb = jnp.concatenate(
        [_dot(bu_ref[:, 2 * k * wid:2 * (k + 1) * wid].astype(BF16), cm_ref[k]).astype(BF16)
         for k in range(n_slab)], axis=1)
    y = _dot(permt_ref[...], y_tb) + dskip_ref[...] * u
    y_ref[...] = _gelu_tanh(y).astype(BF16).reshape(bb, ts, d)


def _s5_mixer(x, mod, g1, lam_re, lam_im, log_dt, b_re, b_im, c_re, c_im, d_skip):
    bsz, seq, d = x.shape
    n_slab = d // LANE
    dt = jnp.exp(log_dt)[:, None]
    mag = jnp.exp(lam_re * dt)
    lb_re = mag * jnp.cos(lam_im * dt)
    lb_im = mag * jnp.sin(lam_im * dt)
    den = lam_re * lam_re + lam_im * lam_im
    f_re = ((lb_re - 1.0) * lam_re + lb_im * lam_im) / den
    f_im = (lb_im * lam_re - (lb_re - 1.0) * lam_im) / den
    bb_re = f_re[..., None] * b_re - f_im[..., None] * b_im
    bb_im = f_re[..., None] * b_im + f_im[..., None] * b_re
    eye = jnp.eye(S5_SLAB, dtype=F32)
    wid = S5_SLAB * S5_STATE

    def in_mat(b):
        b = b.reshape(n_slab, S5_SLAB, S5_STATE, S5_GROUP)
        return jnp.einsum('kgpi,gh->kgihp', b, eye).reshape(n_slab, LANE, wid)

    def out_mat(c):
        c = c.reshape(n_slab, S5_SLAB, S5_GROUP, S5_STATE)
        return jnp.einsum('kgip,gh->kgphi', c, eye).reshape(n_slab, wid, LANE)

    bm = jnp.concatenate([in_mat(bb_re), in_mat(bb_im)], axis=2).astype(BF16)
    cm = jnp.concatenate([out_mat(c_re), -out_mat(c_im)], axis=1).astype(BF16)
    a_re = lb_re.reshape(n_slab, 1, wid)
    a_im = lb_im.reshape(n_slab, 1, wid)

    bb, ts = S5_BB, 64
    rows = bb * ts
    r = jnp.arange(rows)
    perm = ((r % bb)[:, None] * ts + (r // bb)[:, None] == r[None, :]).astype(BF16)
    blk = lambda i, t: (i, t, 0)
    return pl.pallas_call(
        _s5_kernel,
        out_shape=jax.ShapeDtypeStruct((bsz, seq, d), BF16),
        grid=(bsz // bb, seq // ts),
        in_specs=[pl.BlockSpec((bb, ts, d), blk),
                  pl.BlockSpec((bb, N_MOD, d), lambda i, t: (i, 0, 0)),
                  _const_spec((1, d)),
                  _const_spec((rows, rows)), _const_spec((rows, rows)),
                  _const_spec(bm.shape), _const_spec(cm.shape),
                  _const_spec(a_re.shape), _const_spec(a_im.shape),
                  _const_spec((1, d))],
        out_specs=pl.BlockSpec((bb, ts, d), blk),
        scratch_shapes=[pltpu.VMEM((rows, 2 * wid * n_slab), F32),
                        pltpu.VMEM((bb, 2 * wid * n_slab), F32)],
        compiler_params=_cparams(2),
        name="s5_scan",
    )(x, mod, g1.reshape(1, d), perm, perm.T, bm, cm, a_re, a_im, d_skip.reshape(1, d))


def kernel(x, c, ada_w, ada_b, norm1_g, norm2_g, final_g, rel_bias, mla_w_dkv, mla_g_q, mla_g_kv, mla_w_uq, mla_w_ukv, mla_w_o, nsa_w_in, nsa_pe_k, nsa_w1_k, nsa_w2_k, nsa_pe_v, nsa_w1_v, nsa_w2_v, nsa_w_o, s5_lam_re, s5_lam_im, s5_log_dt, s5_b_re, s5_b_im, s5_c_re, s5_c_im, s5_d, s5_w_glu, ffn_w_up, ffn_conv_w, ffn_conv_b, ffn_w_down):
    depth = ada_w.shape[0]
    mods = _ada_mod(c, ada_w, ada_b)
    for i in range(depth):
        mod = mods[i]
        kind, j = i % N_MIXERS, i // N_MIXERS
        if kind == 0:
            o = _mla_mixer(x, mod, norm1_g[i], mla_w_dkv[j], mla_g_q[j], mla_g_kv[j],
                           mla_w_uq[j], mla_w_ukv[j])
            w_post, glu = mla_w_o[j], False
        elif kind == 1:
            o = _nsa_mixer(x, mod, norm1_g[i], nsa_w_in[j], nsa_pe_k[j], nsa_w1_k[j], nsa_w2_k[j],
                           nsa_pe_v[j], nsa_w1_v[j], nsa_w2_v[j], rel_bias)
            w_post, glu = nsa_w_o[j], False
        else:
            o = _s5_mixer(x, mod, norm1_g[i], s5_lam_re[j], s5_lam_im[j], s5_log_dt[j],
                          s5_b_re[j], s5_b_im[j], s5_c_re[j], s5_c_im[j], s5_d[j])
            w_post, glu = s5_w_glu[j], True
        x = _post_ffn(x, o, mod, w_post.astype(BF16), norm2_g[i], ffn_w_up[i].astype(BF16),
                      ffn_conv_w[i], ffn_conv_b[i], ffn_w_down[i].astype(BF16), final_g,
                      glu=glu, final=(i == depth - 1))
    return x
```

```python
import functools
import math

import jax
import jax.numpy as jnp
from jax import lax
from jax.experimental import pallas as pl
from jax.experimental.pallas import tpu as pltpu

F32 = jnp.float32
BF16 = jnp.bfloat16

NORM_EPS = 1e-6
N_MOD = 6
N_MIXERS = 3

MLA_HEADS = 8
MLA_Q_LORA = 512
MLA_KV_LORA = 256
MLA_NOPE = 128
MLA_ROPE = 64
MLA_V = 128
MLA_QK = MLA_NOPE + MLA_ROPE
ROPE_THETA = 10000.0

NSA_HEADS = 8
NSA_GROUPS = 2
NSA_HPG = NSA_HEADS // NSA_GROUPS
NSA_DK = 128
NSA_DV = 128
CMP_BLOCK = 32
CMP_STRIDE = 16
CMP_RATIO = CMP_BLOCK // CMP_STRIDE
CMP_HIDDEN = 256
SLC_BLOCK = 64
SLC_TOPN = 16
N_LOCAL = 2
FORCE_BONUS = 1e4
WINDOW = 512
REL_BUCKETS = 32
REL_MAX_DIST = 128

S5_GROUP = 16
S5_STATE = 64
CONV_WIDTH = 3

LANE = 128
NEG = -1e30
LOG2E = math.log2(math.e)
VMEM_LIMIT = 56 * 1024 * 1024

ATT_TILE = 128
MLA_TQ = 256
S5_BB = 8
S5_SLAB = 8


def _cparams(n_axes):
    return pltpu.CompilerParams(
        dimension_semantics=("arbitrary",) * n_axes,
        vmem_limit_bytes=VMEM_LIMIT)


def _const_spec(shape):
    zeros = (0,) * len(shape)
    return pl.BlockSpec(shape, lambda *_: zeros, pipeline_mode=pl.Buffered(1))


def _dot(a, b):
    return jnp.dot(a, b, preferred_element_type=F32)


def _dot_nt(a, b):
    return lax.dot_general(a, b, (((1,), (1,)), ((), ())), preferred_element_type=F32)


def _rms(x, g):
    return x * lax.rsqrt(jnp.mean(x * x, axis=-1, keepdims=True) + NORM_EPS) * g


def _norm_mod(x, g, shift, scale):
    return _rms(x, g) * (1.0 + scale) + shift


def _gelu_tanh(x):
    c = math.sqrt(2.0 / math.pi)
    return 0.5 * x * (1.0 + jnp.tanh(c * (x + 0.044715 * (x * x * x))))


def _sigmoid(x):
    return 1.0 / (1.0 + jnp.exp(-x))


def _t5_bucket(dist):
    n = jnp.maximum(dist, 0)
    max_exact = REL_BUCKETS // 2
    nf = jnp.maximum(n, 1).astype(F32)
    large = max_exact + (jnp.log(nf / max_exact) / math.log(REL_MAX_DIST / max_exact)
                         * (REL_BUCKETS - max_exact)).astype(jnp.int32)
    large = jnp.minimum(large, REL_BUCKETS - 1)
    return jnp.where(n < max_exact, n, large)


def _ada_kernel(c_ref, w_ref, b_ref, o_ref):
    c = c_ref[...]
    c_act = c * _sigmoid(c)
    o_ref[0] = jnp.dot(c_act, w_ref[0], preferred_element_type=F32,
                       precision=lax.Precision.HIGHEST) + b_ref[0]


def _ada_mod(c, ada_w, ada_b):
    depth, d, n = ada_w.shape
    bsz = c.shape[0]
    tn = 1024
    out = pl.pallas_call(
        _ada_kernel,
        out_shape=jax.ShapeDtypeStruct((depth, bsz, n), F32),
        grid=(depth, n // tn),
        in_specs=[pl.BlockSpec((bsz, d), lambda i, j: (0, 0)),
                  pl.BlockSpec((1, d, tn), lambda i, j: (i, 0, j)),
                  pl.BlockSpec((1, 1, tn), lambda i, j: (i, 0, j))],
        out_specs=pl.BlockSpec((1, bsz, tn), lambda i, j: (i, 0, j)),
        compiler_params=_cparams(2),
        name="ada_mod",
    )(c, ada_w, ada_b.reshape(depth, 1, n))
    return out.reshape(depth, bsz, N_MOD, d)


def _post_ffn_kernel(x_ref, o_ref, mod_ref, wpost_ref, g2_ref, wup_ref, cw_ref, cb_ref,
                     wdown_ref, gf_ref, out_ref, act_ref, carry_ref,
                     *, glu, final, fc):
    tm = x_ref.shape[1]
    d_ff = wdown_ref.shape[0]

    @pl.when(pl.program_id(1) == 0)
    def _():
        carry_ref[...] = jnp.zeros_like(carry_ref)

    x = x_ref[0]
    mod = mod_ref[0]
    y = _dot(o_ref[0], wpost_ref[...])
    if glu:
        d = x.shape[-1]
        y = y[:, :d] * _sigmoid(y[:, d:])
    x1 = x + mod[2:3] * y
    h = _norm_mod(x1, g2_ref[...], mod[3:4], mod[4:5]).astype(BF16)

    row = lax.broadcasted_iota(jnp.int32, (8, fc), 0)

    def conv(z, col):
        c = carry_ref[:, col:col + fc]
        carry_ref[:, col:col + fc] = z[tm - 8:, :]
        r1 = pltpu.roll(z, 1, 0)
        r2 = pltpu.roll(z, 2, 0)
        s1 = jnp.concatenate([jnp.where(row < 1, pltpu.roll(c, 1, 0), r1[:8]), r1[8:]], axis=0)
        s2 = jnp.concatenate([jnp.where(row < 2, pltpu.roll(c, 2, 0), r2[:8]), r2[8:]], axis=0)
        w = cw_ref[:, col:col + fc]
        return w[0:1] * s2 + w[1:2] * s1 + w[2:3] * z + cb_ref[:, col:col + fc]

    for f in range(0, d_ff, fc):
        val = conv(_dot(h, wup_ref[:, f:f + fc]), f)
        gate = conv(_dot(h, wup_ref[:, d_ff + f:d_ff + f + fc]), d_ff + f)
        act_ref[:, f:f + fc] = (val * (gate * _sigmoid(gate))).astype(BF16)

    x2 = x1 + mod[5:6] * _dot(act_ref[...], wdown_ref[...])
    if final:
        x2 = _rms(x2, gf_ref[...])
    out_ref[0] = x2


def _post_ffn(x, o, mod, w_post, g2, w_up, conv_w, conv_b, w_down, final_g, *, glu, final):
    bsz, seq, d = x.shape
    d_ff = w_down.shape[0]
    tm = 512
    fc = 256
    row = lambda b, i: (b, i, 0)
    return pl.pallas_call(
        functools.partial(_post_ffn_kernel, glu=glu, final=final, fc=fc),
        out_shape=jax.ShapeDtypeStruct((bsz, seq, d), F32),
        grid=(bsz, seq // tm),
        in_specs=[pl.BlockSpec((1, tm, d), row),
                  pl.BlockSpec((1, tm, d), row),
                  pl.BlockSpec((1, N_MOD, d), lambda b, i: (b, 0, 0)),
                  _const_spec(w_post.shape),
                  _const_spec((1, d)),
                  _const_spec(w_up.shape),
                  _const_spec(conv_w.shape),
                  _const_spec((1, 2 * d_ff)),
                  _const_spec(w_down.shape),
                  _const_spec((1, d))],
        out_specs=pl.BlockSpec((1, tm, d), row),
        scratch_shapes=[pltpu.VMEM((tm, d_ff), BF16),
                        pltpu.VMEM((8, 2 * d_ff), F32)],
        compiler_params=_cparams(2),
        name="post_ffn",
    )(x, o, mod, w_post, g2.reshape(1, d), w_up, conv_w, conv_b.reshape(1, -1), w_down,
      final_g.reshape(1, d))


def _rope_tables(seq, reps):
    half = MLA_ROPE // 2
    inv = ROPE_THETA ** (-jnp.arange(half, dtype=F32) / half)
    ang = jnp.arange(seq).astype(F32)[:, None] * inv[None, :]
    cos, sin = jnp.cos(ang), jnp.sin(ang)
    cos64 = jnp.concatenate([cos, cos], axis=1)
    sin64 = jnp.concatenate([-sin, sin], axis=1)
    return jnp.tile(cos64, (1, reps)), jnp.tile(sin64, (1, reps))


def _mla_proj_kernel(x_ref, mod_ref, g1_ref, wdkv_ref, gq_ref, gkv_ref, wuq_ref, wukv_ref,
                     cos_ref, sin_ref, qn_ref, qr_ref, kn_ref, kr_ref, v_ref):
    nq = MLA_HEADS * MLA_NOPE
    nr = MLA_HEADS * MLA_ROPE
    scale = MLA_QK ** -0.5 * LOG2E
    mod = mod_ref[0]
    h = _norm_mod(x_ref[0], g1_ref[...], mod[0:1], mod[1:2]).astype(BF16)
    dkv = _dot(h, wdkv_ref[...])
    c_q = _rms(dkv[:, :MLA_Q_LORA], gq_ref[...]).astype(BF16)
    c_kv = _rms(dkv[:, MLA_Q_LORA:MLA_Q_LORA + MLA_KV_LORA], gkv_ref[...]).astype(BF16)
    cos = cos_ref[...]
    sin = sin_ref[...]
    r0 = MLA_Q_LORA + MLA_KV_LORA
    kr_ref[0] = (dkv[:, r0:r0 + LANE] * cos[:, :LANE]
                 + dkv[:, r0 + LANE:r0 + 2 * LANE] * sin[:, :LANE]).astype(BF16)
    q = _dot(c_q, wuq_ref[...])
    qn_ref[0] = (q[:, :nq] * scale).astype(BF16)
    qr_ref[0] = ((q[:, nq:nq + nr] * cos + q[:, nq + nr:] * sin) * scale).astype(BF16)
    kv = _dot(c_kv, wukv_ref[...])
    kn_ref[0] = kv[:, :nq].astype(BF16)
    v_ref[0] = kv[:, nq:].astype(BF16)


def _mla_attn_kernel(qn_ref, qr_ref, kn_ref, kr_ref, v_ref, o_ref):
    seq = qn_ref.shape[1]
    tq = MLA_TQ
    odd = pl.program_id(1) % 2
    lane = lax.broadcasted_iota(jnp.int32, (tq, LANE), 1)
    keep = (lane >= MLA_ROPE) == (odd == 1)
    row = lax.broadcasted_iota(jnp.int32, (tq, tq), 0)
    col = lax.broadcasted_iota(jnp.int32, (tq, tq), 1)
    for i in range(seq // tq):
        lo, hi = i * tq, (i + 1) * tq
        qr = jnp.where(keep, qr_ref[0, lo:hi, :], jnp.zeros((), BF16))
        q = jnp.concatenate([qn_ref[0, lo:hi, :], qr], axis=1)
        k_d = jnp.concatenate([kn_ref[0, lo:hi, :], kr_ref[0, lo:hi, :]], axis=1)
        s_d = jnp.where(col <= row, _dot_nt(q, k_d), NEG)
        m = jnp.max(s_d, axis=1, keepdims=True)
        if i > 0:
            k_o = jnp.concatenate([kn_ref[0, :lo, :], kr_ref[0, :lo, :]], axis=1)
            s_o = _dot_nt(q, k_o)
            m = jnp.maximum(m, jnp.max(s_o, axis=1, keepdims=True))
            p_o = jnp.exp2(s_o - m)
        p_d = jnp.exp2(s_d - m)
        l = jnp.sum(p_d, axis=1, keepdims=True)
        o = _dot(p_d.astype(BF16), v_ref[0, lo:hi, :])
        if i > 0:
            l = l + jnp.sum(p_o, axis=1, keepdims=True)
            o = o + _dot(p_o.astype(BF16), v_ref[0, :lo, :])
        o_ref[0, lo:hi, :] = (o / l).astype(BF16)


def _mla_mixer(x, mod, g1, w_dkv, g_q, g_kv, w_uq, w_ukv):
    bsz, seq, d = x.shape
    hds = MLA_HEADS
    half = MLA_ROPE // 2
    swap = jnp.concatenate([jnp.arange(half, MLA_ROPE), jnp.arange(half)])
    r0 = MLA_Q_LORA + MLA_KV_LORA
    kr = w_dkv[:, r0:]
    wdkv = jnp.concatenate([w_dkv[:, :r0], kr, kr, kr[:, swap], kr[:, swap]], axis=1).astype(BF16)
    wq = w_uq.reshape(MLA_Q_LORA, hds, MLA_QK)
    wq_r = wq[:, :, MLA_NOPE:]
    wuq = jnp.concatenate([wq[:, :, :MLA_NOPE].reshape(MLA_Q_LORA, -1),
                           wq_r.reshape(MLA_Q_LORA, -1),
                           wq_r[:, :, swap].reshape(MLA_Q_LORA, -1)], axis=1).astype(BF16)
    wkv = w_ukv.reshape(MLA_KV_LORA, hds, MLA_NOPE + MLA_V)
    wukv = jnp.concatenate([wkv[:, :, :MLA_NOPE].reshape(MLA_KV_LORA, -1),
                            wkv[:, :, MLA_NOPE:].reshape(MLA_KV_LORA, -1)], axis=1).astype(BF16)
    cos, sin = _rope_tables(seq, hds)
    nq, nr = hds * MLA_NOPE, hds * MLA_ROPE
    tm = 512
    row = lambda i, b: (b, i, 0)
    qn, qr, kn, kr2, v = pl.pallas_call(
        _mla_proj_kernel,
        out_shape=[jax.ShapeDtypeStruct((bsz, seq, nq), BF16),
                   jax.ShapeDtypeStruct((bsz, seq, nr), BF16),
                   jax.ShapeDtypeStruct((bsz, seq, nq), BF16),
                   jax.ShapeDtypeStruct((bsz, seq, LANE), BF16),
                   jax.ShapeDtypeStruct((bsz, seq, nq), BF16)],
        grid=(seq // tm, bsz),
        in_specs=[pl.BlockSpec((1, tm, d), row),
                  pl.BlockSpec((1, N_MOD, d), lambda i, b: (b, 0, 0)),
                  _const_spec((1, d)),
                  _const_spec(wdkv.shape),
                  _const_spec((1, MLA_Q_LORA)),
                  _const_spec((1, MLA_KV_LORA)),
                  _const_spec(wuq.shape),
                  _const_spec(wukv.shape),
                  pl.BlockSpec((tm, nr), lambda i, b: (i, 0)),
                  pl.BlockSpec((tm, nr), lambda i, b: (i, 0))],
        out_specs=[pl.BlockSpec((1, tm, nq), row),
                   pl.BlockSpec((1, tm, nr), row),
                   pl.BlockSpec((1, tm, nq), row),
                   pl.BlockSpec((1, tm, LANE), row),
                   pl.BlockSpec((1, tm, nq), row)],
        compiler_params=_cparams(2),
        name="mla_proj",
    )(x, mod, g1.reshape(1, d), wdkv, g_q.reshape(1, -1), g_kv.reshape(1, -1), wuq, wukv, cos, sin)

    head = lambda b, h: (b, 0, h)
    return pl.pallas_call(
        _mla_attn_kernel,
        out_shape=jax.ShapeDtypeStruct((bsz, seq, hds * MLA_V), BF16),
        grid=(bsz, hds),
        in_specs=[pl.BlockSpec((1, seq, LANE), head),
                  pl.BlockSpec((1, seq, LANE), lambda b, h: (b, 0, h // 2)),
                  pl.BlockSpec((1, seq, LANE), head),
                  pl.BlockSpec((1, seq, LANE), lambda b, h: (b, 0, 0)),
                  pl.BlockSpec((1, seq, LANE), head)],
        out_specs=pl.BlockSpec((1, seq, LANE), head),
        compiler_params=_cparams(2),
        name="mla_attn",
    )(qn, qr, kn, kr2, v)


def _nsa_proj_kernel(x_ref, mod_ref, g1_ref, w_ref, q_ref, kc_ref, vc_ref, ks_ref, vs_ref,
                     kw_ref, vw_ref, gate_ref):
    nq = NSA_HEADS * NSA_DK
    nkv = NSA_GROUPS * NSA_DK
    mod = mod_ref[0]
    h = _norm_mod(x_ref[0], g1_ref[...], mod[0:1], mod[1:2]).astype(BF16)
    p = _dot(h, w_ref[...])
    q_ref[0] = (p[:, :nq] * (NSA_DK ** -0.5 * LOG2E)).astype(BF16)
    for n, ref in enumerate((kc_ref, vc_ref, ks_ref, vs_ref, kw_ref, vw_ref)):
        ref[0] = p[:, nq + n * nkv:nq + (n + 1) * nkv].astype(BF16)
    gate_ref[0] = _sigmoid(p[:, nq + 6 * nkv:])


def _nsa_compress_kernel(fk_ref, fv_ref, pek_ref, pev_ref, w1k_ref, w1v_ref, w2k_ref, w2v_ref,
                         ok_ref, ov_ref):
    for f_ref, pe_ref, w1_ref, w2_ref, o_ref in ((fk_ref, pek_ref, w1k_ref, w2k_ref, ok_ref),
                                                 (fv_ref, pev_ref, w1v_ref, w2v_ref, ov_ref)):
        flat = (f_ref[...].astype(F32) + pe_ref[...]).astype(BF16)
        hid = _gelu_tanh(_dot(flat, w1_ref[...])).astype(BF16)
        o_ref[...] = _dot(hid, w2_ref[...]).astype(BF16)


def _nsa_select_kernel(rb_ref, q_ref, kc_ref, vc_ref, bucket_ref, ovl_ref, bonus_ref,
                       ocmp_ref, sel_ref, bias_ref):
    g = pl.program_id(0)
    seq = q_ref.shape[1]
    n_blk = bonus_ref.shape[0]
    bucket = bucket_ref[...]

    @pl.when(pl.program_id(1) == 0)
    def _():
        for hp in range(NSA_HPG):
            acc = jnp.zeros(bucket.shape, F32)
            for r in range(REL_BUCKETS):
                acc = jnp.where(bucket == r, rb_ref[g * NSA_HPG + hp, r] * LOG2E, acc)
            bias_ref[hp] = acc

    valid = bucket >= 0
    kc = kc_ref[0, 0]
    vc = vc_ref[0, 0]
    psum = jnp.zeros(bucket.shape, F32)
    for hp in range(NSA_HPG):
        s = _dot_nt(q_ref[0, :, hp * NSA_DK:(hp + 1) * NSA_DK], kc) + bias_ref[hp]
        s = jnp.where(valid, s, NEG)
        m = jnp.max(s, axis=1, keepdims=True)
        m = jnp.where(m > 0.5 * NEG, m, 0.0)
        p = jnp.exp2(s - m)
        p = p / jnp.maximum(jnp.sum(p, axis=1, keepdims=True), 1e-30)
        ocmp_ref[0, :, hp * NSA_DV:(hp + 1) * NSA_DV] = _dot(p.astype(BF16), vc).astype(BF16)
        psum = psum + p

    imp = lax.dot_general(ovl_ref[...], psum, (((1,), (1,)), ((), ())),
                          preferred_element_type=F32, precision=lax.Precision.HIGHEST)
    bonus = bonus_ref[...]
    score = jnp.where(bonus > 0.5 * NEG, imp + bonus, NEG)
    blk = lax.broadcasted_iota(jnp.int32, (n_blk, seq), 0)
    rank = jnp.zeros((n_blk, seq), F32)
    for i in range(n_blk):
        row = score[i:i + 1, :]
        rank = rank + jnp.where(blk > i, jnp.where(row >= score, 1.0, 0.0),
                                jnp.where(row > score, 1.0, 0.0))
    selneg = jnp.where(rank < float(SLC_TOPN), jnp.where(score > 0.5 * NEG, 0.0, NEG), NEG)
    pad = jnp.zeros((LANE - n_blk, seq), F32)
    sel_ref[0, 0] = jnp.concatenate([selneg, pad], axis=0).T.astype(BF16)


def _softmax_pv(scores, values):
    m = functools.reduce(jnp.maximum, [jnp.max(s, axis=1, keepdims=True) for s in scores])
    ps = [jnp.exp2(s - m) for s in scores]
    l = functools.reduce(jnp.add, [jnp.sum(p, axis=1, keepdims=True) for p in ps])
    o = functools.reduce(jnp.add, [_dot(p.astype(BF16), v) for p, v in zip(ps, values)])
    return o / l


def _nsa_attn_kernel(rb_ref, q_ref, sel_ref, ks_ref, vs_ref, kw_ref, vw_ref, et_ref, ocmp_ref,
                     gate_ref, bkt_ref, o_ref, kaug_ref, bias_ref):
    g = pl.program_id(0)
    t = ATT_TILE
    hp_n = NSA_HPG
    seq = q_ref.shape[1]

    @pl.when(pl.program_id(1) == 0)
    def _():
        kaug_ref[:, NSA_DK:] = et_ref[...]
        for kind in range(2):
            bkt = bkt_ref[kind]
            for hp in range(hp_n):
                hd = g * hp_n + hp
                acc = jnp.zeros((t, t), F32)
                for r in range(REL_BUCKETS - 1):
                    acc = jnp.where(
                        bkt == r, (rb_ref[hd, r] - rb_ref[hd, REL_BUCKETS - 1]) * LOG2E, acc)
                bias_ref[hp * t:(hp + 1) * t, (1 - kind) * t:(2 - kind) * t] = acc
        bias_ref[:, 2 * t:] = jnp.zeros((hp_n * t, t), F32)

    kaug_ref[:, :NSA_DK] = ks_ref[0]

    rows = hp_n * t
    r2 = lax.broadcasted_iota(jnp.int32, (rows, 2 * t), 0) & (t - 1)
    c2 = lax.broadcasted_iota(jnp.int32, (rows, 2 * t), 1)
    r1 = lax.broadcasted_iota(jnp.int32, (rows, t), 0) & (t - 1)
    c1 = lax.broadcasted_iota(jnp.int32, (rows, t), 1)
    near_ok = c2 <= r2 + t
    first_ok = c2 <= r2
    edge_ok = c1 > r1
    lane = lax.broadcasted_iota(jnp.int32, (t, LANE), 1)

    for i in range(seq // t):
        lo, hi = i * t, (i + 1) * t
        q_heads = [q_ref[0, lo:hi, hp * NSA_DK:(hp + 1) * NSA_DK] for hp in range(hp_n)]
        q_win = jnp.concatenate(q_heads, axis=0)
        sel = sel_ref[0, 0, lo:hi, :]
        q_sel = jnp.concatenate([jnp.concatenate([qh, sel], axis=1) for qh in q_heads], axis=0)
        if i == 0:
            n_lo, n_hi, bias, ok = 0, 2 * t, bias_ref[:, t:], first_ok
        else:
            n_lo, n_hi, bias, ok = lo - t, hi, bias_ref[:, :2 * t], near_ok

        scores = [jnp.where(ok, _dot_nt(q_sel, kaug_ref[n_lo:n_hi, :]) + bias, NEG)]
        values = [vs_ref[0, n_lo:n_hi, :]]
        if n_lo > 0:
            scores.append(_dot_nt(q_sel, kaug_ref[:n_lo, :]))
            values.append(vs_ref[0, :n_lo, :])
        o_slc = _softmax_pv(scores, values)

        scores = [jnp.where(ok, _dot_nt(q_win, kw_ref[0, n_lo:n_hi, :]) + bias, NEG)]
        values = [vw_ref[0, n_lo:n_hi, :]]
        w_lo = max(lo - WINDOW, 0)
        if n_lo > w_lo:
            s_m = _dot_nt(q_win, kw_ref[0, w_lo:n_lo, :])
            if lo - WINDOW >= 0:
                s_m = jnp.concatenate([jnp.where(edge_ok, s_m[:, :t], NEG), s_m[:, t:]], axis=1)
            scores.append(s_m)
            values.append(vw_ref[0, w_lo:n_lo, :])
        o_win = _softmax_pv(scores, values)

        gate = gate_ref[0, lo:hi, :]
        for hp in range(hp_n):
            hd = g * hp_n + hp

            def gcol(branch):
                return jnp.sum(jnp.where(lane == branch * NSA_HEADS + hd, gate, 0.0),
                               axis=1, keepdims=True)
            sl = slice(hp * t, (hp + 1) * t)
            o = (gcol(0) * ocmp_ref[0, lo:hi, hp * NSA_DV:(hp + 1) * NSA_DV].astype(F32)
                 + gcol(1) * o_slc[sl] + gcol(2) * o_win[sl])
            o_ref[0, lo:hi, hp * NSA_DV:(hp + 1) * NSA_DV] = o.astype(BF16)


def _nsa_mixer(x, mod, g1, w_in, pe_k, w1_k, w2_k, pe_v, w1_v, w2_v, rel_bias):
    bsz, seq, d = x.shape
    grp, hpg = NSA_GROUPS, NSA_HPG
    nq, nkv = NSA_HEADS * NSA_DK, NSA_GROUPS * NSA_DK
    n_gate = 3 * NSA_HEADS
    w = jnp.pad(w_in, ((0, 0), (0, LANE - n_gate))).astype(BF16)
    tm = 512
    row = lambda i, b: (b, i, 0)
    kv_shape = jax.ShapeDtypeStruct((bsz, seq, nkv), BF16)
    q, kc, vc, ks, vs, kw, vw, gates = pl.pallas_call(
        _nsa_proj_kernel,
        out_shape=[jax.ShapeDtypeStruct((bsz, seq, nq), BF16)] + [kv_shape] * 6
                  + [jax.ShapeDtypeStruct((bsz, seq, LANE), F32)],
        grid=(seq // tm, bsz),
        in_specs=[pl.BlockSpec((1, tm, d), row),
                  pl.BlockSpec((1, N_MOD, d), lambda i, b: (b, 0, 0)),
                  _const_spec((1, d)),
                  _const_spec(w.shape)],
        out_specs=[pl.BlockSpec((1, tm, nq), row)] + [pl.BlockSpec((1, tm, nkv), row)] * 6
                  + [pl.BlockSpec((1, tm, LANE), row)],
        compiler_params=_cparams(2),
        name="nsa_proj",
    )(x, mod, g1.reshape(1, d), w)

    n_chunk = seq // CMP_STRIDE
    n_cmp = n_chunk - CMP_RATIO + 1

    def blocks(t):
        ch = t.reshape(bsz, n_chunk, CMP_STRIDE, grp, NSA_DK).transpose(0, 3, 1, 2, 4)
        ch = ch.reshape(bsz, grp, n_chunk, CMP_STRIDE * NSA_DK)
        ch = jnp.pad(ch, ((0, 0), (0, 0), (0, CMP_RATIO - 1), (0, 0)))
        flat = jnp.concatenate([ch[:, :, r:r + n_chunk] for r in range(CMP_RATIO)], axis=3)
        return flat.reshape(bsz * grp * n_chunk, CMP_BLOCK * NSA_DK)

    fk, fv = blocks(kc), blocks(vc)
    tr = min(1024, bsz * grp * n_chunk)
    wide = CMP_BLOCK * NSA_DK
    kcmp, vcmp = pl.pallas_call(
        _nsa_compress_kernel,
        out_shape=[jax.ShapeDtypeStruct((bsz * grp * n_chunk, NSA_DK), BF16)] * 2,
        grid=(bsz * grp * n_chunk // tr,),
        in_specs=[pl.BlockSpec((tr, wide), lambda i: (i, 0)),
                  pl.BlockSpec((tr, wide), lambda i: (i, 0)),
                  _const_spec((1, wide)), _const_spec((1, wide)),
                  _const_spec((wide, CMP_HIDDEN)), _const_spec((wide, CMP_HIDDEN)),
                  _const_spec((CMP_HIDDEN, NSA_DK)), _const_spec((CMP_HIDDEN, NSA_DV))],
        out_specs=[pl.BlockSpec((tr, NSA_DK), lambda i: (i, 0))] * 2,
        compiler_params=_cparams(1),
        name="nsa_compress",
    )(fk, fv, pe_k.reshape(1, wide), pe_v.reshape(1, wide), w1_k.astype(BF16), w1_v.astype(BF16),
      w2_k.astype(BF16), w2_v.astype(BF16))
    kcmp = kcmp.reshape(bsz, grp, n_chunk, NSA_DK)
    vcmp = vcmp.reshape(bsz, grp, n_chunk, NSA_DV)

    pos = jnp.arange(seq)
    cmp_start = jnp.arange(n_chunk) * CMP_STRIDE
    dist_c = pos[:, None] - (cmp_start + CMP_BLOCK - 1)[None, :]
    ok_c = (dist_c >= 0) & (jnp.arange(n_chunk) < n_cmp)[None, :]
    bucket_c = jnp.where(ok_c, _t5_bucket(dist_c), -1).astype(jnp.int32)
    n_blk = seq // SLC_BLOCK
    blk = jnp.arange(n_blk)
    blk_start = blk * SLC_BLOCK
    overlap_t = ((cmp_start[None, :] < blk_start[:, None] + SLC_BLOCK)
                 & (cmp_start[None, :] + CMP_BLOCK > blk_start[:, None])
                 & (jnp.arange(n_chunk) < n_cmp)[None, :]).astype(F32)
    back = (pos // SLC_BLOCK)[None, :] - blk[:, None]
    forced = (blk[:, None] == 0) | ((back >= 0) & (back < N_LOCAL))
    causal_blk = blk_start[:, None] <= pos[None, :]
    bonus_t = jnp.where(causal_blk, jnp.where(forced, FORCE_BONUS, 0.0), NEG).astype(F32)

    ocmp, selneg = pl.pallas_call(
        _nsa_select_kernel,
        out_shape=[jax.ShapeDtypeStruct((bsz, seq, nq), BF16),
                   jax.ShapeDtypeStruct((bsz, grp, seq, LANE), BF16)],
        grid=(grp, bsz),
        in_specs=[pl.BlockSpec(memory_space=pltpu.SMEM),
                  pl.BlockSpec((1, seq, hpg * NSA_DK), lambda g, b: (b, 0, g)),
                  pl.BlockSpec((1, 1, n_chunk, NSA_DK), lambda g, b: (b, g, 0, 0)),
                  pl.BlockSpec((1, 1, n_chunk, NSA_DV), lambda g, b: (b, g, 0, 0)),
                  _const_spec(bucket_c.shape),
                  _const_spec(overlap_t.shape),
                  _const_spec(bonus_t.shape)],
        out_specs=[pl.BlockSpec((1, seq, hpg * NSA_DV), lambda g, b: (b, 0, g)),
                   pl.BlockSpec((1, 1, seq, LANE), lambda g, b: (b, g, 0, 0))],
        scratch_shapes=[pltpu.VMEM((hpg, seq, n_chunk), F32)],
        compiler_params=_cparams(2),
        name="nsa_select",
    )(rel_bias, q, kcmp, vcmp, bucket_c, overlap_t, bonus_t)

    t = ATT_TILE
    rr = jnp.arange(t)
    near = jnp.stack([_t5_bucket(k * t + rr[:, None] - rr[None, :]) for k in range(2)]).astype(jnp.int32)
    e_t = (jnp.arange(seq)[:, None] // SLC_BLOCK == jnp.arange(LANE)[None, :]).astype(BF16)
    grp_blk = lambda g, b: (b, 0, g)
    return pl.pallas_call(
        _nsa_attn_kernel,
        out_shape=jax.ShapeDtypeStruct((bsz, seq, nq), BF16),
        grid=(grp, bsz),
        in_specs=[pl.BlockSpec(memory_space=pltpu.SMEM),
                  pl.BlockSpec((1, seq, hpg * NSA_DK), grp_blk),
                  pl.BlockSpec((1, 1, seq, LANE), lambda g, b: (b, g, 0, 0)),
                  pl.BlockSpec((1, seq, NSA_DK), grp_blk),
                  pl.BlockSpec((1, seq, NSA_DV), grp_blk),
                  pl.BlockSpec((1, seq, NSA_DK), grp_blk),
                  pl.BlockSpec((1, seq, NSA_DV), grp_blk),
                  _const_spec(e_t.shape),
                  pl.BlockSpec((1, seq, hpg * NSA_DV), grp_blk),
                  pl.BlockSpec((1, seq, LANE), lambda g, b: (b, 0, 0)),
                  _const_spec(near.shape)],
        out_specs=pl.BlockSpec((1, seq, hpg * NSA_DV), grp_blk),
        scratch_shapes=[pltpu.VMEM((seq, NSA_DK + LANE), BF16),
                        pltpu.VMEM((hpg * t, 3 * t), F32)],
        compiler_params=_cparams(2),
        name="nsa_attn",
    )(rel_bias, q, selneg, ks, vs, kw, vw, e_t, ocmp, gates, near)


def _s5_kernel(x_ref, mod_ref, g1_ref, perm_ref, permt_ref, bm_ref, cm_ref, are_ref, aim_ref,
               dskip_ref, y_ref, bu_ref, state_ref):
    bb, ts, d = x_ref.shape
    rows = bb * ts
    n_slab = d // LANE
    wid = S5_SLAB * S5_STATE

    @pl.when(pl.program_id(1) == 0)
    def _():
        state_ref[...] = jnp.zeros_like(state_ref)

    mod = mod_ref[...]
    u = (_rms(x_ref[...], g1_ref[...]) * (1.0 + mod[:, 1:2, :]) + mod[:, 0:1, :]).reshape(rows, d)
    u_tb = _dot(perm_ref[...], u.astype(BF16)).astype(BF16)
    for k in range(n_slab):
        bu_ref[:, 2 * k * wid:2 * (k + 1) * wid] = _dot(u_tb[:, k * LANE:(k + 1) * LANE], bm_ref[k])

    for k in range(n_slab):
        c_re, c_im = 2 * k * wid, (2 * k + 1) * wid
        a_re = jnp.broadcast_to(are_ref[k], (bb, wid))
        a_im = jnp.broadcast_to(aim_ref[k], (bb, wid))

        def body(t, carry, c_re=c_re, c_im=c_im, a_re=a_re, a_im=a_im):
            xr, xi = carry
            r0 = pl.multiple_of(t * bb, bb)
            nr = a_re * xr - a_im * xi + bu_ref[pl.ds(r0, bb), c_re:c_re + wid]
            ni = a_re * xi + a_im * xr + bu_ref[pl.ds(r0, bb), c_im:c_im + wid]
            bu_ref[pl.ds(r0, bb), c_re:c_re + wid] = nr
            bu_ref[pl.ds(r0, bb), c_im:c_im + wid] = ni
            return nr, ni

        xr, xi = lax.fori_loop(0, ts, body, (state_ref[:, c_re:c_re + wid],
                                             state_ref[:, c_im:c_im + wid]), unroll=4)
        state_ref[:, c_re:c_re + wid] = xr
        state_ref[:, c_im:c_im + wid] = xi

    y_tb = jnp.concatenate(
        [_dot(bu_ref[:, 2 * k * wid:2 * (k + 1) * wid].astype(BF16), cm_ref[k]).astype(BF16)
         for k in range(n_slab)], axis=1)
    y = _dot(permt_ref[...], y_tb) + dskip_ref[...] * u
    y_ref[...] = _gelu_tanh(y).astype(BF16).reshape(bb, ts, d)


def _s5_mixer(x, mod, g1, lam_re, lam_im, log_dt, b_re, b_im, c_re, c_im, d_skip):
    bsz, seq, d = x.shape
    n_slab = d // LANE
    dt = jnp.exp(log_dt)[:, None]
    mag = jnp.exp(lam_re * dt)
    lb_re = mag * jnp.cos(lam_im * dt)
    lb_im = mag * jnp.sin(lam_im * dt)
    den = lam_re * lam_re + lam_im * lam_im
    f_re = ((lb_re - 1.0) * lam_re + lb_im * lam_im) / den
    f_im = (lb_im * lam_re - (lb_re - 1.0) * lam_im) / den
    bb_re = f_re[..., None] * b_re - f_im[..., None] * b_im
    bb_im = f_re[..., None] * b_im + f_im[..., None] * b_re
    eye = jnp.eye(S5_SLAB, dtype=F32)
    wid = S5_SLAB * S5_STATE

    def in_mat(b):
        b = b.reshape(n_slab, S5_SLAB, S5_STATE, S5_GROUP)
        return jnp.einsum('kgpi,gh->kgihp', b, eye).reshape(n_slab, LANE, wid)

    def out_mat(c):
        c = c.reshape(n_slab, S5_SLAB, S5_GROUP, S5_STATE)
        return jnp.einsum('kgip,gh->kgphi', c, eye).reshape(n_slab, wid, LANE)

    bm = jnp.concatenate([in_mat(bb_re), in_mat(bb_im)], axis=2).astype(BF16)
    cm = jnp.concatenate([out_mat(c_re), -out_mat(c_im)], axis=1).astype(BF16)
    a_re = lb_re.reshape(n_slab, 1, wid)
    a_im = lb_im.reshape(n_slab, 1, wid)

    bb, ts = S5_BB, 64
    rows = bb * ts
    r = jnp.arange(rows)
    perm = ((r % bb)[:, None] * ts + (r // bb)[:, None] == r[None, :]).astype(BF16)
    blk = lambda i, t: (i, t, 0)
    return pl.pallas_call(
        _s5_kernel,
        out_shape=jax.ShapeDtypeStruct((bsz, seq, d), BF16),
        grid=(bsz // bb, seq // ts),
        in_specs=[pl.BlockSpec((bb, ts, d), blk),
                  pl.BlockSpec((bb, N_MOD, d), lambda i, t: (i, 0, 0)),
                  _const_spec((1, d)),
                  _const_spec((rows, rows)), _const_spec((rows, rows)),
                  _const_spec(bm.shape), _const_spec(cm.shape),
                  _const_spec(a_re.shape), _const_spec(a_im.shape),
                  _const_spec((1, d))],
        out_specs=pl.BlockSpec((bb, ts, d), blk),
        scratch_shapes=[pltpu.VMEM((rows, 2 * wid * n_slab), F32),
                        pltpu.VMEM((bb, 2 * wid * n_slab), F32)],
        compiler_params=_cparams(2),
        name="s5_scan",
    )(x, mod, g1.reshape(1, d), perm, perm.T, bm, cm, a_re, a_im, d_skip.reshape(1, d))


def kernel(x, c, ada_w, ada_b, norm1_g, norm2_g, final_g, rel_bias, mla_w_dkv, mla_g_q, mla_g_kv, mla_w_uq, mla_w_ukv, mla_w_o, nsa_w_in, nsa_pe_k, nsa_w1_k, nsa_w2_k, nsa_pe_v, nsa_w1_v, nsa_w2_v, nsa_w_o, s5_lam_re, s5_lam_im, s5_log_dt, s5_b_re, s5_b_im, s5_c_re, s5_c_im, s5_d, s5_w_glu, ffn_w_up, ffn_conv_w, ffn_conv_b, ffn_w_down):
    depth = ada_w.shape[0]
    mods = _ada_mod(c, ada_w, ada_b)
    for i in range(depth):
        mod = mods[i]
        kind, j = i % N_MIXERS, i // N_MIXERS
        if kind == 0:
            o = _mla_mixer(x, mod, norm1_g[i], mla_w_dkv[j], mla_g_q[j], mla_g_kv[j],
                           mla_w_uq[j], mla_w_ukv[j])
            w_post, glu = mla_w_o[j], False
        elif kind == 1:
            o = _nsa_mixer(x, mod, norm1_g[i], nsa_w_in[j], nsa_pe_k[j], nsa_w1_k[j], nsa_w2_k[j],
                           nsa_pe_v[j], nsa_w1_v[j], nsa_w2_v[j], rel_bias)
            w_post, glu = nsa_w_o[j], False
        else:
            o = _s5_mixer(x, mod, norm1_g[i], s5_lam_re[j], s5_lam_im[j], s5_log_dt[j],
                          s5_b_re[j], s5_b_im[j], s5_c_re[j], s5_c_im[j], s5_d[j])
            w_post, glu = s5_w_glu[j], True
        x = _post_ffn(x, o, mod, w_post.astype(BF16), norm2_g[i], ffn_w_up[i].astype(BF16),
                      ffn_conv_w[i], ffn_conv_b[i], ffn_w_down[i].astype(BF16), final_g,
                      glu=glu, final=(i == depth - 1))
    return x
```

```python
import functools
import math

import jax
import jax.numpy as jnp
from jax import lax
from jax.experimental import pallas as pl
from jax.experimental.pallas import tpu as pltpu

F32 = jnp.float32
BF16 = jnp.bfloat16

NORM_EPS = 1e-6
N_MOD = 6
N_MIXERS = 3

MLA_HEADS = 8
MLA_Q_LORA = 512
MLA_KV_LORA = 256
MLA_NOPE = 128
MLA_ROPE = 64
MLA_V = 128
MLA_QK = MLA_NOPE + MLA_ROPE
ROPE_THETA = 10000.0

NSA_HEADS = 8
NSA_GROUPS = 2
NSA_HPG = NSA_HEADS // NSA_GROUPS
NSA_DK = 128
NSA_DV = 128
CMP_BLOCK = 32
CMP_STRIDE = 16
CMP_RATIO = CMP_BLOCK // CMP_STRIDE
CMP_HIDDEN = 256
SLC_BLOCK = 64
SLC_TOPN = 16
N_LOCAL = 2
FORCE_BONUS = 1e4
WINDOW = 512
REL_BUCKETS = 32
REL_MAX_DIST = 128

S5_GROUP = 16
S5_STATE = 64
CONV_WIDTH = 3

LANE = 128
NEG = -1e30
LOG2E = math.log2(math.e)
VMEM_LIMIT = 56 * 1024 * 1024

ATT_TILE = 128
MLA_TQ = 256
S5_BB = 8
S5_SLAB = 8


def _cparams(n_axes):
    return pltpu.CompilerParams(
        dimension_semantics=("arbitrary",) * n_axes,
        vmem_limit_bytes=VMEM_LIMIT)


def _const_spec(shape):
    zeros = (0,) * len(shape)
    return pl.BlockSpec(shape, lambda *_: zeros, pipeline_mode=pl.Buffered(1))


def _dot(a, b):
    return jnp.dot(a, b, preferred_element_type=F32)


def _dot_nt(a, b):
    return lax.dot_general(a, b, (((1,), (1,)), ((), ())), preferred_element_type=F32)


def _rms(x, g):
    return x * lax.rsqrt(jnp.mean(x * x, axis=-1, keepdims=True) + NORM_EPS) * g


def _norm_mod(x, g, shift, scale):
    return _rms(x, g) * (1.0 + scale) + shift


def _gelu_tanh(x):
    c = math.sqrt(2.0 / math.pi)
    return 0.5 * x * (1.0 + jnp.tanh(c * (x + 0.044715 * (x * x * x))))


def _sigmoid(x):
    return 1.0 / (1.0 + jnp.exp(-x))


def _softmax_pv(scores, values):
    m = functools.reduce(jnp.maximum, [jnp.max(s, axis=1, keepdims=True) for s in scores])
    ps = [jnp.exp2(s - m) for s in scores]
    l = functools.reduce(jnp.add, [jnp.sum(p, axis=1, keepdims=True) for p in ps])
    o = functools.reduce(jnp.add, [_dot(p.astype(BF16), v) for p, v in zip(ps, values)])
    return o / l


def _t5_bucket(dist):
    n = jnp.maximum(dist, 0)
    max_exact = REL_BUCKETS // 2
    nf = jnp.maximum(n, 1).astype(F32)
    large = max_exact + (jnp.log(nf / max_exact) / math.log(REL_MAX_DIST / max_exact)
                         * (REL_BUCKETS - max_exact)).astype(jnp.int32)
    large = jnp.minimum(large, REL_BUCKETS - 1)
    return jnp.where(n < max_exact, n, large)


def _ada_kernel(c_ref, w_ref, b_ref, o_ref):
    c = c_ref[...]
    c_act = c * _sigmoid(c)
    o_ref[0] = jnp.dot(c_act, w_ref[0], preferred_element_type=F32,
                       precision=lax.Precision.HIGHEST) + b_ref[0]


def _ada_mod(c, ada_w, ada_b):
    depth, d, n = ada_w.shape
    bsz = c.shape[0]
    tn = 1024
    out = pl.pallas_call(
        _ada_kernel,
        out_shape=jax.ShapeDtypeStruct((depth, bsz, n), F32),
        grid=(depth, n // tn),
        in_specs=[pl.BlockSpec((bsz, d), lambda i, j: (0, 0)),
                  pl.BlockSpec((1, d, tn), lambda i, j: (i, 0, j)),
                  pl.BlockSpec((1, 1, tn), lambda i, j: (i, 0, j))],
        out_specs=pl.BlockSpec((1, bsz, tn), lambda i, j: (i, 0, j)),
        compiler_params=_cparams(2),
        name="ada_mod",
    )(c, ada_w, ada_b.reshape(depth, 1, n))
    return out.reshape(depth, bsz, N_MOD, d)


def _post_ffn_kernel(x_ref, o_ref, mod_ref, wpost_ref, g2_ref, wup_ref, cw_ref, cb_ref,
                     wdown_ref, gf_ref, out_ref, act_ref, carry_ref,
                     *, glu, final, fc):
    tm = x_ref.shape[1]
    d_ff = wdown_ref.shape[0]

    @pl.when(pl.program_id(1) == 0)
    def _():
        carry_ref[...] = jnp.zeros_like(carry_ref)

    x = x_ref[0]
    mod = mod_ref[0]
    y = _dot(o_ref[0], wpost_ref[...])
    if glu:
        d = x.shape[-1]
        y = y[:, :d] * _sigmoid(y[:, d:])
    x1 = x + mod[2:3] * y
    h = _norm_mod(x1, g2_ref[...], mod[3:4], mod[4:5]).astype(BF16)

    row = lax.broadcasted_iota(jnp.int32, (8, fc), 0)

    def conv(z, col):
        c = carry_ref[:, col:col + fc]
        carry_ref[:, col:col + fc] = z[tm - 8:, :]
        r1 = pltpu.roll(z, 1, 0)
        r2 = pltpu.roll(z, 2, 0)
        s1 = jnp.concatenate([jnp.where(row < 1, pltpu.roll(c, 1, 0), r1[:8]), r1[8:]], axis=0)
        s2 = jnp.concatenate([jnp.where(row < 2, pltpu.roll(c, 2, 0), r2[:8]), r2[8:]], axis=0)
        w = cw_ref[:, col:col + fc]
        return w[0:1] * s2 + w[1:2] * s1 + w[2:3] * z + cb_ref[:, col:col + fc]

    for f in range(0, d_ff, fc):
        val = conv(_dot(h, wup_ref[:, f:f + fc]), f)
        gate = conv(_dot(h, wup_ref[:, d_ff + f:d_ff + f + fc]), d_ff + f)
        act_ref[:, f:f + fc] = (val * (gate * _sigmoid(gate))).astype(BF16)

    x2 = x1 + mod[5:6] * _dot(act_ref[...], wdown_ref[...])
    if final:
        x2 = _rms(x2, gf_ref[...])
    out_ref[0] = x2


def _post_ffn(x, o, mod, w_post, g2, w_up, conv_w, conv_b, w_down, final_g, *, glu, final):
    bsz, seq, d = x.shape
    d_ff = w_down.shape[0]
    tm = 512
    fc = 256
    row = lambda b, i: (b, i, 0)
    return pl.pallas_call(
        functools.partial(_post_ffn_kernel, glu=glu, final=final, fc=fc),
        out_shape=jax.ShapeDtypeStruct((bsz, seq, d), F32),
        grid=(bsz, seq // tm),
        in_specs=[pl.BlockSpec((1, tm, d), row),
                  pl.BlockSpec((1, tm, d), row),
                  pl.BlockSpec((1, N_MOD, d), lambda b, i: (b, 0, 0)),
                  _const_spec(w_post.shape),
                  _const_spec((1, d)),
                  _const_spec(w_up.shape),
                  _const_spec(conv_w.shape),
                  _const_spec((1, 2 * d_ff)),
                  _const_spec(w_down.shape),
                  _const_spec((1, d))],
        out_specs=pl.BlockSpec((1, tm, d), row),
        scratch_shapes=[pltpu.VMEM((tm, d_ff), BF16),
                        pltpu.VMEM((8, 2 * d_ff), F32)],
        compiler_params=_cparams(2),
        name="post_ffn",
    )(x, o, mod, w_post, g2.reshape(1, d), w_up, conv_w, conv_b.reshape(1, -1), w_down,
      final_g.reshape(1, d))


def _rope_tables(seq, reps):
    half = MLA_ROPE // 2
    inv = ROPE_THETA ** (-jnp.arange(half, dtype=F32) / half)
    ang = jnp.arange(seq).astype(F32)[:, None] * inv[None, :]
    cos, sin = jnp.cos(ang), jnp.sin(ang)
    cos64 = jnp.concatenate([cos, cos], axis=1)
    sin64 = jnp.concatenate([-sin, sin], axis=1)
    return jnp.tile(cos64, (1, reps)), jnp.tile(sin64, (1, reps))


def _mla_proj_kernel(x_ref, mod_ref, g1_ref, wdkv_ref, gq_ref, gkv_ref, wuq_ref, wukv_ref,
                     cos_ref, sin_ref, qn_ref, qr_ref, kn_ref, kr_ref, v_ref):
    nq = MLA_HEADS * MLA_NOPE
    nr = MLA_HEADS * MLA_ROPE
    scale = MLA_QK ** -0.5 * LOG2E
    mod = mod_ref[0]
    h = _norm_mod(x_ref[0], g1_ref[...], mod[0:1], mod[1:2]).astype(BF16)
    dkv = _dot(h, wdkv_ref[...])
    c_q = _rms(dkv[:, :MLA_Q_LORA], gq_ref[...]).astype(BF16)
    c_kv = _rms(dkv[:, MLA_Q_LORA:MLA_Q_LORA + MLA_KV_LORA], gkv_ref[...]).astype(BF16)
    cos = cos_ref[...]
    sin = sin_ref[...]
    r0 = MLA_Q_LORA + MLA_KV_LORA
    kr_ref[0] = (dkv[:, r0:r0 + LANE] * cos[:, :LANE]
                 + dkv[:, r0 + LANE:r0 + 2 * LANE] * sin[:, :LANE]).astype(BF16)
    q = _dot(c_q, wuq_ref[...])
    qn_ref[0] = (q[:, :nq] * scale).astype(BF16)
    qr_ref[0] = ((q[:, nq:nq + nr] * cos + q[:, nq + nr:] * sin) * scale).astype(BF16)
    kv = _dot(c_kv, wukv_ref[...])
    kn_ref[0] = kv[:, :nq].astype(BF16)
    v_ref[0] = kv[:, nq:].astype(BF16)


def _mla_attn_kernel(qn_ref, qr_ref, kn_ref, kr_ref, v_ref, o_ref):
    seq = qn_ref.shape[1]
    tq = MLA_TQ
    odd = pl.program_id(1) % 2
    lane = lax.broadcasted_iota(jnp.int32, (tq, LANE), 1)
    keep = (lane >= MLA_ROPE) == (odd == 1)
    row = lax.broadcasted_iota(jnp.int32, (tq, tq), 0)
    col = lax.broadcasted_iota(jnp.int32, (tq, tq), 1)
    def scores(i):
        lo, hi = i * tq, (i + 1) * tq
        qr = jnp.where(keep, qr_ref[0, lo:hi, :], jnp.zeros((), BF16))
        q = jnp.concatenate([qn_ref[0, lo:hi, :], qr], axis=1)
        k_d = jnp.concatenate([kn_ref[0, lo:hi, :], kr_ref[0, lo:hi, :]], axis=1)
        out = [jnp.where(col <= row, _dot_nt(q, k_d), NEG)]
        if i > 0:
            k_o = jnp.concatenate([kn_ref[0, :lo, :], kr_ref[0, :lo, :]], axis=1)
            out.append(_dot_nt(q, k_o))
        return out

    n_tile = seq // tq
    nxt = scores(0)
    for i in range(n_tile):
        lo, hi = i * tq, (i + 1) * tq
        cur = nxt
        if i + 1 < n_tile:
            nxt = scores(i + 1)
        values = [v_ref[0, lo:hi, :]] + ([v_ref[0, :lo, :]] if i > 0 else [])
        o_ref[0, lo:hi, :] = _softmax_pv(cur, values).astype(BF16)


def _mla_mixer(x, mod, g1, w_dkv, g_q, g_kv, w_uq, w_ukv):
    bsz, seq, d = x.shape
    hds = MLA_HEADS
    half = MLA_ROPE // 2
    swap = jnp.concatenate([jnp.arange(half, MLA_ROPE), jnp.arange(half)])
    r0 = MLA_Q_LORA + MLA_KV_LORA
    kr = w_dkv[:, r0:]
    wdkv = jnp.concatenate([w_dkv[:, :r0], kr, kr, kr[:, swap], kr[:, swap]], axis=1).astype(BF16)
    wq = w_uq.reshape(MLA_Q_LORA, hds, MLA_QK)
    wq_r = wq[:, :, MLA_NOPE:]
    wuq = jnp.concatenate([wq[:, :, :MLA_NOPE].reshape(MLA_Q_LORA, -1),
                           wq_r.reshape(MLA_Q_LORA, -1),
                           wq_r[:, :, swap].reshape(MLA_Q_LORA, -1)], axis=1).astype(BF16)
    wkv = w_ukv.reshape(MLA_KV_LORA, hds, MLA_NOPE + MLA_V)
    wukv = jnp.concatenate([wkv[:, :, :MLA_NOPE].reshape(MLA_KV_LORA, -1),
                            wkv[:, :, MLA_NOPE:].reshape(MLA_KV_LORA, -1)], axis=1).astype(BF16)
    cos, sin = _rope_tables(seq, hds)
    nq, nr = hds * MLA_NOPE, hds * MLA_ROPE
    tm = 512
    row = lambda i, b: (b, i, 0)
    qn, qr, kn, kr2, v = pl.pallas_call(
        _mla_proj_kernel,
        out_shape=[jax.ShapeDtypeStruct((bsz, seq, nq), BF16),
                   jax.ShapeDtypeStruct((bsz, seq, nr), BF16),
                   jax.ShapeDtypeStruct((bsz, seq, nq), BF16),
                   jax.ShapeDtypeStruct((bsz, seq, LANE), BF16),
                   jax.ShapeDtypeStruct((bsz, seq, nq), BF16)],
        grid=(seq // tm, bsz),
        in_specs=[pl.BlockSpec((1, tm, d), row),
                  pl.BlockSpec((1, N_MOD, d), lambda i, b: (b, 0, 0)),
                  _const_spec((1, d)),
                  _const_spec(wdkv.shape),
                  _const_spec((1, MLA_Q_LORA)),
                  _const_spec((1, MLA_KV_LORA)),
                  _const_spec(wuq.shape),
                  _const_spec(wukv.shape),
                  pl.BlockSpec((tm, nr), lambda i, b: (i, 0)),
                  pl.BlockSpec((tm, nr), lambda i, b: (i, 0))],
        out_specs=[pl.BlockSpec((1, tm, nq), row),
                   pl.BlockSpec((1, tm, nr), row),
                   pl.BlockSpec((1, tm, nq), row),
                   pl.BlockSpec((1, tm, LANE), row),
                   pl.BlockSpec((1, tm, nq), row)],
        compiler_params=_cparams(2),
        name="mla_proj",
    )(x, mod, g1.reshape(1, d), wdkv, g_q.reshape(1, -1), g_kv.reshape(1, -1), wuq, wukv, cos, sin)

    head = lambda b, h: (b, 0, h)
    return pl.pallas_call(
        _mla_attn_kernel,
        out_shape=jax.ShapeDtypeStruct((bsz, seq, hds * MLA_V), BF16),
        grid=(bsz, hds),
        in_specs=[pl.BlockSpec((1, seq, LANE), head),
                  pl.BlockSpec((1, seq, LANE), lambda b, h: (b, 0, h // 2)),
                  pl.BlockSpec((1, seq, LANE), head),
                  pl.BlockSpec((1, seq, LANE), lambda b, h: (b, 0, 0)),
                  pl.BlockSpec((1, seq, LANE), head)],
        out_specs=pl.BlockSpec((1, seq, LANE), head),
        compiler_params=_cparams(2),
        name="mla_attn",
    )(qn, qr, kn, kr2, v)


def _nsa_proj_kernel(x_ref, mod_ref, g1_ref, w_ref, q_ref, kc_ref, vc_ref, ks_ref, vs_ref,
                     kw_ref, vw_ref, gate_ref):
    nq = NSA_HEADS * NSA_DK
    nkv = NSA_GROUPS * NSA_DK
    mod = mod_ref[0]
    h = _norm_mod(x_ref[0], g1_ref[...], mod[0:1], mod[1:2]).astype(BF16)
    p = _dot(h, w_ref[...])
    q_ref[0] = (p[:, :nq] * (NSA_DK ** -0.5 * LOG2E)).astype(BF16)
    for n, ref in enumerate((kc_ref, vc_ref, ks_ref, vs_ref, kw_ref, vw_ref)):
        ref[0] = p[:, nq + n * nkv:nq + (n + 1) * nkv].astype(BF16)
    gate_ref[0] = _sigmoid(p[:, nq + 6 * nkv:])


def _nsa_compress_kernel(fk_ref, fv_ref, pek_ref, pev_ref, w1k_ref, w1v_ref, w2k_ref, w2v_ref,
                         ok_ref, ov_ref):
    for f_ref, pe_ref, w1_ref, w2_ref, o_ref in ((fk_ref, pek_ref, w1k_ref, w2k_ref, ok_ref),
                                                 (fv_ref, pev_ref, w1v_ref, w2v_ref, ov_ref)):
        flat = (f_ref[...].astype(F32) + pe_ref[...]).astype(BF16)
        hid = _gelu_tanh(_dot(flat, w1_ref[...])).astype(BF16)
        o_ref[...] = _dot(hid, w2_ref[...]).astype(BF16)


def _nsa_select_kernel(rb_ref, q_ref, kc_ref, vc_ref, bucket_ref, ovl_ref, bonus_ref,
                       ocmp_ref, sel_ref, bias_ref):
    g = pl.program_id(0)
    seq = q_ref.shape[1]
    n_blk = bonus_ref.shape[0]
    bucket = bucket_ref[...]

    @pl.when(pl.program_id(1) == 0)
    def _():
        for hp in range(NSA_HPG):
            acc = jnp.zeros(bucket.shape, F32)
            for r in range(REL_BUCKETS):
                acc = jnp.where(bucket == r, rb_ref[g * NSA_HPG + hp, r] * LOG2E, acc)
            bias_ref[hp] = acc

    valid = bucket >= 0
    kc = kc_ref[0, 0]
    vc = vc_ref[0, 0]
    psum = jnp.zeros(bucket.shape, F32)
    for hp in range(NSA_HPG):
        s = _dot_nt(q_ref[0, :, hp * NSA_DK:(hp + 1) * NSA_DK], kc) + bias_ref[hp]
        s = jnp.where(valid, s, NEG)
        m = jnp.max(s, axis=1, keepdims=True)
        m = jnp.where(m > 0.5 * NEG, m, 0.0)
        p = jnp.exp2(s - m)
        p = p / jnp.maximum(jnp.sum(p, axis=1, keepdims=True), 1e-30)
        ocmp_ref[0, :, hp * NSA_DV:(hp + 1) * NSA_DV] = _dot(p.astype(BF16), vc).astype(BF16)
        psum = psum + p

    imp = lax.dot_general(ovl_ref[...], psum, (((1,), (1,)), ((), ())),
                          preferred_element_type=F32, precision=lax.Precision.HIGHEST)
    bonus = bonus_ref[...]
    score = jnp.where(bonus > 0.5 * NEG, imp + bonus, NEG)
    blk = lax.broadcasted_iota(jnp.int32, (n_blk, seq), 0)
    rank = jnp.zeros((n_blk, seq), F32)
    for i in range(n_blk):
        row = score[i:i + 1, :]
        rank = rank + jnp.where(blk > i, jnp.where(row >= score, 1.0, 0.0),
                                jnp.where(row > score, 1.0, 0.0))
    selneg = jnp.where(rank < float(SLC_TOPN), jnp.where(score > 0.5 * NEG, 0.0, NEG), NEG)
    pad = jnp.zeros((LANE - n_blk, seq), F32)
    sel_ref[0, 0] = jnp.concatenate([selneg, pad], axis=0).T.astype(BF16)


def _nsa_attn_kernel(rb_ref, q_ref, sel_ref, ks_ref, vs_ref, kw_ref, vw_ref, et_ref, ocmp_ref,
                     gate_ref, bkt_ref, o_ref, kaug_ref, bias_ref):
    g = pl.program_id(0)
    t = ATT_TILE
    hp_n = NSA_HPG
    seq = q_ref.shape[1]

    @pl.when(pl.program_id(1) == 0)
    def _():
        kaug_ref[:, NSA_DK:] = et_ref[...]
        for kind in range(2):
            bkt = bkt_ref[kind]
            for hp in range(hp_n):
                hd = g * hp_n + hp
                acc = jnp.zeros((t, t), F32)
                for r in range(REL_BUCKETS - 1):
                    acc = jnp.where(
                        bkt == r, (rb_ref[hd, r] - rb_ref[hd, REL_BUCKETS - 1]) * LOG2E, acc)
                bias_ref[hp * t:(hp + 1) * t, (1 - kind) * t:(2 - kind) * t] = acc
        bias_ref[:, 2 * t:] = jnp.zeros((hp_n * t, t), F32)

    kaug_ref[:, :NSA_DK] = ks_ref[0]

    rows = hp_n * t
    r2 = lax.broadcasted_iota(jnp.int32, (rows, 2 * t), 0) & (t - 1)
    c2 = lax.broadcasted_iota(jnp.int32, (rows, 2 * t), 1)
    r1 = lax.broadcasted_iota(jnp.int32, (rows, t), 0) & (t - 1)
    c1 = lax.broadcasted_iota(jnp.int32, (rows, t), 1)
    near_ok = c2 <= r2 + t
    first_ok = c2 <= r2
    edge_ok = c1 > r1
    lane = lax.broadcasted_iota(jnp.int32, (t, LANE), 1)

    def near(i):
        if i == 0:
            return 0, 2 * t, bias_ref[:, t:], first_ok
        return (i - 1) * t, (i + 1) * t, bias_ref[:, :2 * t], near_ok

    def scores_sel(i):
        lo, hi = i * t, (i + 1) * t
        n_lo, n_hi, bias, ok = near(i)
        sel = sel_ref[0, 0, lo:hi, :]
        q_sel = jnp.concatenate(
            [jnp.concatenate([q_ref[0, lo:hi, hp * NSA_DK:(hp + 1) * NSA_DK], sel], axis=1)
             for hp in range(hp_n)], axis=0)
        out = [jnp.where(ok, _dot_nt(q_sel, kaug_ref[n_lo:n_hi, :]) + bias, NEG)]
        if n_lo > 0:
            out.append(_dot_nt(q_sel, kaug_ref[:n_lo, :]))
        return out

    def scores_win(i):
        lo, hi = i * t, (i + 1) * t
        n_lo, n_hi, bias, ok = near(i)
        q_win = jnp.concatenate(
            [q_ref[0, lo:hi, hp * NSA_DK:(hp + 1) * NSA_DK] for hp in range(hp_n)], axis=0)
        out = [jnp.where(ok, _dot_nt(q_win, kw_ref[0, n_lo:n_hi, :]) + bias, NEG)]
        w_lo = max(lo - WINDOW, 0)
        if n_lo > w_lo:
            s_m = _dot_nt(q_win, kw_ref[0, w_lo:n_lo, :])
            if lo - WINDOW >= 0:
                s_m = jnp.concatenate([jnp.where(edge_ok, s_m[:, :t], NEG), s_m[:, t:]], axis=1)
            out.append(s_m)
        return out

    def pieces(v_ref, i, far_lo):
        n_lo, n_hi, _, _ = near(i)
        return [v_ref[0, n_lo:n_hi, :]] + ([v_ref[0, far_lo:n_lo, :]] if n_lo > far_lo else [])

    n_tile = seq // t
    nxt_sel = scores_sel(0)
    for i in range(n_tile):
        lo, hi = i * t, (i + 1) * t
        cur_sel = nxt_sel
        cur_win = scores_win(i)
        o_slc = _softmax_pv(cur_sel, pieces(vs_ref, i, 0))
        if i + 1 < n_tile:
            nxt_sel = scores_sel(i + 1)
        o_win = _softmax_pv(cur_win, pieces(vw_ref, i, max(lo - WINDOW, 0)))

        gate = gate_ref[0, lo:hi, :]
        for hp in range(hp_n):
            hd = g * hp_n + hp

            def gcol(branch):
                return jnp.sum(jnp.where(lane == branch * NSA_HEADS + hd, gate, 0.0),
                               axis=1, keepdims=True)
            sl = slice(hp * t, (hp + 1) * t)
            o = (gcol(0) * ocmp_ref[0, lo:hi, hp * NSA_DV:(hp + 1) * NSA_DV].astype(F32)
                 + gcol(1) * o_slc[sl] + gcol(2) * o_win[sl])
            o_ref[0, lo:hi, hp * NSA_DV:(hp + 1) * NSA_DV] = o.astype(BF16)


def _nsa_mixer(x, mod, g1, w_in, pe_k, w1_k, w2_k, pe_v, w1_v, w2_v, rel_bias):
    bsz, seq, d = x.shape
    grp, hpg = NSA_GROUPS, NSA_HPG
    nq, nkv = NSA_HEADS * NSA_DK, NSA_GROUPS * NSA_DK
    n_gate = 3 * NSA_HEADS
    w = jnp.pad(w_in, ((0, 0), (0, LANE - n_gate))).astype(BF16)
    tm = 512
    row = lambda i, b: (b, i, 0)
    kv_shape = jax.ShapeDtypeStruct((bsz, seq, nkv), BF16)
    q, kc, vc, ks, vs, kw, vw, gates = pl.pallas_call(
        _nsa_proj_kernel,
        out_shape=[jax.ShapeDtypeStruct((bsz, seq, nq), BF16)] + [kv_shape] * 6
                  + [jax.ShapeDtypeStruct((bsz, seq, LANE), F32)],
        grid=(seq // tm, bsz),
        in_specs=[pl.BlockSpec((1, tm, d), row),
                  pl.BlockSpec((1, N_MOD, d), lambda i, b: (b, 0, 0)),
                  _const_spec((1, d)),
                  _const_spec(w.shape)],
        out_specs=[pl.BlockSpec((1, tm, nq), row)] + [pl.BlockSpec((1, tm, nkv), row)] * 6
                  + [pl.BlockSpec((1, tm, LANE), row)],
        compiler_params=_cparams(2),
        name="nsa_proj",
    )(x, mod, g1.reshape(1, d), w)

    n_chunk = seq // CMP_STRIDE
    n_cmp = n_chunk - CMP_RATIO + 1

    def blocks(t):
        ch = t.reshape(bsz, n_chunk, CMP_STRIDE, grp, NSA_DK).transpose(0, 3, 1, 2, 4)
        ch = ch.reshape(bsz, grp, n_chunk, CMP_STRIDE * NSA_DK)
        ch = jnp.pad(ch, ((0, 0), (0, 0), (0, CMP_RATIO - 1), (0, 0)))
        flat = jnp.concatenate([ch[:, :, r:r + n_chunk] for r in range(CMP_RATIO)], axis=3)
        return flat.reshape(bsz * grp * n_chunk, CMP_BLOCK * NSA_DK)

    fk, fv = blocks(kc), blocks(vc)
    tr = min(1024, bsz * grp * n_chunk)
    wide = CMP_BLOCK * NSA_DK
    kcmp, vcmp = pl.pallas_call(
        _nsa_compress_kernel,
        out_shape=[jax.ShapeDtypeStruct((bsz * grp * n_chunk, NSA_DK), BF16)] * 2,
        grid=(bsz * grp * n_chunk // tr,),
        in_specs=[pl.BlockSpec((tr, wide), lambda i: (i, 0)),
                  pl.BlockSpec((tr, wide), lambda i: (i, 0)),
                  _const_spec((1, wide)), _const_spec((1, wide)),
                  _const_spec((wide, CMP_HIDDEN)), _const_spec((wide, CMP_HIDDEN)),
                  _const_spec((CMP_HIDDEN, NSA_DK)), _const_spec((CMP_HIDDEN, NSA_DV))],
        out_specs=[pl.BlockSpec((tr, NSA_DK), lambda i: (i, 0))] * 2,
        compiler_params=_cparams(1),
        name="nsa_compress",
    )(fk, fv, pe_k.reshape(1, wide), pe_v.reshape(1, wide), w1_k.astype(BF16), w1_v.astype(BF16),
      w2_k.astype(BF16), w2_v.astype(BF16))
    kcmp = kcmp.reshape(bsz, grp, n_chunk, NSA_DK)
    vcmp = vcmp.reshape(bsz, grp, n_chunk, NSA_DV)

    pos = jnp.arange(seq)
    cmp_start = jnp.arange(n_chunk) * CMP_STRIDE
    dist_c = pos[:, None] - (cmp_start + CMP_BLOCK - 1)[None, :]
    ok_c = (dist_c >= 0) & (jnp.arange(n_chunk) < n_cmp)[None, :]
    bucket_c = jnp.where(ok_c, _t5_bucket(dist_c), -1).astype(jnp.int32)
    n_blk = seq // SLC_BLOCK
    blk = jnp.arange(n_blk)
    blk_start = blk * SLC_BLOCK
    overlap_t = ((cmp_start[None, :] < blk_start[:, None] + SLC_BLOCK)
                 & (cmp_start[None, :] + CMP_BLOCK > blk_start[:, None])
                 & (jnp.arange(n_chunk) < n_cmp)[None, :]).astype(F32)
    back = (pos // SLC_BLOCK)[None, :] - blk[:, None]
    forced = (blk[:, None] == 0) | ((back >= 0) & (back < N_LOCAL))
    causal_blk = blk_start[:, None] <= pos[None, :]
    bonus_t = jnp.where(causal_blk, jnp.where(forced, FORCE_BONUS, 0.0), NEG).astype(F32)

    ocmp, selneg = pl.pallas_call(
        _nsa_select_kernel,
        out_shape=[jax.ShapeDtypeStruct((bsz, seq, nq), BF16),
                   jax.ShapeDtypeStruct((bsz, grp, seq, LANE), BF16)],
        grid=(grp, bsz),
        in_specs=[pl.BlockSpec(memory_space=pltpu.SMEM),
                  pl.BlockSpec((1, seq, hpg * NSA_DK), lambda g, b: (b, 0, g)),
                  pl.BlockSpec((1, 1, n_chunk, NSA_DK), lambda g, b: (b, g, 0, 0)),
                  pl.BlockSpec((1, 1, n_chunk, NSA_DV), lambda g, b: (b, g, 0, 0)),
                  _const_spec(bucket_c.shape),
                  _const_spec(overlap_t.shape),
                  _const_spec(bonus_t.shape)],
        out_specs=[pl.BlockSpec((1, seq, hpg * NSA_DV), lambda g, b: (b, 0, g)),
                   pl.BlockSpec((1, 1, seq, LANE), lambda g, b: (b, g, 0, 0))],
        scratch_shapes=[pltpu.VMEM((hpg, seq, n_chunk), F32)],
        compiler_params=_cparams(2),
        name="nsa_select",
    )(rel_bias, q, kcmp, vcmp, bucket_c, overlap_t, bonus_t)

    t = ATT_TILE
    rr = jnp.arange(t)
    near = jnp.stack([_t5_bucket(k * t + rr[:, None] - rr[None, :]) for k in range(2)]).astype(jnp.int32)
    e_t = (jnp.arange(seq)[:, None] // SLC_BLOCK == jnp.arange(LANE)[None, :]).astype(BF16)
    grp_blk = lambda g, b: (b, 0, g)
    return pl.pallas_call(
        _nsa_attn_kernel,
        out_shape=jax.ShapeDtypeStruct((bsz, seq, nq), BF16),
        grid=(grp, bsz),
        in_specs=[pl.BlockSpec(memory_space=pltpu.SMEM),
                  pl.BlockSpec((1, seq, hpg * NSA_DK), grp_blk),
                  pl.BlockSpec((1, 1, seq, LANE), lambda g, b: (b, g, 0, 0)),
                  pl.BlockSpec((1, seq, NSA_DK), grp_blk),
                  pl.BlockSpec((1, seq, NSA_DV), grp_blk),
                  pl.BlockSpec((1, seq, NSA_DK), grp_blk),
                  pl.BlockSpec((1, seq, NSA_DV), grp_blk),
                  _const_spec(e_t.shape),
                  pl.BlockSpec((1, seq, hpg * NSA_DV), grp_blk),
                  pl.BlockSpec((1, seq, LANE), lambda g, b: (b, 0, 0)),
                  _const_spec(near.shape)],
        out_specs=pl.BlockSpec((1, seq, hpg * NSA_DV), grp_blk),
        scratch_shapes=[pltpu.VMEM((seq, NSA_DK + LANE), BF16),
                        pltpu.VMEM((hpg * t, 3 * t), F32)],
        compiler_params=_cparams(2),
        name="nsa_attn",
    )(rel_bias, q, selneg, ks, vs, kw, vw, e_t, ocmp, gates, near)


def _s5_kernel(x_ref, mod_ref, g1_ref, perm_ref, permt_ref, bm_ref, cm_ref, are_ref, aim_ref,
               dskip_ref, y_ref, bu_ref, state_ref):
    bb, ts, d = x_ref.shape
    rows = bb * ts
    n_slab = d // LANE
    wid = S5_SLAB * S5_STATE

    @pl.when(pl.program_id(1) == 0)
    def _():
        state_ref[...] = jnp.zeros_like(state_ref)

    mod = mod_ref[...]
    u = (_rms(x_ref[...], g1_ref[...]) * (1.0 + mod[:, 1:2, :]) + mod[:, 0:1, :]).reshape(rows, d)
    u_tb = _dot(perm_ref[...], u.astype(BF16)).astype(BF16)
    def project_in(k):
        bu_ref[:, 2 * k * wid:2 * (k + 1) * wid] = _dot(u_tb[:, k * LANE:(k + 1) * LANE], bm_ref[k])

    def scan(k):
        c_re, c_im = 2 * k * wid, (2 * k + 1) * wid
        a_re = jnp.broadcast_to(are_ref[k], (bb, wid))
        a_im = jnp.broadcast_to(aim_ref[k], (bb, wid))
        xr, xi = state_ref[:, c_re:c_re + wid], state_ref[:, c_im:c_im + wid]
        for t in range(ts):
            r0 = t * bb
            nr = a_re * xr - a_im * xi + bu_ref[r0:r0 + bb, c_re:c_re + wid]
            ni = a_re * xi + a_im * xr + bu_ref[r0:r0 + bb, c_im:c_im + wid]
            bu_ref[r0:r0 + bb, c_re:c_re + wid] = nr
            bu_ref[r0:r0 + bb, c_im:c_im + wid] = ni
            xr, xi = nr, ni
        state_ref[:, c_re:c_re + wid] = xr
        state_ref[:, c_im:c_im + wid] = xi

    def project_out(k):
        return _dot(bu_ref[:, 2 * k * wid:2 * (k + 1) * wid].astype(BF16), cm_ref[k]).astype(BF16)

    ys = []
    project_in(0)
    for k in range(n_slab):
        if k + 1 < n_slab:
            project_in(k + 1)
        scan(k)
        ys.append(project_out(k))
    y_tb = jnp.concatenate(ys, axis=1)
    y = _dot(permt_ref[...], y_tb) + dskip_ref[...] * u
    y_ref[...] = _gelu_tanh(y).astype(BF16).reshape(bb, ts, d)


def _s5_mixer(x, mod, g1, lam_re, lam_im, log_dt, b_re, b_im, c_re, c_im, d_skip):
    bsz, seq, d = x.shape
    n_slab = d // LANE
    dt = jnp.exp(log_dt)[:, None]
    mag = jnp.exp(lam_re * dt)
    lb_re = mag * jnp.cos(lam_im * dt)
    lb_im = mag * jnp.sin(lam_im * dt)
    den = lam_re * lam_re + lam_im * lam_im
    f_re = ((lb_re - 1.0) * lam_re + lb_im * lam_im) / den
    f_im = (lb_im * lam_re - (lb_re - 1.0) * lam_im) / den
    bb_re = f_re[..., None] * b_re - f_im[..., None] * b_im
    bb_im = f_re[..., None] * b_im + f_im[..., None] * b_re
    eye = jnp.eye(S5_SLAB, dtype=F32)
    wid = S5_SLAB * S5_STATE

    def in_mat(b):
        b = b.reshape(n_slab, S5_SLAB, S5_STATE, S5_GROUP)
        return jnp.einsum('kgpi,gh->kgihp', b, eye).reshape(n_slab, LANE, wid)

    def out_mat(c):
        c = c.reshape(n_slab, S5_SLAB, S5_GROUP, S5_STATE)
        return jnp.einsum('kgip,gh->kgphi', c, eye).reshape(n_slab, wid, LANE)

    bm = jnp.concatenate([in_mat(bb_re), in_mat(bb_im)], axis=2).astype(BF16)
    cm = jnp.concatenate([out_mat(c_re), -out_mat(c_im)], axis=1).astype(BF16)
    a_re = lb_re.reshape(n_slab, 1, wid)
    a_im = lb_im.reshape(n_slab, 1, wid)

    bb, ts = S5_BB, 64
    rows = bb * ts
    r = jnp.arange(rows)
    perm = ((r % bb)[:, None] * ts + (r // bb)[:, None] == r[None, :]).astype(BF16)
    blk = lambda i, t: (i, t, 0)
    return pl.pallas_call(
        _s5_kernel,
        out_shape=jax.ShapeDtypeStruct((bsz, seq, d), BF16),
        grid=(bsz // bb, seq // ts),
        in_specs=[pl.BlockSpec((bb, ts, d), blk),
                  pl.BlockSpec((bb, N_MOD, d), lambda i, t: (i, 0, 0)),
                  _const_spec((1, d)),
                  _const_spec((rows, rows)), _const_spec((rows, rows)),
                  _const_spec(bm.shape), _const_spec(cm.shape),
                  _const_spec(a_re.shape), _const_spec(a_im.shape),
                  _const_spec((1, d))],
        out_specs=pl.BlockSpec((bb, ts, d), blk),
        scratch_shapes=[pltpu.VMEM((rows, 2 * wid * n_slab), F32),
                        pltpu.VMEM((bb, 2 * wid * n_slab), F32)],
        compiler_params=_cparams(2),
        name="s5_scan",
    )(x, mod, g1.reshape(1, d), perm, perm.T, bm, cm, a_re, a_im, d_skip.reshape(1, d))


def kernel(x, c, ada_w, ada_b, norm1_g, norm2_g, final_g, rel_bias, mla_w_dkv, mla_g_q, mla_g_kv, mla_w_uq, mla_w_ukv, mla_w_o, nsa_w_in, nsa_pe_k, nsa_w1_k, nsa_w2_k, nsa_pe_v, nsa_w1_v, nsa_w2_v, nsa_w_o, s5_lam_re, s5_lam_im, s5_log_dt, s5_b_re, s5_b_im, s5_c_re, s5_c_im, s5_d, s5_w_glu, ffn_w_up, ffn_conv_w, ffn_conv_b, ffn_w_down):
    depth = ada_w.shape[0]
    mods = _ada_mod(c, ada_w, ada_b)
    for i in range(depth):
        mod = mods[i]
        kind, j = i % N_MIXERS, i // N_MIXERS
        if kind == 0:
            o = _mla_mixer(x, mod, norm1_g[i], mla_w_dkv[j], mla_g_q[j], mla_g_kv[j],
                           mla_w_uq[j], mla_w_ukv[j])
            w_post, glu = mla_w_o[j], False
        elif kind == 1:
            o = _nsa_mixer(x, mod, norm1_g[i], nsa_w_in[j], nsa_pe_k[j], nsa_w1_k[j], nsa_w2_k[j],
                           nsa_pe_v[j], nsa_w1_v[j], nsa_w2_v[j], rel_bias)
            w_post, glu = nsa_w_o[j], False
        else:
            o = _s5_mixer(x, mod, norm1_g[i], s5_lam_re[j], s5_lam_im[j], s5_log_dt[j],
                          s5_b_re[j], s5_b_im[j], s5_c_re[j], s5_c_im[j], s5_d[j])
            w_post, glu = s5_w_glu[j], True
        x = _post_ffn(x, o, mod, w_post.astype(BF16), norm2_g[i], ffn_w_up[i].astype(BF16),
                      ffn_conv_w[i], ffn_conv_b[i], ffn_w_down[i].astype(BF16), final_g,
                      glu=glu, final=(i == depth - 1))
    return x
```

```python
import functools
import math

import jax
import jax.numpy as jnp
from jax import lax
from jax.experimental import pallas as pl
from jax.experimental.pallas import tpu as pltpu

F32 = jnp.float32
BF16 = jnp.bfloat16

NORM_EPS = 1e-6
N_MOD = 6
N_MIXERS = 3

MLA_HEADS = 8
MLA_Q_LORA = 512
MLA_KV_LORA = 256
MLA_NOPE = 128
MLA_ROPE = 64
MLA_V = 128
MLA_QK = MLA_NOPE + MLA_ROPE
ROPE_THETA = 10000.0

NSA_HEADS = 8
NSA_GROUPS = 2
NSA_HPG = NSA_HEADS // NSA_GROUPS
NSA_DK = 128
NSA_DV = 128
CMP_BLOCK = 32
CMP_STRIDE = 16
CMP_RATIO = CMP_BLOCK // CMP_STRIDE
CMP_HIDDEN = 256
SLC_BLOCK = 64
SLC_TOPN = 16
N_LOCAL = 2
FORCE_BONUS = 1e4
WINDOW = 512
REL_BUCKETS = 32
REL_MAX_DIST = 128

S5_GROUP = 16
S5_STATE = 64
CONV_WIDTH = 3

LANE = 128
NEG = -1e30
LOG2E = math.log2(math.e)
VMEM_LIMIT = 56 * 1024 * 1024

ATT_TILE = 128
MLA_TQ = 256
S5_BB = 8
S5_SLAB = 8


def _cparams(n_axes):
    return pltpu.CompilerParams(
        dimension_semantics=("arbitrary",) * n_axes,
        vmem_limit_bytes=VMEM_LIMIT)


def _const_spec(shape):
    zeros = (0,) * len(shape)
    return pl.BlockSpec(shape, lambda *_: zeros, pipeline_mode=pl.Buffered(1))


def _dot(a, b):
    return jnp.dot(a, b, preferred_element_type=F32)


def _dot_nt(a, b):
    return lax.dot_general(a, b, (((1,), (1,)), ((), ())), preferred_element_type=F32)


def _rms(x, g):
    return x * lax.rsqrt(jnp.mean(x * x, axis=-1, keepdims=True) + NORM_EPS) * g


def _norm_mod(x, g, shift, scale):
    return _rms(x, g) * (1.0 + scale) + shift


def _gelu_tanh(x):
    c = math.sqrt(2.0 / math.pi)
    return 0.5 * x * (1.0 + jnp.tanh(c * (x + 0.044715 * (x * x * x))))


def _sigmoid(x):
    return 1.0 / (1.0 + jnp.exp(-x))


def _softmax_pv(scores, values):
    m = functools.reduce(jnp.maximum, [jnp.max(s, axis=1, keepdims=True) for s in scores])
    ps = [jnp.exp2(s - m) for s in scores]
    l = functools.reduce(jnp.add, [jnp.sum(p, axis=1, keepdims=True) for p in ps])
    o = functools.reduce(jnp.add, [_dot(p.astype(BF16), v) for p, v in zip(ps, values)])
    return o / l


def _t5_bucket(dist):
    n = jnp.maximum(dist, 0)
    max_exact = REL_BUCKETS // 2
    nf = jnp.maximum(n, 1).astype(F32)
    large = max_exact + (jnp.log(nf / max_exact) / math.log(REL_MAX_DIST / max_exact)
                         * (REL_BUCKETS - max_exact)).astype(jnp.int32)
    large = jnp.minimum(large, REL_BUCKETS - 1)
    return jnp.where(n < max_exact, n, large)


def _ada_kernel(c_ref, w_ref, b_ref, o_ref):
    c = c_ref[...]
    c_act = c * _sigmoid(c)
    o_ref[0] = jnp.dot(c_act, w_ref[0], preferred_element_type=F32,
                       precision=lax.Precision.HIGHEST) + b_ref[0]


def _ada_mod(c, ada_w, ada_b):
    depth, d, n = ada_w.shape
    bsz = c.shape[0]
    tn = 1024
    out = pl.pallas_call(
        _ada_kernel,
        out_shape=jax.ShapeDtypeStruct((depth, bsz, n), F32),
        grid=(depth, n // tn),
        in_specs=[pl.BlockSpec((bsz, d), lambda i, j: (0, 0)),
                  pl.BlockSpec((1, d, tn), lambda i, j: (i, 0, j)),
                  pl.BlockSpec((1, 1, tn), lambda i, j: (i, 0, j))],
        out_specs=pl.BlockSpec((1, bsz, tn), lambda i, j: (i, 0, j)),
        compiler_params=_cparams(2),
        name="ada_mod",
    )(c, ada_w, ada_b.reshape(depth, 1, n))
    return out.reshape(depth, bsz, N_MOD, d)


def _post_ffn_kernel(x_ref, o_ref, mod_ref, wpost_ref, g2_ref, wup_ref, cw_ref, cb_ref,
                     wdown_ref, gf_ref, out_ref, act_ref, carry_ref,
                     *, glu, final, fc):
    tm = x_ref.shape[1]
    d_ff = wdown_ref.shape[0]

    @pl.when(pl.program_id(1) == 0)
    def _():
        carry_ref[...] = jnp.zeros_like(carry_ref)

    x = x_ref[0]
    mod = mod_ref[0]
    y = _dot(o_ref[0], wpost_ref[...])
    if glu:
        d = x.shape[-1]
        y = y[:, :d] * _sigmoid(y[:, d:])
    x1 = x + mod[2:3] * y
    h = _norm_mod(x1, g2_ref[...], mod[3:4], mod[4:5]).astype(BF16)

    row = lax.broadcasted_iota(jnp.int32, (8, fc), 0)

    def conv(z, col):
        c = carry_ref[:, col:col + fc]
        carry_ref[:, col:col + fc] = z[tm - 8:, :]
        r1 = pltpu.roll(z, 1, 0)
        r2 = pltpu.roll(z, 2, 0)
        s1 = jnp.concatenate([jnp.where(row < 1, pltpu.roll(c, 1, 0), r1[:8]), r1[8:]], axis=0)
        s2 = jnp.concatenate([jnp.where(row < 2, pltpu.roll(c, 2, 0), r2[:8]), r2[8:]], axis=0)
        w = cw_ref[:, col:col + fc]
        return w[0:1] * s2 + w[1:2] * s1 + w[2:3] * z + cb_ref[:, col:col + fc]

    for f in range(0, d_ff, fc):
        val = conv(_dot(h, wup_ref[:, f:f + fc]), f)
        gate = conv(_dot(h, wup_ref[:, d_ff + f:d_ff + f + fc]), d_ff + f)
        act_ref[:, f:f + fc] = (val * (gate * _sigmoid(gate))).astype(BF16)

    x2 = x1 + mod[5:6] * _dot(act_ref[...], wdown_ref[...])
    if final:
        x2 = _rms(x2, gf_ref[...])
    out_ref[0] = x2


def _post_ffn(x, o, mod, w_post, g2, w_up, conv_w, conv_b, w_down, final_g, *, glu, final):
    bsz, seq, d = x.shape
    d_ff = w_down.shape[0]
    tm = 512
    fc = 256
    row = lambda b, i: (b, i, 0)
    return pl.pallas_call(
        functools.partial(_post_ffn_kernel, glu=glu, final=final, fc=fc),
        out_shape=jax.ShapeDtypeStruct((bsz, seq, d), F32),
        grid=(bsz, seq // tm),
        in_specs=[pl.BlockSpec((1, tm, d), row),
                  pl.BlockSpec((1, tm, d), row),
                  pl.BlockSpec((1, N_MOD, d), lambda b, i: (b, 0, 0)),
                  _const_spec(w_post.shape),
                  _const_spec((1, d)),
                  _const_spec(w_up.shape),
                  _const_spec(conv_w.shape),
                  _const_spec((1, 2 * d_ff)),
                  _const_spec(w_down.shape),
                  _const_spec((1, d))],
        out_specs=pl.BlockSpec((1, tm, d), row),
        scratch_shapes=[pltpu.VMEM((tm, d_ff), BF16),
                        pltpu.VMEM((8, 2 * d_ff), F32)],
        compiler_params=_cparams(2),
        name="post_ffn",
    )(x, o, mod, w_post, g2.reshape(1, d), w_up, conv_w, conv_b.reshape(1, -1), w_down,
      final_g.reshape(1, d))


def _rope_tables(seq, reps):
    half = MLA_ROPE // 2
    inv = ROPE_THETA ** (-jnp.arange(half, dtype=F32) / half)
    ang = jnp.arange(seq).astype(F32)[:, None] * inv[None, :]
    cos, sin = jnp.cos(ang), jnp.sin(ang)
    cos64 = jnp.concatenate([cos, cos], axis=1)
    sin64 = jnp.concatenate([-sin, sin], axis=1)
    return jnp.tile(cos64, (1, reps)), jnp.tile(sin64, (1, reps))


def _mla_proj_kernel(x_ref, mod_ref, g1_ref, wdkv_ref, gq_ref, gkv_ref, wuq_ref, wukv_ref,
                     cos_ref, sin_ref, qn_ref, qr_ref, kn_ref, kr_ref, v_ref):
    nq = MLA_HEADS * MLA_NOPE
    nr = MLA_HEADS * MLA_ROPE
    scale = MLA_QK ** -0.5 * LOG2E
    mod = mod_ref[0]
    h = _norm_mod(x_ref[0], g1_ref[...], mod[0:1], mod[1:2]).astype(BF16)
    dkv = _dot(h, wdkv_ref[...])
    c_q = _rms(dkv[:, :MLA_Q_LORA], gq_ref[...]).astype(BF16)
    c_kv = _rms(dkv[:, MLA_Q_LORA:MLA_Q_LORA + MLA_KV_LORA], gkv_ref[...]).astype(BF16)
    cos = cos_ref[...]
    sin = sin_ref[...]
    r0 = MLA_Q_LORA + MLA_KV_LORA
    kr_ref[0] = (dkv[:, r0:r0 + LANE] * cos[:, :LANE]
                 + dkv[:, r0 + LANE:r0 + 2 * LANE] * sin[:, :LANE]).astype(BF16)
    q = _dot(c_q, wuq_ref[...])
    qn_ref[0] = (q[:, :nq] * scale).astype(BF16)
    qr_ref[0] = ((q[:, nq:nq + nr] * cos + q[:, nq + nr:] * sin) * scale).astype(BF16)
    kv = _dot(c_kv, wukv_ref[...])
    kn_ref[0] = kv[:, :nq].astype(BF16)
    v_ref[0] = kv[:, nq:].astype(BF16)


def _mla_attn_kernel(qn_ref, qr_ref, kn_ref, kr_ref, v_ref, o_ref):
    seq = qn_ref.shape[1]
    tq = MLA_TQ
    odd = pl.program_id(1) % 2
    lane = lax.broadcasted_iota(jnp.int32, (tq, LANE), 1)
    keep = (lane >= MLA_ROPE) == (odd == 1)
    row = lax.broadcasted_iota(jnp.int32, (tq, tq), 0)
    col = lax.broadcasted_iota(jnp.int32, (tq, tq), 1)
    def scores(i):
        lo, hi = i * tq, (i + 1) * tq
        qr = jnp.where(keep, qr_ref[0, lo:hi, :], jnp.zeros((), BF16))
        q = jnp.concatenate([qn_ref[0, lo:hi, :], qr], axis=1)
        k_d = jnp.concatenate([kn_ref[0, lo:hi, :], kr_ref[0, lo:hi, :]], axis=1)
        out = [jnp.where(col <= row, _dot_nt(q, k_d), NEG)]
        if i > 0:
            k_o = jnp.concatenate([kn_ref[0, :lo, :], kr_ref[0, :lo, :]], axis=1)
            out.append(_dot_nt(q, k_o))
        return out

    n_tile = seq // tq
    nxt = scores(0)
    for i in range(n_tile):
        lo, hi = i * tq, (i + 1) * tq
        cur = nxt
        if i + 1 < n_tile:
            nxt = scores(i + 1)
        values = [v_ref[0, lo:hi, :]] + ([v_ref[0, :lo, :]] if i > 0 else [])
        o_ref[0, lo:hi, :] = _softmax_pv(cur, values).astype(BF16)


def _mla_mixer(x, mod, g1, w_dkv, g_q, g_kv, w_uq, w_ukv):
    bsz, seq, d = x.shape
    hds = MLA_HEADS
    half = MLA_ROPE // 2
    swap = jnp.concatenate([jnp.arange(half, MLA_ROPE), jnp.arange(half)])
    r0 = MLA_Q_LORA + MLA_KV_LORA
    kr = w_dkv[:, r0:]
    wdkv = jnp.concatenate([w_dkv[:, :r0], kr, kr, kr[:, swap], kr[:, swap]], axis=1).astype(BF16)
    wq = w_uq.reshape(MLA_Q_LORA, hds, MLA_QK)
    wq_r = wq[:, :, MLA_NOPE:]
    wuq = jnp.concatenate([wq[:, :, :MLA_NOPE].reshape(MLA_Q_LORA, -1),
                           wq_r.reshape(MLA_Q_LORA, -1),
                           wq_r[:, :, swap].reshape(MLA_Q_LORA, -1)], axis=1).astype(BF16)
    wkv = w_ukv.reshape(MLA_KV_LORA, hds, MLA_NOPE + MLA_V)
    wukv = jnp.concatenate([wkv[:, :, :MLA_NOPE].reshape(MLA_KV_LORA, -1),
                            wkv[:, :, MLA_NOPE:].reshape(MLA_KV_LORA, -1)], axis=1).astype(BF16)
    cos, sin = _rope_tables(seq, hds)
    nq, nr = hds * MLA_NOPE, hds * MLA_ROPE
    tm = 512
    row = lambda i, b: (b, i, 0)
    qn, qr, kn, kr2, v = pl.pallas_call(
        _mla_proj_kernel,
        out_shape=[jax.ShapeDtypeStruct((bsz, seq, nq), BF16),
                   jax.ShapeDtypeStruct((bsz, seq, nr), BF16),
                   jax.ShapeDtypeStruct((bsz, seq, nq), BF16),
                   jax.ShapeDtypeStruct((bsz, seq, LANE), BF16),
                   jax.ShapeDtypeStruct((bsz, seq, nq), BF16)],
        grid=(seq // tm, bsz),
        in_specs=[pl.BlockSpec((1, tm, d), row),
                  pl.BlockSpec((1, N_MOD, d), lambda i, b: (b, 0, 0)),
                  _const_spec((1, d)),
                  _const_spec(wdkv.shape),
                  _const_spec((1, MLA_Q_LORA)),
                  _const_spec((1, MLA_KV_LORA)),
                  _const_spec(wuq.shape),
                  _const_spec(wukv.shape),
                  pl.BlockSpec((tm, nr), lambda i, b: (i, 0)),
                  pl.BlockSpec((tm, nr), lambda i, b: (i, 0))],
        out_specs=[pl.BlockSpec((1, tm, nq), row),
                   pl.BlockSpec((1, tm, nr), row),
                   pl.BlockSpec((1, tm, nq), row),
                   pl.BlockSpec((1, tm, LANE), row),
                   pl.BlockSpec((1, tm, nq), row)],
        compiler_params=_cparams(2),
        name="mla_proj",
    )(x, mod, g1.reshape(1, d), wdkv, g_q.reshape(1, -1), g_kv.reshape(1, -1), wuq, wukv, cos, sin)

    head = lambda b, h: (b, 0, h)
    return pl.pallas_call(
        _mla_attn_kernel,
        out_shape=jax.ShapeDtypeStruct((bsz, seq, hds * MLA_V), BF16),
        grid=(bsz, hds),
        in_specs=[pl.BlockSpec((1, seq, LANE), head),
                  pl.BlockSpec((1, seq, LANE), lambda b, h: (b, 0, h // 2)),
                  pl.BlockSpec((1, seq, LANE), head),
                  pl.BlockSpec((1, seq, LANE), lambda b, h: (b, 0, 0)),
                  pl.BlockSpec((1, seq, LANE), head)],
        out_specs=pl.BlockSpec((1, seq, LANE), head),
        compiler_params=_cparams(2),
        name="mla_attn",
    )(qn, qr, kn, kr2, v)


def _nsa_proj_kernel(x_ref, mod_ref, g1_ref, w_ref, q_ref, kc_ref, vc_ref, ks_ref, vs_ref,
                     kw_ref, vw_ref, gate_ref, cmp_ref):
    nq = NSA_HEADS * NSA_DK
    nkv = NSA_GROUPS * NSA_DK
    tm = x_ref.shape[1]
    mod = mod_ref[0]
    h = _norm_mod(x_ref[0], g1_ref[...], mod[0:1], mod[1:2]).astype(BF16)
    p = _dot(h, w_ref[...])
    q_ref[0] = (p[:, :nq] * (NSA_DK ** -0.5 * LOG2E)).astype(BF16)
    for n, ref in enumerate((ks_ref, vs_ref, kw_ref, vw_ref)):
        ref[0] = p[:, nq + (n + 2) * nkv:nq + (n + 3) * nkv].astype(BF16)
    gate_ref[0] = _sigmoid(p[:, nq + 6 * nkv:])
    for n, ref in enumerate((kc_ref, vc_ref)):
        for g in range(NSA_GROUPS):
            c0 = nq + n * nkv + g * NSA_DK
            stage = cmp_ref.at[n * NSA_GROUPS + g]
            stage[...] = p[:, c0:c0 + NSA_DK]
            for r in range(CMP_STRIDE):
                ref[0, g, r] = stage[pl.ds(r, tm // CMP_STRIDE, stride=CMP_STRIDE), :].astype(BF16)


def _nsa_compress_kernel(ck_ref, cv_ref, pek_ref, pev_ref, w1k_ref, w1v_ref, w2k_ref, w2v_ref,
                         ok_ref, ov_ref):
    nb, grp, _, n_chunk, dk = ck_ref.shape
    half = CMP_STRIDE * dk
    rows = nb * grp * n_chunk
    for c_ref, pe_ref, w1_ref, w2_ref, o_ref in ((ck_ref, pek_ref, w1k_ref, w2k_ref, ok_ref),
                                                 (cv_ref, pev_ref, w1v_ref, w2v_ref, ov_ref)):
        chunks = jnp.concatenate(
            [jnp.concatenate([c_ref[b, g, r] for r in range(CMP_STRIDE)], axis=1)
             for b in range(nb) for g in range(grp)], axis=0)
        first = _dot(chunks, w1_ref[:half, :])
        second = _dot(chunks, w1_ref[half:, :])
        pe = jnp.broadcast_to(pe_ref[...], (8, 2 * half)).astype(BF16)
        const = _dot(pe, w1_ref[...])[0:1]
        hid = _gelu_tanh(first + pltpu.roll(second, rows - 1, 0) + const).astype(BF16)
        o_ref[...] = _dot(hid, w2_ref[...]).astype(BF16).reshape(nb, grp, n_chunk, dk)


def _nsa_select_kernel(rb_ref, q_ref, kc_ref, vc_ref, bucket_ref, ovl_ref, bonus_ref,
                       ocmp_ref, sel_ref, bias_ref):
    g = pl.program_id(0)
    seq = q_ref.shape[1]
    n_blk = bonus_ref.shape[0]
    bucket = bucket_ref[...]

    @pl.when(pl.program_id(1) == 0)
    def _():
        for hp in range(NSA_HPG):
            acc = jnp.zeros(bucket.shape, F32)
            for r in range(REL_BUCKETS):
                acc = jnp.where(bucket == r, rb_ref[g * NSA_HPG + hp, r] * LOG2E, acc)
            bias_ref[hp] = acc

    valid = bucket >= 0
    kc = kc_ref[0, 0]
    vc = vc_ref[0, 0]
    psum = jnp.zeros(bucket.shape, F32)
    for hp in range(NSA_HPG):
        s = _dot_nt(q_ref[0, :, hp * NSA_DK:(hp + 1) * NSA_DK], kc) + bias_ref[hp]
        s = jnp.where(valid, s, NEG)
        m = jnp.max(s, axis=1, keepdims=True)
        m = jnp.where(m > 0.5 * NEG, m, 0.0)
        p = jnp.exp2(s - m)
        p = p / jnp.maximum(jnp.sum(p, axis=1, keepdims=True), 1e-30)
        ocmp_ref[0, :, hp * NSA_DV:(hp + 1) * NSA_DV] = _dot(p.astype(BF16), vc).astype(BF16)
        psum = psum + p

    imp = lax.dot_general(ovl_ref[...], psum, (((1,), (1,)), ((), ())),
                          preferred_element_type=F32, precision=lax.Precision.HIGHEST)
    bonus = bonus_ref[...]
    score = jnp.where(bonus > 0.5 * NEG, imp + bonus, NEG)
    blk = lax.broadcasted_iota(jnp.int32, (n_blk, seq), 0)
    rank = jnp.zeros((n_blk, seq), F32)
    for i in range(n_blk):
        row = score[i:i + 1, :]
        rank = rank + jnp.where(blk > i, jnp.where(row >= score, 1.0, 0.0),
                                jnp.where(row > score, 1.0, 0.0))
    selneg = jnp.where(rank < float(SLC_TOPN), jnp.where(score > 0.5 * NEG, 0.0, NEG), NEG)
    pad = jnp.zeros((LANE - n_blk, seq), F32)
    sel_ref[0, 0] = jnp.concatenate([selneg, pad], axis=0).T.astype(BF16)


def _nsa_attn_kernel(rb_ref, q_ref, sel_ref, ks_ref, vs_ref, kw_ref, vw_ref, et_ref, ocmp_ref,
                     gate_ref, bkt_ref, o_ref, kaug_ref, bias_ref):
    g = pl.program_id(0)
    t = ATT_TILE
    hp_n = NSA_HPG
    seq = q_ref.shape[1]

    @pl.when(pl.program_id(1) == 0)
    def _():
        kaug_ref[:, NSA_DK:] = et_ref[...]
        for kind in range(2):
            bkt = bkt_ref[kind]
            for hp in range(hp_n):
                hd = g * hp_n + hp
                acc = jnp.zeros((t, t), F32)
                for r in range(REL_BUCKETS - 1):
                    acc = jnp.where(
                        bkt == r, (rb_ref[hd, r] - rb_ref[hd, REL_BUCKETS - 1]) * LOG2E, acc)
                bias_ref[hp * t:(hp + 1) * t, (1 - kind) * t:(2 - kind) * t] = acc
        bias_ref[:, 2 * t:] = jnp.zeros((hp_n * t, t), F32)

    kaug_ref[:, :NSA_DK] = ks_ref[0]

    rows = hp_n * t
    r2 = lax.broadcasted_iota(jnp.int32, (rows, 2 * t), 0) & (t - 1)
    c2 = lax.broadcasted_iota(jnp.int32, (rows, 2 * t), 1)
    r1 = lax.broadcasted_iota(jnp.int32, (rows, t), 0) & (t - 1)
    c1 = lax.broadcasted_iota(jnp.int32, (rows, t), 1)
    near_ok = c2 <= r2 + t
    first_ok = c2 <= r2
    edge_ok = c1 > r1
    lane = lax.broadcasted_iota(jnp.int32, (t, LANE), 1)

    def near(i):
        if i == 0:
            return 0, 2 * t, bias_ref[:, t:], first_ok
        return (i - 1) * t, (i + 1) * t, bias_ref[:, :2 * t], near_ok

    def scores_sel(i):
        lo, hi = i * t, (i + 1) * t
        n_lo, n_hi, bias, ok = near(i)
        sel = sel_ref[0, 0, lo:hi, :]
        q_sel = jnp.concatenate(
            [jnp.concatenate([q_ref[0, lo:hi, hp * NSA_DK:(hp + 1) * NSA_DK], sel], axis=1)
             for hp in range(hp_n)], axis=0)
        out = [jnp.where(ok, _dot_nt(q_sel, kaug_ref[n_lo:n_hi, :]) + bias, NEG)]
        if n_lo > 0:
            out.append(_dot_nt(q_sel, kaug_ref[:n_lo, :]))
        return out

    def scores_win(i):
        lo, hi = i * t, (i + 1) * t
        n_lo, n_hi, bias, ok = near(i)
        q_win = jnp.concatenate(
            [q_ref[0, lo:hi, hp * NSA_DK:(hp + 1) * NSA_DK] for hp in range(hp_n)], axis=0)
        out = [jnp.where(ok, _dot_nt(q_win, kw_ref[0, n_lo:n_hi, :]) + bias, NEG)]
        w_lo = max(lo - WINDOW, 0)
        if n_lo > w_lo:
            s_m = _dot_nt(q_win, kw_ref[0, w_lo:n_lo, :])
            if lo - WINDOW >= 0:
                s_m = jnp.concatenate([jnp.where(edge_ok, s_m[:, :t], NEG), s_m[:, t:]], axis=1)
            out.append(s_m)
        return out

    def pieces(v_ref, i, far_lo):
        n_lo, n_hi, _, _ = near(i)
        return [v_ref[0, n_lo:n_hi, :]] + ([v_ref[0, far_lo:n_lo, :]] if n_lo > far_lo else [])

    n_tile = seq // t
    nxt_sel = scores_sel(0)
    for i in range(n_tile):
        lo, hi = i * t, (i + 1) * t
        cur_sel = nxt_sel
        cur_win = scores_win(i)
        o_slc = _softmax_pv(cur_sel, pieces(vs_ref, i, 0))
        if i + 1 < n_tile:
            nxt_sel = scores_sel(i + 1)
        o_win = _softmax_pv(cur_win, pieces(vw_ref, i, max(lo - WINDOW, 0)))

        gate = gate_ref[0, lo:hi, :]
        for hp in range(hp_n):
            hd = g * hp_n + hp

            def gcol(branch):
                return jnp.sum(jnp.where(lane == branch * NSA_HEADS + hd, gate, 0.0),
                               axis=1, keepdims=True)
            sl = slice(hp * t, (hp + 1) * t)
            o = (gcol(0) * ocmp_ref[0, lo:hi, hp * NSA_DV:(hp + 1) * NSA_DV].astype(F32)
                 + gcol(1) * o_slc[sl] + gcol(2) * o_win[sl])
            o_ref[0, lo:hi, hp * NSA_DV:(hp + 1) * NSA_DV] = o.astype(BF16)


def _nsa_mixer(x, mod, g1, w_in, pe_k, w1_k, w2_k, pe_v, w1_v, w2_v, rel_bias):
    bsz, seq, d = x.shape
    grp, hpg = NSA_GROUPS, NSA_HPG
    nq, nkv = NSA_HEADS * NSA_DK, NSA_GROUPS * NSA_DK
    n_gate = 3 * NSA_HEADS
    w = jnp.pad(w_in, ((0, 0), (0, LANE - n_gate))).astype(BF16)
    tm = 512
    row = lambda i, b: (b, i, 0)
    assert CMP_RATIO == 2 and NSA_DK == NSA_DV
    n_chunk = seq // CMP_STRIDE
    n_cmp = n_chunk - CMP_RATIO + 1
    kv_shape = jax.ShapeDtypeStruct((bsz, seq, nkv), BF16)
    cmp_shape = jax.ShapeDtypeStruct((bsz, grp, CMP_STRIDE, n_chunk, NSA_DK), BF16)
    cmp_spec = pl.BlockSpec((1, grp, CMP_STRIDE, tm // CMP_STRIDE, NSA_DK),
                            lambda i, b: (b, 0, 0, i, 0))
    q, kc, vc, ks, vs, kw, vw, gates = pl.pallas_call(
        _nsa_proj_kernel,
        out_shape=[jax.ShapeDtypeStruct((bsz, seq, nq), BF16)] + [cmp_shape] * 2 + [kv_shape] * 4
                  + [jax.ShapeDtypeStruct((bsz, seq, LANE), F32)],
        grid=(seq // tm, bsz),
        in_specs=[pl.BlockSpec((1, tm, d), row),
                  pl.BlockSpec((1, N_MOD, d), lambda i, b: (b, 0, 0)),
                  _const_spec((1, d)),
                  _const_spec(w.shape)],
        out_specs=[pl.BlockSpec((1, tm, nq), row)] + [cmp_spec] * 2
                  + [pl.BlockSpec((1, tm, nkv), row)] * 4 + [pl.BlockSpec((1, tm, LANE), row)],
        scratch_shapes=[pltpu.VMEM((2 * grp, tm, NSA_DK), F32)],
        compiler_params=_cparams(2),
        name="nsa_proj",
    )(x, mod, g1.reshape(1, d), w)

    nb = math.gcd(bsz, 4)
    wide = CMP_BLOCK * NSA_DK
    chunk_spec = pl.BlockSpec((nb, grp, CMP_STRIDE, n_chunk, NSA_DK), lambda i: (i, 0, 0, 0, 0))
    out_spec = pl.BlockSpec((nb, grp, n_chunk, NSA_DK), lambda i: (i, 0, 0, 0))
    kcmp, vcmp = pl.pallas_call(
        _nsa_compress_kernel,
        out_shape=[jax.ShapeDtypeStruct((bsz, grp, n_chunk, NSA_DK), BF16)] * 2,
        grid=(bsz // nb,),
        in_specs=[chunk_spec, chunk_spec,
                  _const_spec((1, wide)), _const_spec((1, wide)),
                  _const_spec((wide, CMP_HIDDEN)), _const_spec((wide, CMP_HIDDEN)),
                  _const_spec((CMP_HIDDEN, NSA_DK)), _const_spec((CMP_HIDDEN, NSA_DV))],
        out_specs=[out_spec, out_spec],
        compiler_params=_cparams(1),
        name="nsa_compress",
    )(kc, vc, pe_k.reshape(1, wide), pe_v.reshape(1, wide), w1_k.astype(BF16), w1_v.astype(BF16),
      w2_k.astype(BF16), w2_v.astype(BF16))

    pos = jnp.arange(seq)
    cmp_start = jnp.arange(n_chunk) * CMP_STRIDE
    dist_c = pos[:, None] - (cmp_start + CMP_BLOCK - 1)[None, :]
    ok_c = (dist_c >= 0) & (jnp.arange(n_chunk) < n_cmp)[None, :]
    bucket_c = jnp.where(ok_c, _t5_bucket(dist_c), -1).astype(jnp.int32)
    n_blk = seq // SLC_BLOCK
    blk = jnp.arange(n_blk)
    blk_start = blk * SLC_BLOCK
    overlap_t = ((cmp_start[None, :] < blk_start[:, None] + SLC_BLOCK)
                 & (cmp_start[None, :] + CMP_BLOCK > blk_start[:, None])
                 & (jnp.arange(n_chunk) < n_cmp)[None, :]).astype(F32)
    back = (pos // SLC_BLOCK)[None, :] - blk[:, None]
    forced = (blk[:, None] == 0) | ((back >= 0) & (back < N_LOCAL))
    causal_blk = blk_start[:, None] <= pos[None, :]
    bonus_t = jnp.where(causal_blk, jnp.where(forced, FORCE_BONUS, 0.0), NEG).astype(F32)

    ocmp, selneg = pl.pallas_call(
        _nsa_select_kernel,
        out_shape=[jax.ShapeDtypeStruct((bsz, seq, nq), BF16),
                   jax.ShapeDtypeStruct((bsz, grp, seq, LANE), BF16)],
        grid=(grp, bsz),
        in_specs=[pl.BlockSpec(memory_space=pltpu.SMEM),
                  pl.BlockSpec((1, seq, hpg * NSA_DK), lambda g, b: (b, 0, g)),
                  pl.BlockSpec((1, 1, n_chunk, NSA_DK), lambda g, b: (b, g, 0, 0)),
                  pl.BlockSpec((1, 1, n_chunk, NSA_DV), lambda g, b: (b, g, 0, 0)),
                  _const_spec(bucket_c.shape),
                  _const_spec(overlap_t.shape),
                  _const_spec(bonus_t.shape)],
        out_specs=[pl.BlockSpec((1, seq, hpg * NSA_DV), lambda g, b: (b, 0, g)),
                   pl.BlockSpec((1, 1, seq, LANE), lambda g, b: (b, g, 0, 0))],
        scratch_shapes=[pltpu.VMEM((hpg, seq, n_chunk), F32)],
        compiler_params=_cparams(2),
        name="nsa_select",
    )(rel_bias, q, kcmp, vcmp, bucket_c, overlap_t, bonus_t)

    t = ATT_TILE
    rr = jnp.arange(t)
    near = jnp.stack([_t5_bucket(k * t + rr[:, None] - rr[None, :]) for k in range(2)]).astype(jnp.int32)
    e_t = (jnp.arange(seq)[:, None] // SLC_BLOCK == jnp.arange(LANE)[None, :]).astype(BF16)
    grp_blk = lambda g, b: (b, 0, g)
    return pl.pallas_call(
        _nsa_attn_kernel,
        out_shape=jax.ShapeDtypeStruct((bsz, seq, nq), BF16),
        grid=(grp, bsz),
        in_specs=[pl.BlockSpec(memory_space=pltpu.SMEM),
                  pl.BlockSpec((1, seq, hpg * NSA_DK), grp_blk),
                  pl.BlockSpec((1, 1, seq, LANE), lambda g, b: (b, g, 0, 0)),
                  pl.BlockSpec((1, seq, NSA_DK), grp_blk),
                  pl.BlockSpec((1, seq, NSA_DV), grp_blk),
                  pl.BlockSpec((1, seq, NSA_DK), grp_blk),
                  pl.BlockSpec((1, seq, NSA_DV), grp_blk),
                  _const_spec(e_t.shape),
                  pl.BlockSpec((1, seq, hpg * NSA_DV), grp_blk),
                  pl.BlockSpec((1, seq, LANE), lambda g, b: (b, 0, 0)),
                  _const_spec(near.shape)],
        out_specs=pl.BlockSpec((1, seq, hpg * NSA_DV), grp_blk),
        scratch_shapes=[pltpu.VMEM((seq, NSA_DK + LANE), BF16),
                        pltpu.VMEM((hpg * t, 3 * t), F32)],
        compiler_params=_cparams(2),
        name="nsa_attn",
    )(rel_bias, q, selneg, ks, vs, kw, vw, e_t, ocmp, gates, near)


def _s5_kernel(x_ref, mod_ref, g1_ref, perm_ref, permt_ref, bm_ref, cm_ref, are_ref, aim_ref,
               dskip_ref, y_ref, bu_ref, state_ref):
    bb, ts, d = x_ref.shape
    rows = bb * ts
    n_slab = d // LANE
    wid = S5_SLAB * S5_STATE

    @pl.when(pl.program_id(1) == 0)
    def _():
        state_ref[...] = jnp.zeros_like(state_ref)

    mod = mod_ref[...]
    u = (_rms(x_ref[...], g1_ref[...]) * (1.0 + mod[:, 1:2, :]) + mod[:, 0:1, :]).reshape(rows, d)
    u_tb = _dot(perm_ref[...], u.astype(BF16)).astype(BF16)
    def project_in(k):
        bu_ref[:, 2 * k * wid:2 * (k + 1) * wid] = _dot(u_tb[:, k * LANE:(k + 1) * LANE], bm_ref[k])

    def scan(k):
        c_re, c_im = 2 * k * wid, (2 * k + 1) * wid
        a_re = jnp.broadcast_to(are_ref[k], (bb, wid))
        a_im = jnp.broadcast_to(aim_ref[k], (bb, wid))
        xr, xi = state_ref[:, c_re:c_re + wid], state_ref[:, c_im:c_im + wid]
        for t in range(ts):
            r0 = t * bb
            nr = a_re * xr - a_im * xi + bu_ref[r0:r0 + bb, c_re:c_re + wid]
            ni = a_re * xi + a_im * xr + bu_ref[r0:r0 + bb, c_im:c_im + wid]
            bu_ref[r0:r0 + bb, c_re:c_re + wid] = nr
            bu_ref[r0:r0 + bb, c_im:c_im + wid] = ni
            xr, xi = nr, ni
        state_ref[:, c_re:c_re + wid] = xr
        state_ref[:, c_im:c_im + wid] = xi

    def project_out(k):
        return _dot(bu_ref[:, 2 * k * wid:2 * (k + 1) * wid].astype(BF16), cm_ref[k]).astype(BF16)

    ys = []
    project_in(0)
    for k in range(n_slab):
        if k + 1 < n_slab:
            project_in(k + 1)
        scan(k)
        ys.append(project_out(k))
    y_tb = jnp.concatenate(ys, axis=1)
    y = _dot(permt_ref[...], y_tb) + dskip_ref[...] * u
    y_ref[...] = _gelu_tanh(y).astype(BF16).reshape(bb, ts, d)


def _s5_mixer(x, mod, g1, lam_re, lam_im, log_dt, b_re, b_im, c_re, c_im, d_skip):
    bsz, seq, d = x.shape
    n_slab = d // LANE
    dt = jnp.exp(log_dt)[:, None]
    mag = jnp.exp(lam_re * dt)
    lb_re = mag * jnp.cos(lam_im * dt)
    lb_im = mag * jnp.sin(lam_im * dt)
    den = lam_re * lam_re + lam_im * lam_im
    f_re = ((lb_re - 1.0) * lam_re + lb_im * lam_im) / den
    f_im = (lb_im * lam_re - (lb_re - 1.0) * lam_im) / den
    bb_re = f_re[..., None] * b_re - f_im[..., None] * b_im
    bb_im = f_re[..., None] * b_im + f_im[..., None] * b_re
    eye = jnp.eye(S5_SLAB, dtype=F32)
    wid = S5_SLAB * S5_STATE

    def in_mat(b):
        b = b.reshape(n_slab, S5_SLAB, S5_STATE, S5_GROUP)
        return jnp.einsum('kgpi,gh->kgihp', b, eye).reshape(n_slab, LANE, wid)

    def out_mat(c):
        c = c.reshape(n_slab, S5_SLAB, S5_GROUP, S5_STATE)
        return jnp.einsum('kgip,gh->kgphi', c, eye).reshape(n_slab, wid, LANE)

    bm = jnp.concatenate([in_mat(bb_re), in_mat(bb_im)], axis=2).astype(BF16)
    cm = jnp.concatenate([out_mat(c_re), -out_mat(c_im)], axis=1).astype(BF16)
    a_re = lb_re.reshape(n_slab, 1, wid)
    a_im = lb_im.reshape(n_slab, 1, wid)

    bb, ts = S5_BB, 64
    rows = bb * ts
    r = jnp.arange(rows)
    perm = ((r % bb)[:, None] * ts + (r // bb)[:, None] == r[None, :]).astype(BF16)
    blk = lambda i, t: (i, t, 0)
    return pl.pallas_call(
        _s5_kernel,
        out_shape=jax.ShapeDtypeStruct((bsz, seq, d), BF16),
        grid=(bsz // bb, seq // ts),
        in_specs=[pl.BlockSpec((bb, ts, d), blk),
                  pl.BlockSpec((bb, N_MOD, d), lambda i, t: (i, 0, 0)),
                  _const_spec((1, d)),
                  _const_spec((rows, rows)), _const_spec((rows, rows)),
                  _const_spec(bm.shape), _const_spec(cm.shape),
                  _const_spec(a_re.shape), _const_spec(a_im.shape),
                  _const_spec((1, d))],
        out_specs=pl.BlockSpec((bb, ts, d), blk),
        scratch_shapes=[pltpu.VMEM((rows, 2 * wid * n_slab), F32),
                        pltpu.VMEM((bb, 2 * wid * n_slab), F32)],
        compiler_params=_cparams(2),
        name="s5_scan",
    )(x, mod, g1.reshape(1, d), perm, perm.T, bm, cm, a_re, a_im, d_skip.reshape(1, d))


def kernel(x, c, ada_w, ada_b, norm1_g, norm2_g, final_g, rel_bias, mla_w_dkv, mla_g_q, mla_g_kv, mla_w_uq, mla_w_ukv, mla_w_o, nsa_w_in, nsa_pe_k, nsa_w1_k, nsa_w2_k, nsa_pe_v, nsa_w1_v, nsa_w2_v, nsa_w_o, s5_lam_re, s5_lam_im, s5_log_dt, s5_b_re, s5_b_im, s5_c_re, s5_c_im, s5_d, s5_w_glu, ffn_w_up, ffn_conv_w, ffn_conv_b, ffn_w_down):
    depth = ada_w.shape[0]
    mods = _ada_mod(c, ada_w, ada_b)
    for i in range(depth):
        mod = mods[i]
        kind, j = i % N_MIXERS, i // N_MIXERS
        if kind == 0:
            o = _mla_mixer(x, mod, norm1_g[i], mla_w_dkv[j], mla_g_q[j], mla_g_kv[j],
                           mla_w_uq[j], mla_w_ukv[j])
            w_post, glu = mla_w_o[j], False
        elif kind == 1:
            o = _nsa_mixer(x, mod, norm1_g[i], nsa_w_in[j], nsa_pe_k[j], nsa_w1_k[j], nsa_w2_k[j],
                           nsa_pe_v[j], nsa_w1_v[j], nsa_w2_v[j], rel_bias)
            w_post, glu = nsa_w_o[j], False
        else:
            o = _s5_mixer(x, mod, norm1_g[i], s5_lam_re[j], s5_lam_im[j], s5_log_dt[j],
                          s5_b_re[j], s5_b_im[j], s5_c_re[j], s5_c_im[j], s5_d[j])
            w_post, glu = s5_w_glu[j], True
        x = _post_ffn(x, o, mod, w_post.astype(BF16), norm2_g[i], ffn_w_up[i].astype(BF16),
                      ffn_conv_w[i], ffn_conv_b[i], ffn_w_down[i].astype(BF16), final_g,
                      glu=glu, final=(i == depth - 1))
    return x
```

```python
import functools
import math

import jax
import jax.numpy as jnp
from jax import lax
from jax.experimental import pallas as pl
from jax.experimental.pallas import tpu as pltpu

F32 = jnp.float32
BF16 = jnp.bfloat16

NORM_EPS = 1e-6
N_MOD = 6
N_MIXERS = 3

MLA_HEADS = 8
MLA_Q_LORA = 512
MLA_KV_LORA = 256
MLA_NOPE = 128
MLA_ROPE = 64
MLA_V = 128
MLA_QK = MLA_NOPE + MLA_ROPE
ROPE_THETA = 10000.0

NSA_HEADS = 8
NSA_GROUPS = 2
NSA_HPG = NSA_HEADS // NSA_GROUPS
NSA_DK = 128
NSA_DV = 128
CMP_BLOCK = 32
CMP_STRIDE = 16
CMP_RATIO = CMP_BLOCK // CMP_STRIDE
CMP_HIDDEN = 256
SLC_BLOCK = 64
SLC_TOPN = 16
N_LOCAL = 2
FORCE_BONUS = 1e4
WINDOW = 512
REL_BUCKETS = 32
REL_MAX_DIST = 128

S5_GROUP = 16
S5_STATE = 64
CONV_WIDTH = 3

LANE = 128
NEG = -1e30
LOG2E = math.log2(math.e)
VMEM_LIMIT = 56 * 1024 * 1024

ATT_TILE = 128
MLA_TQ = 256
S5_BB = 8
S5_SLAB = 8


def _cparams(n_axes):
    return pltpu.CompilerParams(
        dimension_semantics=("arbitrary",) * n_axes,
        vmem_limit_bytes=VMEM_LIMIT)


def _const_spec(shape):
    zeros = (0,) * len(shape)
    return pl.BlockSpec(shape, lambda *_: zeros, pipeline_mode=pl.Buffered(1))


def _dot(a, b):
    return jnp.dot(a, b, preferred_element_type=F32)


def _dot_nt(a, b):
    return lax.dot_general(a, b, (((1,), (1,)), ((), ())), preferred_element_type=F32)


def _rms(x, g):
    return x * lax.rsqrt(jnp.mean(x * x, axis=-1, keepdims=True) + NORM_EPS) * g


def _norm_mod(x, g, shift, scale):
    return _rms(x, g) * (1.0 + scale) + shift


def _gelu_tanh(x):
    c = math.sqrt(2.0 / math.pi)
    return 0.5 * x * (1.0 + jnp.tanh(c * (x + 0.044715 * (x * x * x))))


def _sigmoid(x):
    return 1.0 / (1.0 + jnp.exp(-x))


def _softmax_pv(scores, values):
    m = functools.reduce(jnp.maximum, [jnp.max(s, axis=1, keepdims=True) for s in scores])
    acc = functools.reduce(
        jnp.add, [_dot(jnp.exp2(s - m).astype(BF16), v) for s, v in zip(scores, values)])
    width = acc.shape[1] // 2
    return acc[:, :width] / acc[:, width:]


def _t5_bucket(dist):
    n = jnp.maximum(dist, 0)
    max_exact = REL_BUCKETS // 2
    nf = jnp.maximum(n, 1).astype(F32)
    large = max_exact + (jnp.log(nf / max_exact) / math.log(REL_MAX_DIST / max_exact)
                         * (REL_BUCKETS - max_exact)).astype(jnp.int32)
    large = jnp.minimum(large, REL_BUCKETS - 1)
    return jnp.where(n < max_exact, n, large)


def _ada_kernel(c_ref, w_ref, b_ref, o_ref):
    c = c_ref[...]
    c_act = c * _sigmoid(c)
    o_ref[0] = jnp.dot(c_act, w_ref[0], preferred_element_type=F32,
                       precision=lax.Precision.HIGHEST) + b_ref[0]


def _ada_mod(c, ada_w, ada_b):
    depth, d, n = ada_w.shape
    bsz = c.shape[0]
    tn = 1024
    out = pl.pallas_call(
        _ada_kernel,
        out_shape=jax.ShapeDtypeStruct((depth, bsz, n), F32),
        grid=(depth, n // tn),
        in_specs=[pl.BlockSpec((bsz, d), lambda i, j: (0, 0)),
                  pl.BlockSpec((1, d, tn), lambda i, j: (i, 0, j)),
                  pl.BlockSpec((1, 1, tn), lambda i, j: (i, 0, j))],
        out_specs=pl.BlockSpec((1, bsz, tn), lambda i, j: (i, 0, j)),
        compiler_params=_cparams(2),
        name="ada_mod",
    )(c, ada_w, ada_b.reshape(depth, 1, n))
    return out.reshape(depth, bsz, N_MOD, d)


def _post_ffn_kernel(x_ref, o_ref, mod_ref, wpost_ref, g2_ref, wup_ref, cw_ref, cb_ref,
                     wdown_ref, gf_ref, out_ref, act_ref, carry_ref,
                     *, glu, final, fc):
    tm = x_ref.shape[1]
    d_ff = wdown_ref.shape[0]

    @pl.when(pl.program_id(1) == 0)
    def _():
        carry_ref[...] = jnp.zeros_like(carry_ref)

    x = x_ref[0]
    mod = mod_ref[0]
    y = _dot(o_ref[0], wpost_ref[...])
    if glu:
        d = x.shape[-1]
        y = y[:, :d] * _sigmoid(y[:, d:])
    x1 = x + mod[2:3] * y
    h = _norm_mod(x1, g2_ref[...], mod[3:4], mod[4:5]).astype(BF16)

    row = lax.broadcasted_iota(jnp.int32, (8, fc), 0)

    def conv(z, col):
        c = carry_ref[:, col:col + fc]
        carry_ref[:, col:col + fc] = z[tm - 8:, :]
        r1 = pltpu.roll(z, 1, 0)
        r2 = pltpu.roll(z, 2, 0)
        s1 = jnp.concatenate([jnp.where(row < 1, pltpu.roll(c, 1, 0), r1[:8]), r1[8:]], axis=0)
        s2 = jnp.concatenate([jnp.where(row < 2, pltpu.roll(c, 2, 0), r2[:8]), r2[8:]], axis=0)
        w = cw_ref[:, col:col + fc]
        return w[0:1] * s2 + w[1:2] * s1 + w[2:3] * z + cb_ref[:, col:col + fc]

    for f in range(0, d_ff, fc):
        val = conv(_dot(h, wup_ref[:, f:f + fc]), f)
        gate = conv(_dot(h, wup_ref[:, d_ff + f:d_ff + f + fc]), d_ff + f)
        act_ref[:, f:f + fc] = (val * (gate * _sigmoid(gate))).astype(BF16)

    x2 = x1 + mod[5:6] * _dot(act_ref[...], wdown_ref[...])
    if final:
        x2 = _rms(x2, gf_ref[...])
    out_ref[0] = x2


def _post_ffn(x, o, mod, w_post, g2, w_up, conv_w, conv_b, w_down, final_g, *, glu, final):
    bsz, seq, d = x.shape
    d_ff = w_down.shape[0]
    tm = 512
    fc = 256
    row = lambda b, i: (b, i, 0)
    return pl.pallas_call(
        functools.partial(_post_ffn_kernel, glu=glu, final=final, fc=fc),
        out_shape=jax.ShapeDtypeStruct((bsz, seq, d), F32),
        grid=(bsz, seq // tm),
        in_specs=[pl.BlockSpec((1, tm, d), row),
                  pl.BlockSpec((1, tm, d), row),
                  pl.BlockSpec((1, N_MOD, d), lambda b, i: (b, 0, 0)),
                  _const_spec(w_post.shape),
                  _const_spec((1, d)),
                  _const_spec(w_up.shape),
                  _const_spec(conv_w.shape),
                  _const_spec((1, 2 * d_ff)),
                  _const_spec(w_down.shape),
                  _const_spec((1, d))],
        out_specs=pl.BlockSpec((1, tm, d), row),
        scratch_shapes=[pltpu.VMEM((tm, d_ff), BF16),
                        pltpu.VMEM((8, 2 * d_ff), F32)],
        compiler_params=_cparams(2),
        name="post_ffn",
    )(x, o, mod, w_post, g2.reshape(1, d), w_up, conv_w, conv_b.reshape(1, -1), w_down,
      final_g.reshape(1, d))


def _rope_tables(seq, reps):
    half = MLA_ROPE // 2
    inv = ROPE_THETA ** (-jnp.arange(half, dtype=F32) / half)
    ang = jnp.arange(seq).astype(F32)[:, None] * inv[None, :]
    cos, sin = jnp.cos(ang), jnp.sin(ang)
    cos64 = jnp.concatenate([cos, cos], axis=1)
    sin64 = jnp.concatenate([-sin, sin], axis=1)
    return jnp.tile(cos64, (1, reps)), jnp.tile(sin64, (1, reps))


def _mla_proj_kernel(x_ref, mod_ref, g1_ref, wdkv_ref, gq_ref, gkv_ref, wuq_ref, wukv_ref,
                     cos_ref, sin_ref, qn_ref, qr_ref, kn_ref, kr_ref, v_ref):
    nq = MLA_HEADS * MLA_NOPE
    nr = MLA_HEADS * MLA_ROPE
    scale = MLA_QK ** -0.5 * LOG2E
    mod = mod_ref[0]
    h = _norm_mod(x_ref[0], g1_ref[...], mod[0:1], mod[1:2]).astype(BF16)
    dkv = _dot(h, wdkv_ref[...])
    c_q = _rms(dkv[:, :MLA_Q_LORA], gq_ref[...]).astype(BF16)
    c_kv = _rms(dkv[:, MLA_Q_LORA:MLA_Q_LORA + MLA_KV_LORA], gkv_ref[...]).astype(BF16)
    cos = cos_ref[...]
    sin = sin_ref[...]
    r0 = MLA_Q_LORA + MLA_KV_LORA
    kr_ref[0] = (dkv[:, r0:r0 + LANE] * cos[:, :LANE]
                 + dkv[:, r0 + LANE:r0 + 2 * LANE] * sin[:, :LANE]).astype(BF16)
    q = _dot(c_q, wuq_ref[...])
    qn_ref[0] = (q[:, :nq] * scale).astype(BF16)
    qr_ref[0] = ((q[:, nq:nq + nr] * cos + q[:, nq + nr:] * sin) * scale).astype(BF16)
    kv = _dot(c_kv, wukv_ref[...])
    kn_ref[0] = kv[:, :nq].astype(BF16)
    v_ref[0] = kv[:, nq:].astype(BF16)


def _mla_attn_kernel(qn_ref, qr_ref, kn_ref, kr_ref, vin_ref, o_ref, v_ref):
    seq = qn_ref.shape[1]
    tq = MLA_TQ
    v_ref[0, :, :MLA_V] = vin_ref[0]
    v_ref[0, :, MLA_V:] = jnp.ones((seq, MLA_V), BF16)
    odd = pl.program_id(1) % 2
    lane = lax.broadcasted_iota(jnp.int32, (tq, LANE), 1)
    keep = (lane >= MLA_ROPE) == (odd == 1)
    row = lax.broadcasted_iota(jnp.int32, (tq, tq), 0)
    col = lax.broadcasted_iota(jnp.int32, (tq, tq), 1)
    def scores(i):
        lo, hi = i * tq, (i + 1) * tq
        qr = jnp.where(keep, qr_ref[0, lo:hi, :], jnp.zeros((), BF16))
        q = jnp.concatenate([qn_ref[0, lo:hi, :], qr], axis=1)
        k_d = jnp.concatenate([kn_ref[0, lo:hi, :], kr_ref[0, lo:hi, :]], axis=1)
        out = [jnp.where(col <= row, _dot_nt(q, k_d), NEG)]
        if i > 0:
            k_o = jnp.concatenate([kn_ref[0, :lo, :], kr_ref[0, :lo, :]], axis=1)
            out.append(_dot_nt(q, k_o))
        return out

    n_tile = seq // tq
    nxt = scores(0)
    for i in range(n_tile):
        lo, hi = i * tq, (i + 1) * tq
        cur = nxt
        if i + 1 < n_tile:
            nxt = scores(i + 1)
        values = [v_ref[0, lo:hi, :]] + ([v_ref[0, :lo, :]] if i > 0 else [])
        o_ref[0, lo:hi, :] = _softmax_pv(cur, values).astype(BF16)


def _mla_mixer(x, mod, g1, w_dkv, g_q, g_kv, w_uq, w_ukv):
    bsz, seq, d = x.shape
    hds = MLA_HEADS
    half = MLA_ROPE // 2
    swap = jnp.concatenate([jnp.arange(half, MLA_ROPE), jnp.arange(half)])
    r0 = MLA_Q_LORA + MLA_KV_LORA
    kr = w_dkv[:, r0:]
    wdkv = jnp.concatenate([w_dkv[:, :r0], kr, kr, kr[:, swap], kr[:, swap]], axis=1).astype(BF16)
    wq = w_uq.reshape(MLA_Q_LORA, hds, MLA_QK)
    wq_r = wq[:, :, MLA_NOPE:]
    wuq = jnp.concatenate([wq[:, :, :MLA_NOPE].reshape(MLA_Q_LORA, -1),
                           wq_r.reshape(MLA_Q_LORA, -1),
                           wq_r[:, :, swap].reshape(MLA_Q_LORA, -1)], axis=1).astype(BF16)
    wkv = w_ukv.reshape(MLA_KV_LORA, hds, MLA_NOPE + MLA_V)
    wukv = jnp.concatenate([wkv[:, :, :MLA_NOPE].reshape(MLA_KV_LORA, -1),
                            wkv[:, :, MLA_NOPE:].reshape(MLA_KV_LORA, -1)], axis=1).astype(BF16)
    cos, sin = _rope_tables(seq, hds)
    nq, nr = hds * MLA_NOPE, hds * MLA_ROPE
    tm = 512
    row = lambda i, b: (b, i, 0)
    qn, qr, kn, kr2, v = pl.pallas_call(
        _mla_proj_kernel,
        out_shape=[jax.ShapeDtypeStruct((bsz, seq, nq), BF16),
                   jax.ShapeDtypeStruct((bsz, seq, nr), BF16),
                   jax.ShapeDtypeStruct((bsz, seq, nq), BF16),
                   jax.ShapeDtypeStruct((bsz, seq, LANE), BF16),
                   jax.ShapeDtypeStruct((bsz, seq, nq), BF16)],
        grid=(seq // tm, bsz),
        in_specs=[pl.BlockSpec((1, tm, d), row),
                  pl.BlockSpec((1, N_MOD, d), lambda i, b: (b, 0, 0)),
                  _const_spec((1, d)),
                  _const_spec(wdkv.shape),
                  _const_spec((1, MLA_Q_LORA)),
                  _const_spec((1, MLA_KV_LORA)),
                  _const_spec(wuq.shape),
                  _const_spec(wukv.shape),
                  pl.BlockSpec((tm, nr), lambda i, b: (i, 0)),
                  pl.BlockSpec((tm, nr), lambda i, b: (i, 0))],
        out_specs=[pl.BlockSpec((1, tm, nq), row),
                   pl.BlockSpec((1, tm, nr), row),
                   pl.BlockSpec((1, tm, nq), row),
                   pl.BlockSpec((1, tm, LANE), row),
                   pl.BlockSpec((1, tm, nq), row)],
        compiler_params=_cparams(2),
        name="mla_proj",
    )(x, mod, g1.reshape(1, d), wdkv, g_q.reshape(1, -1), g_kv.reshape(1, -1), wuq, wukv, cos, sin)

    head = lambda b, h: (b, 0, h)
    return pl.pallas_call(
        _mla_attn_kernel,
        out_shape=jax.ShapeDtypeStruct((bsz, seq, hds * MLA_V), BF16),
        grid=(bsz, hds),
        in_specs=[pl.BlockSpec((1, seq, LANE), head),
                  pl.BlockSpec((1, seq, LANE), lambda b, h: (b, 0, h // 2)),
                  pl.BlockSpec((1, seq, LANE), head),
                  pl.BlockSpec((1, seq, LANE), lambda b, h: (b, 0, 0)),
                  pl.BlockSpec((1, seq, LANE), head)],
        out_specs=pl.BlockSpec((1, seq, LANE), head),
        scratch_shapes=[pltpu.VMEM((1, seq, 2 * MLA_V), BF16)],
        compiler_params=_cparams(2),
        name="mla_attn",
    )(qn, qr, kn, kr2, v)


def _nsa_proj_kernel(x_ref, mod_ref, g1_ref, w_ref, q_ref, kc_ref, vc_ref, ks_ref, vs_ref,
                     kw_ref, vw_ref, gate_ref, cmp_ref):
    nq = NSA_HEADS * NSA_DK
    nkv = NSA_GROUPS * NSA_DK
    tm = x_ref.shape[1]
    mod = mod_ref[0]
    h = _norm_mod(x_ref[0], g1_ref[...], mod[0:1], mod[1:2]).astype(BF16)
    p = _dot(h, w_ref[...])
    q_ref[0] = (p[:, :nq] * (NSA_DK ** -0.5 * LOG2E)).astype(BF16)
    for n, ref in enumerate((ks_ref, vs_ref, kw_ref, vw_ref)):
        ref[0] = p[:, nq + (n + 2) * nkv:nq + (n + 3) * nkv].astype(BF16)
    gate_ref[0] = _sigmoid(p[:, nq + 6 * nkv:])
    for n, ref in enumerate((kc_ref, vc_ref)):
        for g in range(NSA_GROUPS):
            c0 = nq + n * nkv + g * NSA_DK
            stage = cmp_ref.at[n * NSA_GROUPS + g]
            stage[...] = p[:, c0:c0 + NSA_DK]
            for r in range(CMP_STRIDE):
                ref[0, g, r] = stage[pl.ds(r, tm // CMP_STRIDE, stride=CMP_STRIDE), :].astype(BF16)


def _nsa_compress_kernel(ck_ref, cv_ref, pek_ref, pev_ref, w1k_ref, w1v_ref, w2k_ref, w2v_ref,
                         ok_ref, ov_ref):
    nb, grp, _, n_chunk, dk = ck_ref.shape
    half = CMP_STRIDE * dk
    rows = nb * grp * n_chunk
    for c_ref, pe_ref, w1_ref, w2_ref, o_ref in ((ck_ref, pek_ref, w1k_ref, w2k_ref, ok_ref),
                                                 (cv_ref, pev_ref, w1v_ref, w2v_ref, ov_ref)):
        chunks = jnp.concatenate(
            [jnp.concatenate([c_ref[b, g, r] for r in range(CMP_STRIDE)], axis=1)
             for b in range(nb) for g in range(grp)], axis=0)
        first = _dot(chunks, w1_ref[:half, :])
        second = _dot(chunks, w1_ref[half:, :])
        pe = jnp.broadcast_to(pe_ref[...], (8, 2 * half)).astype(BF16)
        const = _dot(pe, w1_ref[...])[0:1]
        hid = _gelu_tanh(first + pltpu.roll(second, rows - 1, 0) + const).astype(BF16)
        o_ref[...] = _dot(hid, w2_ref[...]).astype(BF16).reshape(nb, grp, n_chunk, dk)


def _nsa_select_kernel(rb_ref, q_ref, kc_ref, vc_ref, bucket_ref, ovl_ref, bonus_ref,
                       ocmp_ref, sel_ref, bias_ref):
    g = pl.program_id(0)
    seq = q_ref.shape[1]
    n_blk = bonus_ref.shape[0]
    bucket = bucket_ref[...]

    @pl.when(pl.program_id(1) == 0)
    def _():
        for hp in range(NSA_HPG):
            acc = jnp.zeros(bucket.shape, F32)
            for r in range(REL_BUCKETS):
                acc = jnp.where(bucket == r, rb_ref[g * NSA_HPG + hp, r] * LOG2E, acc)
            bias_ref[hp] = acc

    valid = bucket >= 0
    kc = kc_ref[0, 0]
    vc = vc_ref[0, 0]
    psum = jnp.zeros(bucket.shape, F32)
    for hp in range(NSA_HPG):
        s = _dot_nt(q_ref[0, :, hp * NSA_DK:(hp + 1) * NSA_DK], kc) + bias_ref[hp]
        s = jnp.where(valid, s, NEG)
        m = jnp.max(s, axis=1, keepdims=True)
        m = jnp.where(m > 0.5 * NEG, m, 0.0)
        p = jnp.exp2(s - m)
        p = p / jnp.maximum(jnp.sum(p, axis=1, keepdims=True), 1e-30)
        ocmp_ref[0, :, hp * NSA_DV:(hp + 1) * NSA_DV] = _dot(p.astype(BF16), vc).astype(BF16)
        psum = psum + p

    imp = lax.dot_general(ovl_ref[...], psum, (((1,), (1,)), ((), ())),
                          preferred_element_type=F32, precision=lax.Precision.HIGHEST)
    bonus = bonus_ref[...]
    score = jnp.where(bonus > 0.5 * NEG, imp + bonus, NEG)
    blk = lax.broadcasted_iota(jnp.int32, (n_blk, seq), 0)
    rank = jnp.zeros((n_blk, seq), F32)
    for i in range(n_blk):
        row = score[i:i + 1, :]
        rank = rank + jnp.where(blk > i, jnp.where(row >= score, 1.0, 0.0),
                                jnp.where(row > score, 1.0, 0.0))
    selneg = jnp.where(rank < float(SLC_TOPN), jnp.where(score > 0.5 * NEG, 0.0, NEG), NEG)
    pad = jnp.zeros((LANE - n_blk, seq), F32)
    sel_ref[0, 0] = jnp.concatenate([selneg, pad], axis=0).T.astype(BF16)


def _nsa_attn_kernel(rb_ref, q_ref, sel_ref, ks_ref, vsin_ref, kw_ref, vwin_ref, et_ref, ocmp_ref,
                     gate_ref, bkt_ref, o_ref, kaug_ref, bias_ref, vs_ref, vw_ref):
    g = pl.program_id(0)
    t = ATT_TILE
    hp_n = NSA_HPG
    seq = q_ref.shape[1]

    @pl.when(pl.program_id(1) == 0)
    def _():
        kaug_ref[:, NSA_DK:] = et_ref[...]
        vs_ref[0, :, NSA_DV:] = jnp.ones((seq, NSA_DV), BF16)
        vw_ref[0, :, NSA_DV:] = jnp.ones((seq, NSA_DV), BF16)
        for kind in range(2):
            bkt = bkt_ref[kind]
            for hp in range(hp_n):
                hd = g * hp_n + hp
                acc = jnp.zeros((t, t), F32)
                for r in range(REL_BUCKETS - 1):
                    acc = jnp.where(
                        bkt == r, (rb_ref[hd, r] - rb_ref[hd, REL_BUCKETS - 1]) * LOG2E, acc)
                bias_ref[hp * t:(hp + 1) * t, (1 - kind) * t:(2 - kind) * t] = acc
        bias_ref[:, 2 * t:] = jnp.zeros((hp_n * t, t), F32)

    kaug_ref[:, :NSA_DK] = ks_ref[0]
    vs_ref[0, :, :NSA_DV] = vsin_ref[0]
    vw_ref[0, :, :NSA_DV] = vwin_ref[0]

    rows = hp_n * t
    r2 = lax.broadcasted_iota(jnp.int32, (rows, 2 * t), 0) & (t - 1)
    c2 = lax.broadcasted_iota(jnp.int32, (rows, 2 * t), 1)
    r1 = lax.broadcasted_iota(jnp.int32, (rows, t), 0) & (t - 1)
    c1 = lax.broadcasted_iota(jnp.int32, (rows, t), 1)
    near_ok = c2 <= r2 + t
    first_ok = c2 <= r2
    edge_ok = c1 > r1
    lane = lax.broadcasted_iota(jnp.int32, (t, LANE), 1)

    def near(i):
        if i == 0:
            return 0, 2 * t, bias_ref[:, t:], first_ok
        return (i - 1) * t, (i + 1) * t, bias_ref[:, :2 * t], near_ok

    def scores_sel(i):
        lo, hi = i * t, (i + 1) * t
        n_lo, n_hi, bias, ok = near(i)
        sel = sel_ref[0, 0, lo:hi, :]
        q_sel = jnp.concatenate(
            [jnp.concatenate([q_ref[0, lo:hi, hp * NSA_DK:(hp + 1) * NSA_DK], sel], axis=1)
             for hp in range(hp_n)], axis=0)
        out = [jnp.where(ok, _dot_nt(q_sel, kaug_ref[n_lo:n_hi, :]) + bias, NEG)]
        if n_lo > 0:
            out.append(_dot_nt(q_sel, kaug_ref[:n_lo, :]))
        return out

    def scores_win(i):
        lo, hi = i * t, (i + 1) * t
        n_lo, n_hi, bias, ok = near(i)
        q_win = jnp.concatenate(
            [q_ref[0, lo:hi, hp * NSA_DK:(hp + 1) * NSA_DK] for hp in range(hp_n)], axis=0)
        out = [jnp.where(ok, _dot_nt(q_win, kw_ref[0, n_lo:n_hi, :]) + bias, NEG)]
        w_lo = max(lo - WINDOW, 0)
        if n_lo > w_lo:
            s_m = _dot_nt(q_win, kw_ref[0, w_lo:n_lo, :])
            if lo - WINDOW >= 0:
                s_m = jnp.concatenate([jnp.where(edge_ok, s_m[:, :t], NEG), s_m[:, t:]], axis=1)
            out.append(s_m)
        return out

    def pieces(v_ref, i, far_lo):
        n_lo, n_hi, _, _ = near(i)
        return [v_ref[0, n_lo:n_hi, :]] + ([v_ref[0, far_lo:n_lo, :]] if n_lo > far_lo else [])

    n_tile = seq // t
    nxt_sel = scores_sel(0)
    for i in range(n_tile):
        lo, hi = i * t, (i + 1) * t
        cur_sel = nxt_sel
        cur_win = scores_win(i)
        o_slc = _softmax_pv(cur_sel, pieces(vs_ref, i, 0))
        if i + 1 < n_tile:
            nxt_sel = scores_sel(i + 1)
        o_win = _softmax_pv(cur_win, pieces(vw_ref, i, max(lo - WINDOW, 0)))

        gate = gate_ref[0, lo:hi, :]
        for hp in range(hp_n):
            hd = g * hp_n + hp

            def gcol(branch):
                return jnp.sum(jnp.where(lane == branch * NSA_HEADS + hd, gate, 0.0),
                               axis=1, keepdims=True)
            sl = slice(hp * t, (hp + 1) * t)
            o = (gcol(0) * ocmp_ref[0, lo:hi, hp * NSA_DV:(hp + 1) * NSA_DV].astype(F32)
                 + gcol(1) * o_slc[sl] + gcol(2) * o_win[sl])
            o_ref[0, lo:hi, hp * NSA_DV:(hp + 1) * NSA_DV] = o.astype(BF16)


def _nsa_mixer(x, mod, g1, w_in, pe_k, w1_k, w2_k, pe_v, w1_v, w2_v, rel_bias):
    bsz, seq, d = x.shape
    grp, hpg = NSA_GROUPS, NSA_HPG
    nq, nkv = NSA_HEADS * NSA_DK, NSA_GROUPS * NSA_DK
    n_gate = 3 * NSA_HEADS
    w = jnp.pad(w_in, ((0, 0), (0, LANE - n_gate))).astype(BF16)
    tm = 512
    row = lambda i, b: (b, i, 0)
    assert CMP_RATIO == 2 and NSA_DK == NSA_DV
    n_chunk = seq // CMP_STRIDE
    n_cmp = n_chunk - CMP_RATIO + 1
    kv_shape = jax.ShapeDtypeStruct((bsz, seq, nkv), BF16)
    cmp_shape = jax.ShapeDtypeStruct((bsz, grp, CMP_STRIDE, n_chunk, NSA_DK), BF16)
    cmp_spec = pl.BlockSpec((1, grp, CMP_STRIDE, tm // CMP_STRIDE, NSA_DK),
                            lambda i, b: (b, 0, 0, i, 0))
    q, kc, vc, ks, vs, kw, vw, gates = pl.pallas_call(
        _nsa_proj_kernel,
        out_shape=[jax.ShapeDtypeStruct((bsz, seq, nq), BF16)] + [cmp_shape] * 2 + [kv_shape] * 4
                  + [jax.ShapeDtypeStruct((bsz, seq, LANE), F32)],
        grid=(seq // tm, bsz),
        in_specs=[pl.BlockSpec((1, tm, d), row),
                  pl.BlockSpec((1, N_MOD, d), lambda i, b: (b, 0, 0)),
                  _const_spec((1, d)),
                  _const_spec(w.shape)],
        out_specs=[pl.BlockSpec((1, tm, nq), row)] + [cmp_spec] * 2
                  + [pl.BlockSpec((1, tm, nkv), row)] * 4 + [pl.BlockSpec((1, tm, LANE), row)],
        scratch_shapes=[pltpu.VMEM((2 * grp, tm, NSA_DK), F32)],
        compiler_params=_cparams(2),
        name="nsa_proj",
    )(x, mod, g1.reshape(1, d), w)

    nb = math.gcd(bsz, 4)
    wide = CMP_BLOCK * NSA_DK
    chunk_spec = pl.BlockSpec((nb, grp, CMP_STRIDE, n_chunk, NSA_DK), lambda i: (i, 0, 0, 0, 0))
    out_spec = pl.BlockSpec((nb, grp, n_chunk, NSA_DK), lambda i: (i, 0, 0, 0))
    kcmp, vcmp = pl.pallas_call(
        _nsa_compress_kernel,
        out_shape=[jax.ShapeDtypeStruct((bsz, grp, n_chunk, NSA_DK), BF16)] * 2,
        grid=(bsz // nb,),
        in_specs=[chunk_spec, chunk_spec,
                  _const_spec((1, wide)), _const_spec((1, wide)),
                  _const_spec((wide, CMP_HIDDEN)), _const_spec((wide, CMP_HIDDEN)),
                  _const_spec((CMP_HIDDEN, NSA_DK)), _const_spec((CMP_HIDDEN, NSA_DV))],
        out_specs=[out_spec, out_spec],
        compiler_params=_cparams(1),
        name="nsa_compress",
    )(kc, vc, pe_k.reshape(1, wide), pe_v.reshape(1, wide), w1_k.astype(BF16), w1_v.astype(BF16),
      w2_k.astype(BF16), w2_v.astype(BF16))

    pos = jnp.arange(seq)
    cmp_start = jnp.arange(n_chunk) * CMP_STRIDE
    dist_c = pos[:, None] - (cmp_start + CMP_BLOCK - 1)[None, :]
    ok_c = (dist_c >= 0) & (jnp.arange(n_chunk) < n_cmp)[None, :]
    bucket_c = jnp.where(ok_c, _t5_bucket(dist_c), -1).astype(jnp.int32)
    n_blk = seq // SLC_BLOCK
    blk = jnp.arange(n_blk)
    blk_start = blk * SLC_BLOCK
    overlap_t = ((cmp_start[None, :] < blk_start[:, None] + SLC_BLOCK)
                 & (cmp_start[None, :] + CMP_BLOCK > blk_start[:, None])
                 & (jnp.arange(n_chunk) < n_cmp)[None, :]).astype(F32)
    back = (pos // SLC_BLOCK)[None, :] - blk[:, None]
    forced = (blk[:, None] == 0) | ((back >= 0) & (back < N_LOCAL))
    causal_blk = blk_start[:, None] <= pos[None, :]
    bonus_t = jnp.where(causal_blk, jnp.where(forced, FORCE_BONUS, 0.0), NEG).astype(F32)

    ocmp, selneg = pl.pallas_call(
        _nsa_select_kernel,
        out_shape=[jax.ShapeDtypeStruct((bsz, seq, nq), BF16),
                   jax.ShapeDtypeStruct((bsz, grp, seq, LANE), BF16)],
        grid=(grp, bsz),
        in_specs=[pl.BlockSpec(memory_space=pltpu.SMEM),
                  pl.BlockSpec((1, seq, hpg * NSA_DK), lambda g, b: (b, 0, g)),
                  pl.BlockSpec((1, 1, n_chunk, NSA_DK), lambda g, b: (b, g, 0, 0)),
                  pl.BlockSpec((1, 1, n_chunk, NSA_DV), lambda g, b: (b, g, 0, 0)),
                  _const_spec(bucket_c.shape),
                  _const_spec(overlap_t.shape),
                  _const_spec(bonus_t.shape)],
        out_specs=[pl.BlockSpec((1, seq, hpg * NSA_DV), lambda g, b: (b, 0, g)),
                   pl.BlockSpec((1, 1, seq, LANE), lambda g, b: (b, g, 0, 0))],
        scratch_shapes=[pltpu.VMEM((hpg, seq, n_chunk), F32)],
        compiler_params=_cparams(2),
        name="nsa_select",
    )(rel_bias, q, kcmp, vcmp, bucket_c, overlap_t, bonus_t)

    t = ATT_TILE
    rr = jnp.arange(t)
    near = jnp.stack([_t5_bucket(k * t + rr[:, None] - rr[None, :]) for k in range(2)]).astype(jnp.int32)
    e_t = (jnp.arange(seq)[:, None] // SLC_BLOCK == jnp.arange(LANE)[None, :]).astype(BF16)
    grp_blk = lambda g, b: (b, 0, g)
    return pl.pallas_call(
        _nsa_attn_kernel,
        out_shape=jax.ShapeDtypeStruct((bsz, seq, nq), BF16),
        grid=(grp, bsz),
        in_specs=[pl.BlockSpec(memory_space=pltpu.SMEM),
                  pl.BlockSpec((1, seq, hpg * NSA_DK), grp_blk),
                  pl.BlockSpec((1, 1, seq, LANE), lambda g, b: (b, g, 0, 0)),
                  pl.BlockSpec((1, seq, NSA_DK), grp_blk),
                  pl.BlockSpec((1, seq, NSA_DV), grp_blk),
                  pl.BlockSpec((1, seq, NSA_DK), grp_blk),
                  pl.BlockSpec((1, seq, NSA_DV), grp_blk),
                  _const_spec(e_t.shape),
                  pl.BlockSpec((1, seq, hpg * NSA_DV), grp_blk),
                  pl.BlockSpec((1, seq, LANE), lambda g, b: (b, 0, 0)),
                  _const_spec(near.shape)],
        out_specs=pl.BlockSpec((1, seq, hpg * NSA_DV), grp_blk),
        scratch_shapes=[pltpu.VMEM((seq, NSA_DK + LANE), BF16),
                        pltpu.VMEM((hpg * t, 3 * t), F32),
                        pltpu.VMEM((1, seq, 2 * NSA_DV), BF16),
                        pltpu.VMEM((1, seq, 2 * NSA_DV), BF16)],
        compiler_params=_cparams(2),
        name="nsa_attn",
    )(rel_bias, q, selneg, ks, vs, kw, vw, e_t, ocmp, gates, near)


def _s5_kernel(x_ref, mod_ref, g1_ref, perm_ref, permt_ref, bm_ref, cm_ref, are_ref, aim_ref,
               dskip_ref, y_ref, bu_ref, state_ref):
    bb, ts, d = x_ref.shape
    rows = bb * ts
    n_slab = d // LANE
    wid = S5_SLAB * S5_STATE

    @pl.when(pl.program_id(1) == 0)
    def _():
        state_ref[...] = jnp.zeros_like(state_ref)

    mod = mod_ref[...]
    u = (_rms(x_ref[...], g1_ref[...]) * (1.0 + mod[:, 1:2, :]) + mod[:, 0:1, :]).reshape(rows, d)
    u_tb = _dot(perm_ref[...], u.astype(BF16)).astype(BF16)
    def project_in(k):
        bu_ref[:, 2 * k * wid:2 * (k + 1) * wid] = _dot(u_tb[:, k * LANE:(k + 1) * LANE], bm_ref[k])

    def scan(k):
        c_re, c_im = 2 * k * wid, (2 * k + 1) * wid
        a_re = jnp.broadcast_to(are_ref[k], (bb, wid))
        a_im = jnp.broadcast_to(aim_ref[k], (bb, wid))
        xr, xi = state_ref[:, c_re:c_re + wid], state_ref[:, c_im:c_im + wid]
        for t in range(ts):
            r0 = t * bb
            nr = a_re * xr - a_im * xi + bu_ref[r0:r0 + bb, c_re:c_re + wid]
            ni = a_re * xi + a_im * xr + bu_ref[r0:r0 + bb, c_im:c_im + wid]
            bu_ref[r0:r0 + bb, c_re:c_re + wid] = nr
            bu_ref[r0:r0 + bb, c_im:c_im + wid] = ni
            xr, xi = nr, ni
        state_ref[:, c_re:c_re + wid] = xr
        state_ref[:, c_im:c_im + wid] = xi

    def project_out(k):
        return _dot(bu_ref[:, 2 * k * wid:2 * (k + 1) * wid].astype(BF16), cm_ref[k]).astype(BF16)

    ys = []
    project_in(0)
    for k in range(n_slab):
        if k + 1 < n_slab:
            project_in(k + 1)
        scan(k)
        ys.append(project_out(k))
    y_tb = jnp.concatenate(ys, axis=1)
    y = _dot(permt_ref[...], y_tb) + dskip_ref[...] * u
    y_ref[...] = _gelu_tanh(y).astype(BF16).reshape(bb, ts, d)


def _s5_mixer(x, mod, g1, lam_re, lam_im, log_dt, b_re, b_im, c_re, c_im, d_skip):
    bsz, seq, d = x.shape
    n_slab = d // LANE
    dt = jnp.exp(log_dt)[:, None]
    mag = jnp.exp(lam_re * dt)
    lb_re = mag * jnp.cos(lam_im * dt)
    lb_im = mag * jnp.sin(lam_im * dt)
    den = lam_re * lam_re + lam_im * lam_im
    f_re = ((lb_re - 1.0) * lam_re + lb_im * lam_im) / den
    f_im = (lb_im * lam_re - (lb_re - 1.0) * lam_im) / den
    bb_re = f_re[..., None] * b_re - f_im[..., None] * b_im
    bb_im = f_re[..., None] * b_im + f_im[..., None] * b_re
    eye = jnp.eye(S5_SLAB, dtype=F32)
    wid = S5_SLAB * S5_STATE

    def in_mat(b):
        b = b.reshape(n_slab, S5_SLAB, S5_STATE, S5_GROUP)
        return jnp.einsum('kgpi,gh->kgihp', b, eye).reshape(n_slab, LANE, wid)

    def out_mat(c):
        c = c.reshape(n_slab, S5_SLAB, S5_GROUP, S5_STATE)
        return jnp.einsum('kgip,gh->kgphi', c, eye).reshape(n_slab, wid, LANE)

    bm = jnp.concatenate([in_mat(bb_re), in_mat(bb_im)], axis=2).astype(BF16)
    cm = jnp.concatenate([out_mat(c_re), -out_mat(c_im)], axis=1).astype(BF16)
    a_re = lb_re.reshape(n_slab, 1, wid)
    a_im = lb_im.reshape(n_slab, 1, wid)

    bb, ts = S5_BB, 64
    rows = bb * ts
    r = jnp.arange(rows)
    perm = ((r % bb)[:, None] * ts + (r // bb)[:, None] == r[None, :]).astype(BF16)
    blk = lambda i, t: (i, t, 0)
    return pl.pallas_call(
        _s5_kernel,
        out_shape=jax.ShapeDtypeStruct((bsz, seq, d), BF16),
        grid=(bsz // bb, seq // ts),
        in_specs=[pl.BlockSpec((bb, ts, d), blk),
                  pl.BlockSpec((bb, N_MOD, d), lambda i, t: (i, 0, 0)),
                  _const_spec((1, d)),
                  _const_spec((rows, rows)), _const_spec((rows, rows)),
                  _const_spec(bm.shape), _const_spec(cm.shape),
                  _const_spec(a_re.shape), _const_spec(a_im.shape),
                  _const_spec((1, d))],
        out_specs=pl.BlockSpec((bb, ts, d), blk),
        scratch_shapes=[pltpu.VMEM((rows, 2 * wid * n_slab), F32),
                        pltpu.VMEM((bb, 2 * wid * n_slab), F32)],
        compiler_params=_cparams(2),
        name="s5_scan",
    )(x, mod, g1.reshape(1, d), perm, perm.T, bm, cm, a_re, a_im, d_skip.reshape(1, d))


def kernel(x, c, ada_w, ada_b, norm1_g, norm2_g, final_g, rel_bias, mla_w_dkv, mla_g_q, mla_g_kv, mla_w_uq, mla_w_ukv, mla_w_o, nsa_w_in, nsa_pe_k, nsa_w1_k, nsa_w2_k, nsa_pe_v, nsa_w1_v, nsa_w2_v, nsa_w_o, s5_lam_re, s5_lam_im, s5_log_dt, s5_b_re, s5_b_im, s5_c_re, s5_c_im, s5_d, s5_w_glu, ffn_w_up, ffn_conv_w, ffn_conv_b, ffn_w_down):
    depth = ada_w.shape[0]
    mods = _ada_mod(c, ada_w, ada_b)
    for i in range(depth):
        mod = mods[i]
        kind, j = i % N_MIXERS, i // N_MIXERS
        if kind == 0:
            o = _mla_mixer(x, mod, norm1_g[i], mla_w_dkv[j], mla_g_q[j], mla_g_kv[j],
                           mla_w_uq[j], mla_w_ukv[j])
            w_post, glu = mla_w_o[j], False
        elif kind == 1:
            o = _nsa_mixer(x, mod, norm1_g[i], nsa_w_in[j], nsa_pe_k[j], nsa_w1_k[j], nsa_w2_k[j],
                           nsa_pe_v[j], nsa_w1_v[j], nsa_w2_v[j], rel_bias)
            w_post, glu = nsa_w_o[j], False
        else:
            o = _s5_mixer(x, mod, norm1_g[i], s5_lam_re[j], s5_lam_im[j], s5_log_dt[j],
                          s5_b_re[j], s5_b_im[j], s5_c_re[j], s5_c_im[j], s5_d[j])
            w_post, glu = s5_w_glu[j], True
        x = _post_ffn(x, o, mod, w_post.astype(BF16), norm2_g[i], ffn_w_up[i].astype(BF16),
                      ffn_conv_w[i], ffn_conv_b[i], ffn_w_down[i].astype(BF16), final_g,
                      glu=glu, final=(i == depth - 1))
    return x
```

```python
import functools
import math

import jax
import jax.numpy as jnp
from jax import lax
from jax.experimental import pallas as pl
from jax.experimental.pallas import tpu as pltpu

F32 = jnp.float32
BF16 = jnp.bfloat16

NORM_EPS = 1e-6
N_MOD = 6
N_MIXERS = 3

MLA_HEADS = 8
MLA_Q_LORA = 512
MLA_KV_LORA = 256
MLA_NOPE = 128
MLA_ROPE = 64
MLA_V = 128
MLA_QK = MLA_NOPE + MLA_ROPE
ROPE_THETA = 10000.0

NSA_HEADS = 8
NSA_GROUPS = 2
NSA_HPG = NSA_HEADS // NSA_GROUPS
NSA_DK = 128
NSA_DV = 128
CMP_BLOCK = 32
CMP_STRIDE = 16
CMP_RATIO = CMP_BLOCK // CMP_STRIDE
CMP_HIDDEN = 256
SLC_BLOCK = 64
SLC_TOPN = 16
N_LOCAL = 2
FORCE_BONUS = 1e4
WINDOW = 512
REL_BUCKETS = 32
REL_MAX_DIST = 128

S5_GROUP = 16
S5_STATE = 64
CONV_WIDTH = 3

LANE = 128
SUBLANE = 8
MXU_WIDTH = 256
V7X_VMEM_BYTES = 64 * 1024 * 1024
VMEM_LIMIT = V7X_VMEM_BYTES - 8 * 1024 * 1024
NEG = -1e30
LOG2E = math.log2(math.e)

TOKEN_TILE = 512
FFN_CHUNK = MXU_WIDTH
ADA_COLS = 1024
ATT_TILE = 128
MLA_TQ = 256
CMP_BATCH = 4
S5_BB = SUBLANE
S5_TS = 64
S5_SLAB = LANE // S5_GROUP


def _cparams(n_axes):
    return pltpu.CompilerParams(
        dimension_semantics=("arbitrary",) * n_axes,
        vmem_limit_bytes=VMEM_LIMIT)


def _const_spec(shape):
    zeros = (0,) * len(shape)
    return pl.BlockSpec(shape, lambda *_: zeros, pipeline_mode=pl.Buffered(1))


def _dot(a, b):
    return jnp.dot(a, b, preferred_element_type=F32)


def _dot_nt(a, b):
    return lax.dot_general(a, b, (((1,), (1,)), ((), ())), preferred_element_type=F32)


def _rms(x, g):
    return x * lax.rsqrt(jnp.mean(x * x, axis=-1, keepdims=True) + NORM_EPS) * g


def _norm_mod(x, g, shift, scale):
    return _rms(x, g) * (1.0 + scale) + shift


def _gelu_tanh(x):
    c = math.sqrt(2.0 / math.pi)
    return 0.5 * x * (1.0 + jnp.tanh(c * (x + 0.044715 * (x * x * x))))


def _sigmoid(x):
    return 1.0 / (1.0 + jnp.exp(-x))


def _softmax_pv(scores, values):
    m = functools.reduce(jnp.maximum, [jnp.max(s, axis=1, keepdims=True) for s in scores])
    acc = functools.reduce(
        jnp.add, [_dot(jnp.exp2(s - m).astype(BF16), v) for s, v in zip(scores, values)])
    width = acc.shape[1] // 2
    return acc[:, :width] / acc[:, width:]


def _t5_bucket(dist):
    n = jnp.maximum(dist, 0)
    max_exact = REL_BUCKETS // 2
    nf = jnp.maximum(n, 1).astype(F32)
    large = max_exact + (jnp.log(nf / max_exact) / math.log(REL_MAX_DIST / max_exact)
                         * (REL_BUCKETS - max_exact)).astype(jnp.int32)
    large = jnp.minimum(large, REL_BUCKETS - 1)
    return jnp.where(n < max_exact, n, large)


def _ada_kernel(c_ref, w_ref, b_ref, o_ref):
    c = c_ref[...]
    c_act = c * _sigmoid(c)
    o_ref[0] = jnp.dot(c_act, w_ref[0], preferred_element_type=F32,
                       precision=lax.Precision.HIGHEST) + b_ref[0]


def _ada_mod(c, ada_w, ada_b):
    depth, d, n = ada_w.shape
    bsz = c.shape[0]
    tn = ADA_COLS
    out = pl.pallas_call(
        _ada_kernel,
        out_shape=jax.ShapeDtypeStruct((depth, bsz, n), F32),
        grid=(depth, n // tn),
        in_specs=[pl.BlockSpec((bsz, d), lambda i, j: (0, 0)),
                  pl.BlockSpec((1, d, tn), lambda i, j: (i, 0, j)),
                  pl.BlockSpec((1, 1, tn), lambda i, j: (i, 0, j))],
        out_specs=pl.BlockSpec((1, bsz, tn), lambda i, j: (i, 0, j)),
        compiler_params=_cparams(2),
        name="ada_mod",
    )(c, ada_w, ada_b.reshape(depth, 1, n))
    return out.reshape(depth, bsz, N_MOD, d)


def _post_ffn_kernel(x_ref, o_ref, mod_ref, wpost_ref, g2_ref, wup_ref, cw_ref, cb_ref,
                     wdown_ref, gf_ref, out_ref, act_ref, carry_ref,
                     *, glu, final, fc):
    tm = x_ref.shape[1]
    d_ff = wdown_ref.shape[0]

    @pl.when(pl.program_id(1) == 0)
    def _():
        carry_ref[...] = jnp.zeros_like(carry_ref)

    x = x_ref[0]
    mod = mod_ref[0]
    y = _dot(o_ref[0], wpost_ref[...])
    if glu:
        d = x.shape[-1]
        y = y[:, :d] * _sigmoid(y[:, d:])
    x1 = x + mod[2:3] * y
    h = _norm_mod(x1, g2_ref[...], mod[3:4], mod[4:5]).astype(BF16)

    sub = SUBLANE
    row = lax.broadcasted_iota(jnp.int32, (sub, fc), 0)

    def shifted(z, c, k):
        r = pltpu.roll(z, k, 0)
        head = jnp.where(row < k, pltpu.roll(c, k, 0), r[:sub])
        return jnp.concatenate([head, r[sub:]], axis=0)

    def conv(z, col):
        c = carry_ref[:, col:col + fc]
        carry_ref[:, col:col + fc] = z[tm - sub:, :]
        w = cw_ref[:, col:col + fc]
        return (w[0:1] * shifted(z, c, 2) + w[1:2] * shifted(z, c, 1) + w[2:3] * z
                + cb_ref[:, col:col + fc])

    for f in range(0, d_ff, fc):
        val = conv(_dot(h, wup_ref[:, f:f + fc]), f)
        gate = conv(_dot(h, wup_ref[:, d_ff + f:d_ff + f + fc]), d_ff + f)
        act_ref[:, f:f + fc] = (val * (gate * _sigmoid(gate))).astype(BF16)

    x2 = x1 + mod[5:6] * _dot(act_ref[...], wdown_ref[...])
    if final:
        x2 = _rms(x2, gf_ref[...])
    out_ref[0] = x2


def _post_ffn(x, o, mod, w_post, g2, w_up, conv_w, conv_b, w_down, final_g, *, glu, final):
    bsz, seq, d = x.shape
    d_ff = w_down.shape[0]
    tm = TOKEN_TILE
    fc = FFN_CHUNK
    row = lambda b, i: (b, i, 0)
    return pl.pallas_call(
        functools.partial(_post_ffn_kernel, glu=glu, final=final, fc=fc),
        out_shape=jax.ShapeDtypeStruct((bsz, seq, d), F32),
        grid=(bsz, seq // tm),
        in_specs=[pl.BlockSpec((1, tm, d), row),
                  pl.BlockSpec((1, tm, d), row),
                  pl.BlockSpec((1, N_MOD, d), lambda b, i: (b, 0, 0)),
                  _const_spec(w_post.shape),
                  _const_spec((1, d)),
                  _const_spec(w_up.shape),
                  _const_spec(conv_w.shape),
                  _const_spec((1, 2 * d_ff)),
                  _const_spec(w_down.shape),
                  _const_spec((1, d))],
        out_specs=pl.BlockSpec((1, tm, d), row),
        scratch_shapes=[pltpu.VMEM((tm, d_ff), BF16),
                        pltpu.VMEM((SUBLANE, 2 * d_ff), F32)],
        compiler_params=_cparams(2),
        name="post_ffn",
    )(x, o, mod, w_post, g2.reshape(1, d), w_up, conv_w, conv_b.reshape(1, -1), w_down,
      final_g.reshape(1, d))


def _rope_tables(seq, reps):
    half = MLA_ROPE // 2
    inv = ROPE_THETA ** (-jnp.arange(half, dtype=F32) / half)
    ang = jnp.arange(seq).astype(F32)[:, None] * inv[None, :]
    cos, sin = jnp.cos(ang), jnp.sin(ang)
    cos64 = jnp.concatenate([cos, cos], axis=1)
    sin64 = jnp.concatenate([-sin, sin], axis=1)
    return jnp.tile(cos64, (1, reps)), jnp.tile(sin64, (1, reps))


def _mla_proj_kernel(x_ref, mod_ref, g1_ref, wdkv_ref, gq_ref, gkv_ref, wuq_ref, wukv_ref,
                     cos_ref, sin_ref, qn_ref, qr_ref, kn_ref, kr_ref, v_ref):
    nq = MLA_HEADS * MLA_NOPE
    nr = MLA_HEADS * MLA_ROPE
    scale = MLA_QK ** -0.5 * LOG2E
    mod = mod_ref[0]
    h = _norm_mod(x_ref[0], g1_ref[...], mod[0:1], mod[1:2]).astype(BF16)
    dkv = _dot(h, wdkv_ref[...])
    c_q = _rms(dkv[:, :MLA_Q_LORA], gq_ref[...]).astype(BF16)
    c_kv = _rms(dkv[:, MLA_Q_LORA:MLA_Q_LORA + MLA_KV_LORA], gkv_ref[...]).astype(BF16)
    cos = cos_ref[...]
    sin = sin_ref[...]
    r0 = MLA_Q_LORA + MLA_KV_LORA
    kr_ref[0] = (dkv[:, r0:r0 + LANE] * cos[:, :LANE]
                 + dkv[:, r0 + LANE:r0 + 2 * LANE] * sin[:, :LANE]).astype(BF16)
    q = _dot(c_q, wuq_ref[...])
    qn_ref[0] = (q[:, :nq] * scale).astype(BF16)
    qr_ref[0] = ((q[:, nq:nq + nr] * cos + q[:, nq + nr:] * sin) * scale).astype(BF16)
    kv = _dot(c_kv, wukv_ref[...])
    kn_ref[0] = kv[:, :nq].astype(BF16)
    v_ref[0] = kv[:, nq:].astype(BF16)


def _mla_attn_kernel(qn_ref, qr_ref, kn_ref, kr_ref, vin_ref, o_ref, v_ref):
    seq = qn_ref.shape[1]
    tq = MLA_TQ
    v_ref[0, :, :MLA_V] = vin_ref[0]
    v_ref[0, :, MLA_V:] = jnp.ones((seq, MLA_V), BF16)
    odd = pl.program_id(1) % 2
    lane = lax.broadcasted_iota(jnp.int32, (tq, LANE), 1)
    keep = (lane >= MLA_ROPE) == (odd == 1)
    row = lax.broadcasted_iota(jnp.int32, (tq, tq), 0)
    col = lax.broadcasted_iota(jnp.int32, (tq, tq), 1)
    def scores(i):
        lo, hi = i * tq, (i + 1) * tq
        qr = jnp.where(keep, qr_ref[0, lo:hi, :], jnp.zeros((), BF16))
        q = jnp.concatenate([qn_ref[0, lo:hi, :], qr], axis=1)
        k_d = jnp.concatenate([kn_ref[0, lo:hi, :], kr_ref[0, lo:hi, :]], axis=1)
        out = [jnp.where(col <= row, _dot_nt(q, k_d), NEG)]
        if i > 0:
            k_o = jnp.concatenate([kn_ref[0, :lo, :], kr_ref[0, :lo, :]], axis=1)
            out.append(_dot_nt(q, k_o))
        return out

    n_tile = seq // tq
    nxt = scores(0)
    for i in range(n_tile):
        lo, hi = i * tq, (i + 1) * tq
        cur = nxt
        if i + 1 < n_tile:
            nxt = scores(i + 1)
        values = [v_ref[0, lo:hi, :]] + ([v_ref[0, :lo, :]] if i > 0 else [])
        o_ref[0, lo:hi, :] = _softmax_pv(cur, values).astype(BF16)


def _mla_mixer(x, mod, g1, w_dkv, g_q, g_kv, w_uq, w_ukv):
    bsz, seq, d = x.shape
    hds = MLA_HEADS
    half = MLA_ROPE // 2
    swap = jnp.concatenate([jnp.arange(half, MLA_ROPE), jnp.arange(half)])
    r0 = MLA_Q_LORA + MLA_KV_LORA
    kr = w_dkv[:, r0:]
    wdkv = jnp.concatenate([w_dkv[:, :r0], kr, kr, kr[:, swap], kr[:, swap]], axis=1).astype(BF16)
    wq = w_uq.reshape(MLA_Q_LORA, hds, MLA_QK)
    wq_r = wq[:, :, MLA_NOPE:]
    wuq = jnp.concatenate([wq[:, :, :MLA_NOPE].reshape(MLA_Q_LORA, -1),
                           wq_r.reshape(MLA_Q_LORA, -1),
                           wq_r[:, :, swap].reshape(MLA_Q_LORA, -1)], axis=1).astype(BF16)
    wkv = w_ukv.reshape(MLA_KV_LORA, hds, MLA_NOPE + MLA_V)
    wukv = jnp.concatenate([wkv[:, :, :MLA_NOPE].reshape(MLA_KV_LORA, -1),
                            wkv[:, :, MLA_NOPE:].reshape(MLA_KV_LORA, -1)], axis=1).astype(BF16)
    cos, sin = _rope_tables(seq, hds)
    nq, nr = hds * MLA_NOPE, hds * MLA_ROPE
    tm = TOKEN_TILE
    row = lambda i, b: (b, i, 0)
    qn, qr, kn, kr2, v = pl.pallas_call(
        _mla_proj_kernel,
        out_shape=[jax.ShapeDtypeStruct((bsz, seq, nq), BF16),
                   jax.ShapeDtypeStruct((bsz, seq, nr), BF16),
                   jax.ShapeDtypeStruct((bsz, seq, nq), BF16),
                   jax.ShapeDtypeStruct((bsz, seq, LANE), BF16),
                   jax.ShapeDtypeStruct((bsz, seq, nq), BF16)],
        grid=(seq // tm, bsz),
        in_specs=[pl.BlockSpec((1, tm, d), row),
                  pl.BlockSpec((1, N_MOD, d), lambda i, b: (b, 0, 0)),
                  _const_spec((1, d)),
                  _const_spec(wdkv.shape),
                  _const_spec((1, MLA_Q_LORA)),
                  _const_spec((1, MLA_KV_LORA)),
                  _const_spec(wuq.shape),
                  _const_spec(wukv.shape),
                  pl.BlockSpec((tm, nr), lambda i, b: (i, 0)),
                  pl.BlockSpec((tm, nr), lambda i, b: (i, 0))],
        out_specs=[pl.BlockSpec((1, tm, nq), row),
                   pl.BlockSpec((1, tm, nr), row),
                   pl.BlockSpec((1, tm, nq), row),
                   pl.BlockSpec((1, tm, LANE), row),
                   pl.BlockSpec((1, tm, nq), row)],
        compiler_params=_cparams(2),
        name="mla_proj",
    )(x, mod, g1.reshape(1, d), wdkv, g_q.reshape(1, -1), g_kv.reshape(1, -1), wuq, wukv, cos, sin)

    head = lambda b, h: (b, 0, h)
    return pl.pallas_call(
        _mla_attn_kernel,
        out_shape=jax.ShapeDtypeStruct((bsz, seq, hds * MLA_V), BF16),
        grid=(bsz, hds),
        in_specs=[pl.BlockSpec((1, seq, LANE), head),
                  pl.BlockSpec((1, seq, LANE), lambda b, h: (b, 0, h // 2)),
                  pl.BlockSpec((1, seq, LANE), head),
                  pl.BlockSpec((1, seq, LANE), lambda b, h: (b, 0, 0)),
                  pl.BlockSpec((1, seq, LANE), head)],
        out_specs=pl.BlockSpec((1, seq, LANE), head),
        scratch_shapes=[pltpu.VMEM((1, seq, 2 * MLA_V), BF16)],
        compiler_params=_cparams(2),
        name="mla_attn",
    )(qn, qr, kn, kr2, v)


def _nsa_proj_kernel(x_ref, mod_ref, g1_ref, w_ref, q_ref, kc_ref, vc_ref, ks_ref, vs_ref,
                     kw_ref, vw_ref, gate_ref, cmp_ref):
    nq = NSA_HEADS * NSA_DK
    nkv = NSA_GROUPS * NSA_DK
    tm = x_ref.shape[1]
    mod = mod_ref[0]
    h = _norm_mod(x_ref[0], g1_ref[...], mod[0:1], mod[1:2]).astype(BF16)
    p = _dot(h, w_ref[...])
    q_ref[0] = (p[:, :nq] * (NSA_DK ** -0.5 * LOG2E)).astype(BF16)
    for n, ref in enumerate((ks_ref, vs_ref, kw_ref, vw_ref)):
        ref[0] = p[:, nq + (n + 2) * nkv:nq + (n + 3) * nkv].astype(BF16)
    gate_ref[0] = _sigmoid(p[:, nq + 6 * nkv:])
    for n, ref in enumerate((kc_ref, vc_ref)):
        for g in range(NSA_GROUPS):
            c0 = nq + n * nkv + g * NSA_DK
            stage = cmp_ref.at[n * NSA_GROUPS + g]
            stage[...] = p[:, c0:c0 + NSA_DK]
            for r in range(CMP_STRIDE):
                ref[0, g, r] = stage[pl.ds(r, tm // CMP_STRIDE, stride=CMP_STRIDE), :].astype(BF16)


def _nsa_compress_kernel(ck_ref, cv_ref, pek_ref, pev_ref, w1k_ref, w1v_ref, w2k_ref, w2v_ref,
                         ok_ref, ov_ref):
    nb, grp, _, n_chunk, dk = ck_ref.shape
    half = CMP_STRIDE * dk
    rows = nb * grp * n_chunk
    for c_ref, pe_ref, w1_ref, w2_ref, o_ref in ((ck_ref, pek_ref, w1k_ref, w2k_ref, ok_ref),
                                                 (cv_ref, pev_ref, w1v_ref, w2v_ref, ov_ref)):
        chunks = jnp.concatenate(
            [jnp.concatenate([c_ref[b, g, r] for r in range(CMP_STRIDE)], axis=1)
             for b in range(nb) for g in range(grp)], axis=0)
        first = _dot(chunks, w1_ref[:half, :])
        second = _dot(chunks, w1_ref[half:, :])
        pe = jnp.broadcast_to(pe_ref[...], (SUBLANE, 2 * half)).astype(BF16)
        const = _dot(pe, w1_ref[...])[0:1]
        hid = _gelu_tanh(first + pltpu.roll(second, rows - 1, 0) + const).astype(BF16)
        o_ref[...] = _dot(hid, w2_ref[...]).astype(BF16).reshape(nb, grp, n_chunk, dk)


def _nsa_select_kernel(rb_ref, q_ref, kc_ref, vc_ref, bucket_ref, ovl_ref, bonus_ref,
                       ocmp_ref, sel_ref, bias_ref):
    g = pl.program_id(0)
    seq = q_ref.shape[1]
    n_blk = bonus_ref.shape[0]
    bucket = bucket_ref[...]

    @pl.when(pl.program_id(1) == 0)
    def _():
        for hp in range(NSA_HPG):
            acc = jnp.zeros(bucket.shape, F32)
            for r in range(REL_BUCKETS):
                acc = jnp.where(bucket == r, rb_ref[g * NSA_HPG + hp, r] * LOG2E, acc)
            bias_ref[hp] = acc

    valid = bucket >= 0
    kc = kc_ref[0, 0]
    vc = vc_ref[0, 0]
    psum = jnp.zeros(bucket.shape, F32)
    for hp in range(NSA_HPG):
        s = _dot_nt(q_ref[0, :, hp * NSA_DK:(hp + 1) * NSA_DK], kc) + bias_ref[hp]
        s = jnp.where(valid, s, NEG)
        m = jnp.max(s, axis=1, keepdims=True)
        m = jnp.where(m > 0.5 * NEG, m, 0.0)
        p = jnp.exp2(s - m)
        p = p / jnp.maximum(jnp.sum(p, axis=1, keepdims=True), 1e-30)
        ocmp_ref[0, :, hp * NSA_DV:(hp + 1) * NSA_DV] = _dot(p.astype(BF16), vc).astype(BF16)
        psum = psum + p

    imp = lax.dot_general(ovl_ref[...], psum, (((1,), (1,)), ((), ())),
                          preferred_element_type=F32, precision=lax.Precision.HIGHEST)
    bonus = bonus_ref[...]
    score = jnp.where(bonus > 0.5 * NEG, imp + bonus, NEG)
    blk = lax.broadcasted_iota(jnp.int32, (n_blk, seq), 0)
    rank = jnp.zeros((n_blk, seq), F32)
    for i in range(n_blk):
        row = score[i:i + 1, :]
        rank = rank + jnp.where(blk > i, jnp.where(row >= score, 1.0, 0.0),
                                jnp.where(row > score, 1.0, 0.0))
    selneg = jnp.where(rank < float(SLC_TOPN), jnp.where(score > 0.5 * NEG, 0.0, NEG), NEG)
    pad = jnp.zeros((LANE - n_blk, seq), F32)
    sel_ref[0, 0] = jnp.concatenate([selneg, pad], axis=0).T.astype(BF16)


def _nsa_attn_kernel(rb_ref, q_ref, sel_ref, ks_ref, vsin_ref, kw_ref, vwin_ref, et_ref, ocmp_ref,
                     gate_ref, bkt_ref, o_ref, kaug_ref, bias_ref, vs_ref, vw_ref):
    g = pl.program_id(0)
    t = ATT_TILE
    hp_n = NSA_HPG
    seq = q_ref.shape[1]

    @pl.when(pl.program_id(1) == 0)
    def _():
        kaug_ref[:, NSA_DK:] = et_ref[...]
        vs_ref[0, :, NSA_DV:] = jnp.ones((seq, NSA_DV), BF16)
        vw_ref[0, :, NSA_DV:] = jnp.ones((seq, NSA_DV), BF16)
        for kind in range(2):
            bkt = bkt_ref[kind]
            for hp in range(hp_n):
                hd = g * hp_n + hp
                acc = jnp.zeros((t, t), F32)
                for r in range(REL_BUCKETS - 1):
                    acc = jnp.where(
                        bkt == r, (rb_ref[hd, r] - rb_ref[hd, REL_BUCKETS - 1]) * LOG2E, acc)
                bias_ref[hp * t:(hp + 1) * t, (1 - kind) * t:(2 - kind) * t] = acc
        bias_ref[:, 2 * t:] = jnp.zeros((hp_n * t, t), F32)

    kaug_ref[:, :NSA_DK] = ks_ref[0]
    vs_ref[0, :, :NSA_DV] = vsin_ref[0]
    vw_ref[0, :, :NSA_DV] = vwin_ref[0]

    rows = hp_n * t
    r2 = lax.broadcasted_iota(jnp.int32, (rows, 2 * t), 0) & (t - 1)
    c2 = lax.broadcasted_iota(jnp.int32, (rows, 2 * t), 1)
    r1 = lax.broadcasted_iota(jnp.int32, (rows, t), 0) & (t - 1)
    c1 = lax.broadcasted_iota(jnp.int32, (rows, t), 1)
    near_ok = c2 <= r2 + t
    first_ok = c2 <= r2
    edge_ok = c1 > r1
    lane = lax.broadcasted_iota(jnp.int32, (t, LANE), 1)

    def near(i):
        if i == 0:
            return 0, 2 * t, bias_ref[:, t:], first_ok
        return (i - 1) * t, (i + 1) * t, bias_ref[:, :2 * t], near_ok

    def scores_sel(i):
        lo, hi = i * t, (i + 1) * t
        n_lo, n_hi, bias, ok = near(i)
        sel = sel_ref[0, 0, lo:hi, :]
        q_sel = jnp.concatenate(
            [jnp.concatenate([q_ref[0, lo:hi, hp * NSA_DK:(hp + 1) * NSA_DK], sel], axis=1)
             for hp in range(hp_n)], axis=0)
        out = [jnp.where(ok, _dot_nt(q_sel, kaug_ref[n_lo:n_hi, :]) + bias, NEG)]
        if n_lo > 0:
            out.append(_dot_nt(q_sel, kaug_ref[:n_lo, :]))
        return out

    def scores_win(i):
        lo, hi = i * t, (i + 1) * t
        n_lo, n_hi, bias, ok = near(i)
        q_win = jnp.concatenate(
            [q_ref[0, lo:hi, hp * NSA_DK:(hp + 1) * NSA_DK] for hp in range(hp_n)], axis=0)
        out = [jnp.where(ok, _dot_nt(q_win, kw_ref[0, n_lo:n_hi, :]) + bias, NEG)]
        w_lo = max(lo - WINDOW, 0)
        if n_lo > w_lo:
            s_m = _dot_nt(q_win, kw_ref[0, w_lo:n_lo, :])
            if lo - WINDOW >= 0:
                s_m = jnp.concatenate([jnp.where(edge_ok, s_m[:, :t], NEG), s_m[:, t:]], axis=1)
            out.append(s_m)
        return out

    def pieces(v_ref, i, far_lo):
        n_lo, n_hi, _, _ = near(i)
        return [v_ref[0, n_lo:n_hi, :]] + ([v_ref[0, far_lo:n_lo, :]] if n_lo > far_lo else [])

    n_tile = seq // t
    nxt_sel = scores_sel(0)
    for i in range(n_tile):
        lo, hi = i * t, (i + 1) * t
        cur_sel = nxt_sel
        cur_win = scores_win(i)
        o_slc = _softmax_pv(cur_sel, pieces(vs_ref, i, 0))
        if i + 1 < n_tile:
            nxt_sel = scores_sel(i + 1)
        o_win = _softmax_pv(cur_win, pieces(vw_ref, i, max(lo - WINDOW, 0)))

        gate = gate_ref[0, lo:hi, :]
        for hp in range(hp_n):
            hd = g * hp_n + hp

            def gcol(branch):
                return jnp.sum(jnp.where(lane == branch * NSA_HEADS + hd, gate, 0.0),
                               axis=1, keepdims=True)
            sl = slice(hp * t, (hp + 1) * t)
            o = (gcol(0) * ocmp_ref[0, lo:hi, hp * NSA_DV:(hp + 1) * NSA_DV].astype(F32)
                 + gcol(1) * o_slc[sl] + gcol(2) * o_win[sl])
            o_ref[0, lo:hi, hp * NSA_DV:(hp + 1) * NSA_DV] = o.astype(BF16)


def _nsa_mixer(x, mod, g1, w_in, pe_k, w1_k, w2_k, pe_v, w1_v, w2_v, rel_bias):
    bsz, seq, d = x.shape
    grp, hpg = NSA_GROUPS, NSA_HPG
    nq, nkv = NSA_HEADS * NSA_DK, NSA_GROUPS * NSA_DK
    n_gate = 3 * NSA_HEADS
    w = jnp.pad(w_in, ((0, 0), (0, LANE - n_gate))).astype(BF16)
    tm = TOKEN_TILE
    row = lambda i, b: (b, i, 0)
    assert CMP_RATIO == 2 and NSA_DK == NSA_DV
    n_chunk = seq // CMP_STRIDE
    n_cmp = n_chunk - CMP_RATIO + 1
    kv_shape = jax.ShapeDtypeStruct((bsz, seq, nkv), BF16)
    cmp_shape = jax.ShapeDtypeStruct((bsz, grp, CMP_STRIDE, n_chunk, NSA_DK), BF16)
    cmp_spec = pl.BlockSpec((1, grp, CMP_STRIDE, tm // CMP_STRIDE, NSA_DK),
                            lambda i, b: (b, 0, 0, i, 0))
    q, kc, vc, ks, vs, kw, vw, gates = pl.pallas_call(
        _nsa_proj_kernel,
        out_shape=[jax.ShapeDtypeStruct((bsz, seq, nq), BF16)] + [cmp_shape] * 2 + [kv_shape] * 4
                  + [jax.ShapeDtypeStruct((bsz, seq, LANE), F32)],
        grid=(seq // tm, bsz),
        in_specs=[pl.BlockSpec((1, tm, d), row),
                  pl.BlockSpec((1, N_MOD, d), lambda i, b: (b, 0, 0)),
                  _const_spec((1, d)),
                  _const_spec(w.shape)],
        out_specs=[pl.BlockSpec((1, tm, nq), row)] + [cmp_spec] * 2
                  + [pl.BlockSpec((1, tm, nkv), row)] * 4 + [pl.BlockSpec((1, tm, LANE), row)],
        scratch_shapes=[pltpu.VMEM((2 * grp, tm, NSA_DK), F32)],
        compiler_params=_cparams(2),
        name="nsa_proj",
    )(x, mod, g1.reshape(1, d), w)

    nb = math.gcd(bsz, CMP_BATCH)
    wide = CMP_BLOCK * NSA_DK
    chunk_spec = pl.BlockSpec((nb, grp, CMP_STRIDE, n_chunk, NSA_DK), lambda i: (i, 0, 0, 0, 0))
    out_spec = pl.BlockSpec((nb, grp, n_chunk, NSA_DK), lambda i: (i, 0, 0, 0))
    kcmp, vcmp = pl.pallas_call(
        _nsa_compress_kernel,
        out_shape=[jax.ShapeDtypeStruct((bsz, grp, n_chunk, NSA_DK), BF16)] * 2,
        grid=(bsz // nb,),
        in_specs=[chunk_spec, chunk_spec,
                  _const_spec((1, wide)), _const_spec((1, wide)),
                  _const_spec((wide, CMP_HIDDEN)), _const_spec((wide, CMP_HIDDEN)),
                  _const_spec((CMP_HIDDEN, NSA_DK)), _const_spec((CMP_HIDDEN, NSA_DV))],
        out_specs=[out_spec, out_spec],
        compiler_params=_cparams(1),
        name="nsa_compress",
    )(kc, vc, pe_k.reshape(1, wide), pe_v.reshape(1, wide), w1_k.astype(BF16), w1_v.astype(BF16),
      w2_k.astype(BF16), w2_v.astype(BF16))

    pos = jnp.arange(seq)
    cmp_start = jnp.arange(n_chunk) * CMP_STRIDE
    dist_c = pos[:, None] - (cmp_start + CMP_BLOCK - 1)[None, :]
    ok_c = (dist_c >= 0) & (jnp.arange(n_chunk) < n_cmp)[None, :]
    bucket_c = jnp.where(ok_c, _t5_bucket(dist_c), -1).astype(jnp.int32)
    n_blk = seq // SLC_BLOCK
    blk = jnp.arange(n_blk)
    blk_start = blk * SLC_BLOCK
    overlap_t = ((cmp_start[None, :] < blk_start[:, None] + SLC_BLOCK)
                 & (cmp_start[None, :] + CMP_BLOCK > blk_start[:, None])
                 & (jnp.arange(n_chunk) < n_cmp)[None, :]).astype(F32)
    back = (pos // SLC_BLOCK)[None, :] - blk[:, None]
    forced = (blk[:, None] == 0) | ((back >= 0) & (back < N_LOCAL))
    causal_blk = blk_start[:, None] <= pos[None, :]
    bonus_t = jnp.where(causal_blk, jnp.where(forced, FORCE_BONUS, 0.0), NEG).astype(F32)

    ocmp, selneg = pl.pallas_call(
        _nsa_select_kernel,
        out_shape=[jax.ShapeDtypeStruct((bsz, seq, nq), BF16),
                   jax.ShapeDtypeStruct((bsz, grp, seq, LANE), BF16)],
        grid=(grp, bsz),
        in_specs=[pl.BlockSpec(memory_space=pltpu.SMEM),
                  pl.BlockSpec((1, seq, hpg * NSA_DK), lambda g, b: (b, 0, g)),
                  pl.BlockSpec((1, 1, n_chunk, NSA_DK), lambda g, b: (b, g, 0, 0)),
                  pl.BlockSpec((1, 1, n_chunk, NSA_DV), lambda g, b: (b, g, 0, 0)),
                  _const_spec(bucket_c.shape),
                  _const_spec(overlap_t.shape),
                  _const_spec(bonus_t.shape)],
        out_specs=[pl.BlockSpec((1, seq, hpg * NSA_DV), lambda g, b: (b, 0, g)),
                   pl.BlockSpec((1, 1, seq, LANE), lambda g, b: (b, g, 0, 0))],
        scratch_shapes=[pltpu.VMEM((hpg, seq, n_chunk), F32)],
        compiler_params=_cparams(2),
        name="nsa_select",
    )(rel_bias, q, kcmp, vcmp, bucket_c, overlap_t, bonus_t)

    t = ATT_TILE
    rr = jnp.arange(t)
    near = jnp.stack([_t5_bucket(k * t + rr[:, None] - rr[None, :]) for k in range(2)]).astype(jnp.int32)
    e_t = (jnp.arange(seq)[:, None] // SLC_BLOCK == jnp.arange(LANE)[None, :]).astype(BF16)
    grp_blk = lambda g, b: (b, 0, g)
    return pl.pallas_call(
        _nsa_attn_kernel,
        out_shape=jax.ShapeDtypeStruct((bsz, seq, nq), BF16),
        grid=(grp, bsz),
        in_specs=[pl.BlockSpec(memory_space=pltpu.SMEM),
                  pl.BlockSpec((1, seq, hpg * NSA_DK), grp_blk),
                  pl.BlockSpec((1, 1, seq, LANE), lambda g, b: (b, g, 0, 0)),
                  pl.BlockSpec((1, seq, NSA_DK), grp_blk),
                  pl.BlockSpec((1, seq, NSA_DV), grp_blk),
                  pl.BlockSpec((1, seq, NSA_DK), grp_blk),
                  pl.BlockSpec((1, seq, NSA_DV), grp_blk),
                  _const_spec(e_t.shape),
                  pl.BlockSpec((1, seq, hpg * NSA_DV), grp_blk),
                  pl.BlockSpec((1, seq, LANE), lambda g, b: (b, 0, 0)),
                  _const_spec(near.shape)],
        out_specs=pl.BlockSpec((1, seq, hpg * NSA_DV), grp_blk),
        scratch_shapes=[pltpu.VMEM((seq, NSA_DK + LANE), BF16),
                        pltpu.VMEM((hpg * t, 3 * t), F32),
                        pltpu.VMEM((1, seq, 2 * NSA_DV), BF16),
                        pltpu.VMEM((1, seq, 2 * NSA_DV), BF16)],
        compiler_params=_cparams(2),
        name="nsa_attn",
    )(rel_bias, q, selneg, ks, vs, kw, vw, e_t, ocmp, gates, near)


def _s5_kernel(x_ref, mod_ref, g1_ref, perm_ref, permt_ref, bm_ref, cm_ref, are_ref, aim_ref,
               dskip_ref, y_ref, bu_ref, state_ref):
    bb, ts, d = x_ref.shape
    rows = bb * ts
    n_slab = d // LANE
    wid = S5_SLAB * S5_STATE

    @pl.when(pl.program_id(1) == 0)
    def _():
        state_ref[...] = jnp.zeros_like(state_ref)

    mod = mod_ref[...]
    u = (_rms(x_ref[...], g1_ref[...]) * (1.0 + mod[:, 1:2, :]) + mod[:, 0:1, :]).reshape(rows, d)
    u_tb = _dot(perm_ref[...], u.astype(BF16)).astype(BF16)
    def project_in(k):
        bu_ref[:, 2 * k * wid:2 * (k + 1) * wid] = _dot(u_tb[:, k * LANE:(k + 1) * LANE], bm_ref[k])

    def scan(k):
        c_re, c_im = 2 * k * wid, (2 * k + 1) * wid
        a_re = jnp.broadcast_to(are_ref[k], (bb, wid))
        a_im = jnp.broadcast_to(aim_ref[k], (bb, wid))
        xr, xi = state_ref[:, c_re:c_re + wid], state_ref[:, c_im:c_im + wid]
        for t in range(ts):
            r0 = t * bb
            nr = a_re * xr - a_im * xi + bu_ref[r0:r0 + bb, c_re:c_re + wid]
            ni = a_re * xi + a_im * xr + bu_ref[r0:r0 + bb, c_im:c_im + wid]
            bu_ref[r0:r0 + bb, c_re:c_re + wid] = nr
            bu_ref[r0:r0 + bb, c_im:c_im + wid] = ni
            xr, xi = nr, ni
        state_ref[:, c_re:c_re + wid] = xr
        state_ref[:, c_im:c_im + wid] = xi

    def project_out(k):
        return _dot(bu_ref[:, 2 * k * wid:2 * (k + 1) * wid].astype(BF16), cm_ref[k]).astype(BF16)

    ys = []
    project_in(0)
    for k in range(n_slab):
        if k + 1 < n_slab:
            project_in(k + 1)
        scan(k)
        ys.append(project_out(k))
    y_tb = jnp.concatenate(ys, axis=1)
    y = _dot(permt_ref[...], y_tb) + dskip_ref[...] * u
    y_ref[...] = _gelu_tanh(y).astype(BF16).reshape(bb, ts, d)


def _s5_mixer(x, mod, g1, lam_re, lam_im, log_dt, b_re, b_im, c_re, c_im, d_skip):
    bsz, seq, d = x.shape
    n_slab = d // LANE
    dt = jnp.exp(log_dt)[:, None]
    mag = jnp.exp(lam_re * dt)
    lb_re = mag * jnp.cos(lam_im * dt)
    lb_im = mag * jnp.sin(lam_im * dt)
    den = lam_re * lam_re + lam_im * lam_im
    f_re = ((lb_re - 1.0) * lam_re + lb_im * lam_im) / den
    f_im = (lb_im * lam_re - (lb_re - 1.0) * lam_im) / den
    bb_re = f_re[..., None] * b_re - f_im[..., None] * b_im
    bb_im = f_re[..., None] * b_im + f_im[..., None] * b_re
    eye = jnp.eye(S5_SLAB, dtype=F32)
    wid = S5_SLAB * S5_STATE

    def in_mat(b):
        b = b.reshape(n_slab, S5_SLAB, S5_STATE, S5_GROUP)
        return jnp.einsum('kgpi,gh->kgihp', b, eye).reshape(n_slab, LANE, wid)

    def out_mat(c):
        c = c.reshape(n_slab, S5_SLAB, S5_GROUP, S5_STATE)
        return jnp.einsum('kgip,gh->kgphi', c, eye).reshape(n_slab, wid, LANE)

    bm = jnp.concatenate([in_mat(bb_re), in_mat(bb_im)], axis=2).astype(BF16)
    cm = jnp.concatenate([out_mat(c_re), -out_mat(c_im)], axis=1).astype(BF16)
    a_re = lb_re.reshape(n_slab, 1, wid)
    a_im = lb_im.reshape(n_slab, 1, wid)

    bb, ts = S5_BB, S5_TS
    rows = bb * ts
    r = jnp.arange(rows)
    perm = ((r % bb)[:, None] * ts + (r // bb)[:, None] == r[None, :]).astype(BF16)
    blk = lambda i, t: (i, t, 0)
    return pl.pallas_call(
        _s5_kernel,
        out_shape=jax.ShapeDtypeStruct((bsz, seq, d), BF16),
        grid=(bsz // bb, seq // ts),
        in_specs=[pl.BlockSpec((bb, ts, d), blk),
                  pl.BlockSpec((bb, N_MOD, d), lambda i, t: (i, 0, 0)),
                  _const_spec((1, d)),
                  _const_spec((rows, rows)), _const_spec((rows, rows)),
                  _const_spec(bm.shape), _const_spec(cm.shape),
                  _const_spec(a_re.shape), _const_spec(a_im.shape),
                  _const_spec((1, d))],
        out_specs=pl.BlockSpec((bb, ts, d), blk),
        scratch_shapes=[pltpu.VMEM((rows, 2 * wid * n_slab), F32),
                        pltpu.VMEM((bb, 2 * wid * n_slab), F32)],
        compiler_params=_cparams(2),
        name="s5_scan",
    )(x, mod, g1.reshape(1, d), perm, perm.T, bm, cm, a_re, a_im, d_skip.reshape(1, d))


def kernel(x, c, ada_w, ada_b, norm1_g, norm2_g, final_g, rel_bias, mla_w_dkv, mla_g_q, mla_g_kv, mla_w_uq, mla_w_ukv, mla_w_o, nsa_w_in, nsa_pe_k, nsa_w1_k, nsa_w2_k, nsa_pe_v, nsa_w1_v, nsa_w2_v, nsa_w_o, s5_lam_re, s5_lam_im, s5_log_dt, s5_b_re, s5_b_im, s5_c_re, s5_c_im, s5_d, s5_w_glu, ffn_w_up, ffn_conv_w, ffn_conv_b, ffn_w_down):
    depth = ada_w.shape[0]
    mods = _ada_mod(c, ada_w, ada_b)
    for i in range(depth):
        mod = mods[i]
        kind, j = i % N_MIXERS, i // N_MIXERS
        if kind == 0:
            o = _mla_mixer(x, mod, norm1_g[i], mla_w_dkv[j], mla_g_q[j], mla_g_kv[j],
                           mla_w_uq[j], mla_w_ukv[j])
            w_post, glu = mla_w_o[j], False
        elif kind == 1:
            o = _nsa_mixer(x, mod, norm1_g[i], nsa_w_in[j], nsa_pe_k[j], nsa_w1_k[j], nsa_w2_k[j],
                           nsa_pe_v[j], nsa_w1_v[j], nsa_w2_v[j], rel_bias)
            w_post, glu = nsa_w_o[j], False
        else:
            o = _s5_mixer(x, mod, norm1_g[i], s5_lam_re[j], s5_lam_im[j], s5_log_dt[j],
                          s5_b_re[j], s5_b_im[j], s5_c_re[j], s5_c_im[j], s5_d[j])
            w_post, glu = s5_w_glu[j], True
        x = _post_ffn(x, o, mod, w_post.astype(BF16), norm2_g[i], ffn_w_up[i].astype(BF16),
                      ffn_conv_w[i], ffn_conv_b[i], ffn_w_down[i].astype(BF16), final_g,
                      glu=glu, final=(i == depth - 1))
    return x
```

```python
import functools
import math

import jax
import jax.numpy as jnp
from jax import lax
from jax.experimental import pallas as pl
from jax.experimental.pallas import tpu as pltpu

F32 = jnp.float32
BF16 = jnp.bfloat16

NORM_EPS = 1e-6
N_MOD = 6
N_MIXERS = 3

MLA_HEADS = 8
MLA_Q_LORA = 512
MLA_KV_LORA = 256
MLA_NOPE = 128
MLA_ROPE = 64
MLA_V = 128
MLA_QK = MLA_NOPE + MLA_ROPE
ROPE_THETA = 10000.0

NSA_HEADS = 8
NSA_GROUPS = 2
NSA_HPG = NSA_HEADS // NSA_GROUPS
NSA_DK = 128
NSA_DV = 128
CMP_BLOCK = 32
CMP_STRIDE = 16
CMP_RATIO = CMP_BLOCK // CMP_STRIDE
CMP_HIDDEN = 256
SLC_BLOCK = 64
SLC_TOPN = 16
N_LOCAL = 2
FORCE_BONUS = 1e4
WINDOW = 512
REL_BUCKETS = 32
REL_MAX_DIST = 128

S5_GROUP = 16
S5_STATE = 64
CONV_WIDTH = 3

LANE = 128
SUBLANE = 8
MXU_WIDTH = 256
V7X_VMEM_BYTES = 64 * 1024 * 1024
VMEM_LIMIT = V7X_VMEM_BYTES - 8 * 1024 * 1024
NEG = -1e30
LOG2E = math.log2(math.e)

TOKEN_TILE = 512
FFN_CHUNK = MXU_WIDTH
ADA_COLS = 1024
ATT_TILE = 128
MLA_TQ = 256
CMP_BATCH = 4
S5_BB = SUBLANE
S5_TS = 64
S5_SLAB = LANE // S5_GROUP


def _cparams(n_axes):
    return pltpu.CompilerParams(
        dimension_semantics=("arbitrary",) * n_axes,
        vmem_limit_bytes=VMEM_LIMIT)


def _const_spec(shape):
    zeros = (0,) * len(shape)
    return pl.BlockSpec(shape, lambda *_: zeros, pipeline_mode=pl.Buffered(1))


def _dot(a, b):
    return jnp.dot(a, b, preferred_element_type=F32)


def _dot_nt(a, b):
    return lax.dot_general(a, b, (((1,), (1,)), ((), ())), preferred_element_type=F32)


def _rms(x, g):
    return x * lax.rsqrt(jnp.mean(x * x, axis=-1, keepdims=True) + NORM_EPS) * g


def _norm_mod(x, g, shift, scale):
    return _rms(x, g) * (1.0 + scale) + shift


def _gelu_tanh(x):
    c = math.sqrt(2.0 / math.pi)
    return 0.5 * x * (1.0 + jnp.tanh(c * (x + 0.044715 * (x * x * x))))


def _sigmoid(x):
    return 1.0 / (1.0 + jnp.exp(-x))


def _softmax_pv(scores, values):
    m = functools.reduce(jnp.maximum, [jnp.max(s, axis=1, keepdims=True) for s in scores])
    acc = functools.reduce(
        jnp.add, [_dot(jnp.exp2(s - m).astype(BF16), v) for s, v in zip(scores, values)])
    width = acc.shape[1] // 2
    return acc[:, :width] / acc[:, width:]


def _t5_bucket(dist):
    n = jnp.maximum(dist, 0)
    max_exact = REL_BUCKETS // 2
    nf = jnp.maximum(n, 1).astype(F32)
    large = max_exact + (jnp.log(nf / max_exact) / math.log(REL_MAX_DIST / max_exact)
                         * (REL_BUCKETS - max_exact)).astype(jnp.int32)
    large = jnp.minimum(large, REL_BUCKETS - 1)
    return jnp.where(n < max_exact, n, large)


def _ada_kernel(c_ref, w_ref, b_ref, o_ref):
    c = c_ref[...]
    c_act = c * _sigmoid(c)
    o_ref[0] = jnp.dot(c_act, w_ref[0], preferred_element_type=F32,
                       precision=lax.Precision.HIGHEST) + b_ref[0]


def _ada_mod(c, ada_w, ada_b):
    depth, d, n = ada_w.shape
    bsz = c.shape[0]
    tn = ADA_COLS
    out = pl.pallas_call(
        _ada_kernel,
        out_shape=jax.ShapeDtypeStruct((depth, bsz, n), F32),
        grid=(depth, n // tn),
        in_specs=[pl.BlockSpec((bsz, d), lambda i, j: (0, 0)),
                  pl.BlockSpec((1, d, tn), lambda i, j: (i, 0, j)),
                  pl.BlockSpec((1, 1, tn), lambda i, j: (i, 0, j))],
        out_specs=pl.BlockSpec((1, bsz, tn), lambda i, j: (i, 0, j)),
        compiler_params=_cparams(2),
        name="ada_mod",
    )(c, ada_w, ada_b.reshape(depth, 1, n))
    return out.reshape(depth, bsz, N_MOD, d)


def _post_ffn_kernel(x_ref, o_ref, mod_ref, wpost_ref, g2_ref, wup_ref, cw_ref, cb_ref,
                     wdown_ref, gf_ref, out_ref, act_ref, carry_ref,
                     *, glu, final, fc):
    tm = x_ref.shape[1]
    d_ff = wdown_ref.shape[0]

    @pl.when(pl.program_id(1) == 0)
    def _():
        carry_ref[...] = jnp.zeros_like(carry_ref)

    x = x_ref[0]
    mod = mod_ref[0]
    y = _dot(o_ref[0], wpost_ref[...])
    if glu:
        d = x.shape[-1]
        y = y[:, :d] * _sigmoid(y[:, d:])
    x1 = x + mod[2:3] * y
    h = _norm_mod(x1, g2_ref[...], mod[3:4], mod[4:5]).astype(BF16)

    sub = SUBLANE
    row = lax.broadcasted_iota(jnp.int32, (sub, fc), 0)

    def shifted(z, c, k):
        r = pltpu.roll(z, k, 0)
        head = jnp.where(row < k, pltpu.roll(c, k, 0), r[:sub])
        return jnp.concatenate([head, r[sub:]], axis=0)

    def conv(z, col):
        c = carry_ref[:, col:col + fc]
        carry_ref[:, col:col + fc] = z[tm - sub:, :]
        w = cw_ref[:, col:col + fc]
        return (w[0:1] * shifted(z, c, 2) + w[1:2] * shifted(z, c, 1) + w[2:3] * z
                + cb_ref[:, col:col + fc])

    for f in range(0, d_ff, fc):
        val = conv(_dot(h, wup_ref[:, f:f + fc]), f)
        gate = conv(_dot(h, wup_ref[:, d_ff + f:d_ff + f + fc]), d_ff + f)
        act_ref[:, f:f + fc] = (val * (gate * _sigmoid(gate))).astype(BF16)

    x2 = x1 + mod[5:6] * _dot(act_ref[...], wdown_ref[...])
    if final:
        x2 = _rms(x2, gf_ref[...])
    out_ref[0] = x2


def _post_ffn(x, o, mod, w_post, g2, w_up, conv_w, conv_b, w_down, final_g, *, glu, final):
    bsz, seq, d = x.shape
    d_ff = w_down.shape[0]
    tm = TOKEN_TILE
    fc = FFN_CHUNK
    row = lambda b, i: (b, i, 0)
    return pl.pallas_call(
        functools.partial(_post_ffn_kernel, glu=glu, final=final, fc=fc),
        out_shape=jax.ShapeDtypeStruct((bsz, seq, d), F32),
        grid=(bsz, seq // tm),
        in_specs=[pl.BlockSpec((1, tm, d), row),
                  pl.BlockSpec((1, tm, d), row),
                  pl.BlockSpec((1, N_MOD, d), lambda b, i: (b, 0, 0)),
                  _const_spec(w_post.shape),
                  _const_spec((1, d)),
                  _const_spec(w_up.shape),
                  _const_spec(conv_w.shape),
                  _const_spec((1, 2 * d_ff)),
                  _const_spec(w_down.shape),
                  _const_spec((1, d))],
        out_specs=pl.BlockSpec((1, tm, d), row),
        scratch_shapes=[pltpu.VMEM((tm, d_ff), BF16),
                        pltpu.VMEM((SUBLANE, 2 * d_ff), F32)],
        compiler_params=_cparams(2),
        name="post_ffn",
    )(x, o, mod, w_post, g2.reshape(1, d), w_up, conv_w, conv_b.reshape(1, -1), w_down,
      final_g.reshape(1, d))


def _rope_tables(seq, reps):
    half = MLA_ROPE // 2
    inv = ROPE_THETA ** (-jnp.arange(half, dtype=F32) / half)
    ang = jnp.arange(seq).astype(F32)[:, None] * inv[None, :]
    cos, sin = jnp.cos(ang), jnp.sin(ang)
    cos64 = jnp.concatenate([cos, cos], axis=1)
    sin64 = jnp.concatenate([-sin, sin], axis=1)
    return jnp.tile(cos64, (1, reps)), jnp.tile(sin64, (1, reps))


def _mla_proj_kernel(x_ref, mod_ref, g1_ref, wdkv_ref, gq_ref, gkv_ref, wuq_ref, wukv_ref,
                     cos_ref, sin_ref, qn_ref, qr_ref, kn_ref, kr_ref, v_ref):
    nq = MLA_HEADS * MLA_NOPE
    nr = MLA_HEADS * MLA_ROPE
    scale = MLA_QK ** -0.5 * LOG2E
    mod = mod_ref[0]
    h = _norm_mod(x_ref[0], g1_ref[...], mod[0:1], mod[1:2]).astype(BF16)
    dkv = _dot(h, wdkv_ref[...])
    c_q = _rms(dkv[:, :MLA_Q_LORA], gq_ref[...]).astype(BF16)
    c_kv = _rms(dkv[:, MLA_Q_LORA:MLA_Q_LORA + MLA_KV_LORA], gkv_ref[...]).astype(BF16)
    cos = cos_ref[...]
    sin = sin_ref[...]
    r0 = MLA_Q_LORA + MLA_KV_LORA
    kr_ref[0] = (dkv[:, r0:r0 + LANE] * cos[:, :LANE]
                 + dkv[:, r0 + LANE:r0 + 2 * LANE] * sin[:, :LANE]).astype(BF16)
    q = _dot(c_q, wuq_ref[...])
    qn_ref[0] = (q[:, :nq] * scale).astype(BF16)
    t = q[:, nq:]
    lane = lax.broadcasted_iota(jnp.int32, t.shape, 1)
    half = MLA_ROPE // 2
    swapped = jnp.where((lane & (MLA_ROPE - 1)) < half, pltpu.roll(t, nr - half, 1),
                        pltpu.roll(t, half, 1))
    qr_ref[0] = ((t * cos + swapped * sin) * scale).astype(BF16)
    kv = _dot(c_kv, wukv_ref[...])
    kn_ref[0] = kv[:, :nq].astype(BF16)
    v_ref[0] = kv[:, nq:].astype(BF16)


def _mla_attn_kernel(qn_ref, qr_ref, kn_ref, kr_ref, vin_ref, o_ref, v_ref):
    seq = qn_ref.shape[1]
    tq = MLA_TQ
    v_ref[0, :, :MLA_V] = vin_ref[0]
    v_ref[0, :, MLA_V:] = jnp.ones((seq, MLA_V), BF16)
    odd = pl.program_id(1) % 2
    lane = lax.broadcasted_iota(jnp.int32, (tq, LANE), 1)
    keep = (lane >= MLA_ROPE) == (odd == 1)
    row = lax.broadcasted_iota(jnp.int32, (tq, tq), 0)
    col = lax.broadcasted_iota(jnp.int32, (tq, tq), 1)
    def scores(i):
        lo, hi = i * tq, (i + 1) * tq
        qr = jnp.where(keep, qr_ref[0, lo:hi, :], jnp.zeros((), BF16))
        q = jnp.concatenate([qn_ref[0, lo:hi, :], qr], axis=1)
        k_d = jnp.concatenate([kn_ref[0, lo:hi, :], kr_ref[0, lo:hi, :]], axis=1)
        out = [jnp.where(col <= row, _dot_nt(q, k_d), NEG)]
        if i > 0:
            k_o = jnp.concatenate([kn_ref[0, :lo, :], kr_ref[0, :lo, :]], axis=1)
            out.append(_dot_nt(q, k_o))
        return out

    n_tile = seq // tq
    nxt = scores(0)
    for i in range(n_tile):
        lo, hi = i * tq, (i + 1) * tq
        cur = nxt
        if i + 1 < n_tile:
            nxt = scores(i + 1)
        values = [v_ref[0, lo:hi, :]] + ([v_ref[0, :lo, :]] if i > 0 else [])
        o_ref[0, lo:hi, :] = _softmax_pv(cur, values).astype(BF16)


def _mla_mixer(x, mod, g1, w_dkv, g_q, g_kv, w_uq, w_ukv):
    bsz, seq, d = x.shape
    hds = MLA_HEADS
    half = MLA_ROPE // 2
    swap = jnp.concatenate([jnp.arange(half, MLA_ROPE), jnp.arange(half)])
    r0 = MLA_Q_LORA + MLA_KV_LORA
    kr = w_dkv[:, r0:]
    wdkv = jnp.concatenate([w_dkv[:, :r0], kr, kr, kr[:, swap], kr[:, swap]], axis=1).astype(BF16)
    wq = w_uq.reshape(MLA_Q_LORA, hds, MLA_QK)
    wq_r = wq[:, :, MLA_NOPE:]
    wuq = jnp.concatenate([wq[:, :, :MLA_NOPE].reshape(MLA_Q_LORA, -1),
                           wq_r.reshape(MLA_Q_LORA, -1)], axis=1).astype(BF16)
    wkv = w_ukv.reshape(MLA_KV_LORA, hds, MLA_NOPE + MLA_V)
    wukv = jnp.concatenate([wkv[:, :, :MLA_NOPE].reshape(MLA_KV_LORA, -1),
                            wkv[:, :, MLA_NOPE:].reshape(MLA_KV_LORA, -1)], axis=1).astype(BF16)
    cos, sin = _rope_tables(seq, hds)
    nq, nr = hds * MLA_NOPE, hds * MLA_ROPE
    tm = TOKEN_TILE
    row = lambda i, b: (b, i, 0)
    qn, qr, kn, kr2, v = pl.pallas_call(
        _mla_proj_kernel,
        out_shape=[jax.ShapeDtypeStruct((bsz, seq, nq), BF16),
                   jax.ShapeDtypeStruct((bsz, seq, nr), BF16),
                   jax.ShapeDtypeStruct((bsz, seq, nq), BF16),
                   jax.ShapeDtypeStruct((bsz, seq, LANE), BF16),
                   jax.ShapeDtypeStruct((bsz, seq, nq), BF16)],
        grid=(seq // tm, bsz),
        in_specs=[pl.BlockSpec((1, tm, d), row),
                  pl.BlockSpec((1, N_MOD, d), lambda i, b: (b, 0, 0)),
                  _const_spec((1, d)),
                  _const_spec(wdkv.shape),
                  _const_spec((1, MLA_Q_LORA)),
                  _const_spec((1, MLA_KV_LORA)),
                  _const_spec(wuq.shape),
                  _const_spec(wukv.shape),
                  pl.BlockSpec((tm, nr), lambda i, b: (i, 0)),
                  pl.BlockSpec((tm, nr), lambda i, b: (i, 0))],
        out_specs=[pl.BlockSpec((1, tm, nq), row),
                   pl.BlockSpec((1, tm, nr), row),
                   pl.BlockSpec((1, tm, nq), row),
                   pl.BlockSpec((1, tm, LANE), row),
                   pl.BlockSpec((1, tm, nq), row)],
        compiler_params=_cparams(2),
        name="mla_proj",
    )(x, mod, g1.reshape(1, d), wdkv, g_q.reshape(1, -1), g_kv.reshape(1, -1), wuq, wukv, cos, sin)

    head = lambda b, h: (b, 0, h)
    return pl.pallas_call(
        _mla_attn_kernel,
        out_shape=jax.ShapeDtypeStruct((bsz, seq, hds * MLA_V), BF16),
        grid=(bsz, hds),
        in_specs=[pl.BlockSpec((1, seq, LANE), head),
                  pl.BlockSpec((1, seq, LANE), lambda b, h: (b, 0, h // 2)),
                  pl.BlockSpec((1, seq, LANE), head),
                  pl.BlockSpec((1, seq, LANE), lambda b, h: (b, 0, 0)),
                  pl.BlockSpec((1, seq, LANE), head)],
        out_specs=pl.BlockSpec((1, seq, LANE), head),
        scratch_shapes=[pltpu.VMEM((1, seq, 2 * MLA_V), BF16)],
        compiler_params=_cparams(2),
        name="mla_attn",
    )(qn, qr, kn, kr2, v)


def _nsa_proj_kernel(x_ref, mod_ref, g1_ref, w_ref, q_ref, kc_ref, vc_ref, ks_ref, vs_ref,
                     kw_ref, vw_ref, gate_ref, cmp_ref):
    nq = NSA_HEADS * NSA_DK
    nkv = NSA_GROUPS * NSA_DK
    tm = x_ref.shape[1]
    mod = mod_ref[0]
    h = _norm_mod(x_ref[0], g1_ref[...], mod[0:1], mod[1:2]).astype(BF16)
    p = _dot(h, w_ref[...])
    q_ref[0] = (p[:, :nq] * (NSA_DK ** -0.5 * LOG2E)).astype(BF16)
    for n, ref in enumerate((ks_ref, vs_ref, kw_ref, vw_ref)):
        ref[0] = p[:, nq + (n + 2) * nkv:nq + (n + 3) * nkv].astype(BF16)
    gate_ref[0] = _sigmoid(p[:, nq + 6 * nkv:])
    for n, ref in enumerate((kc_ref, vc_ref)):
        for g in range(NSA_GROUPS):
            c0 = nq + n * nkv + g * NSA_DK
            stage = cmp_ref.at[n * NSA_GROUPS + g]
            stage[...] = p[:, c0:c0 + NSA_DK]
            for r in range(CMP_STRIDE):
                ref[0, g, r] = stage[pl.ds(r, tm // CMP_STRIDE, stride=CMP_STRIDE), :].astype(BF16)


def _nsa_compress_kernel(ck_ref, cv_ref, pek_ref, pev_ref, w1k_ref, w1v_ref, w2k_ref, w2v_ref,
                         ok_ref, ov_ref):
    nb, grp, _, n_chunk, dk = ck_ref.shape
    half = CMP_STRIDE * dk
    rows = nb * grp * n_chunk
    for c_ref, pe_ref, w1_ref, w2_ref, o_ref in ((ck_ref, pek_ref, w1k_ref, w2k_ref, ok_ref),
                                                 (cv_ref, pev_ref, w1v_ref, w2v_ref, ov_ref)):
        chunks = jnp.concatenate(
            [jnp.concatenate([c_ref[b, g, r] for r in range(CMP_STRIDE)], axis=1)
             for b in range(nb) for g in range(grp)], axis=0)
        first = _dot(chunks, w1_ref[:half, :])
        second = _dot(chunks, w1_ref[half:, :])
        pe = jnp.broadcast_to(pe_ref[...], (SUBLANE, 2 * half)).astype(BF16)
        const = _dot(pe, w1_ref[...])[0:1]
        hid = _gelu_tanh(first + pltpu.roll(second, rows - 1, 0) + const).astype(BF16)
        o_ref[...] = _dot(hid, w2_ref[...]).astype(BF16).reshape(nb, grp, n_chunk, dk)


def _nsa_select_kernel(rb_ref, q_ref, kc_ref, vc_ref, bucket_ref, ovl_ref, bonus_ref,
                       ocmp_ref, sel_ref, bias_ref):
    g = pl.program_id(0)
    seq = q_ref.shape[1]
    n_blk = bonus_ref.shape[0]
    bucket = bucket_ref[...]

    @pl.when(pl.program_id(1) == 0)
    def _():
        for hp in range(NSA_HPG):
            acc = jnp.zeros(bucket.shape, F32)
            for r in range(REL_BUCKETS):
                acc = jnp.where(bucket == r, rb_ref[g * NSA_HPG + hp, r] * LOG2E, acc)
            bias_ref[hp] = acc

    valid = bucket >= 0
    kc = kc_ref[0, 0]
    vc = vc_ref[0, 0]
    psum = jnp.zeros(bucket.shape, F32)
    for hp in range(NSA_HPG):
        s = _dot_nt(q_ref[0, :, hp * NSA_DK:(hp + 1) * NSA_DK], kc) + bias_ref[hp]
        s = jnp.where(valid, s, NEG)
        m = jnp.max(s, axis=1, keepdims=True)
        m = jnp.where(m > 0.5 * NEG, m, 0.0)
        p = jnp.exp2(s - m)
        p = p / jnp.maximum(jnp.sum(p, axis=1, keepdims=True), 1e-30)
        ocmp_ref[0, :, hp * NSA_DV:(hp + 1) * NSA_DV] = _dot(p.astype(BF16), vc).astype(BF16)
        psum = psum + p

    imp = lax.dot_general(ovl_ref[...], psum, (((1,), (1,)), ((), ())),
                          preferred_element_type=F32, precision=lax.Precision.HIGHEST)
    bonus = bonus_ref[...]
    score = jnp.where(bonus > 0.5 * NEG, imp + bonus, NEG)
    blk = lax.broadcasted_iota(jnp.int32, (n_blk, seq), 0)
    rank = jnp.zeros((n_blk, seq), F32)
    for i in range(n_blk):
        row = score[i:i + 1, :]
        rank = rank + jnp.where(blk > i, jnp.where(row >= score, 1.0, 0.0),
                                jnp.where(row > score, 1.0, 0.0))
    selneg = jnp.where(rank < float(SLC_TOPN), jnp.where(score > 0.5 * NEG, 0.0, NEG), NEG)
    pad = jnp.zeros((LANE - n_blk, seq), F32)
    sel_ref[0, 0] = jnp.concatenate([selneg, pad], axis=0).T.astype(BF16)


def _nsa_attn_kernel(rb_ref, q_ref, sel_ref, ks_ref, vsin_ref, kw_ref, vwin_ref, et_ref, ocmp_ref,
                     gate_ref, bkt_ref, o_ref, kaug_ref, bias_ref, vs_ref, vw_ref):
    g = pl.program_id(0)
    t = ATT_TILE
    hp_n = NSA_HPG
    seq = q_ref.shape[1]

    @pl.when(pl.program_id(1) == 0)
    def _():
        kaug_ref[:, NSA_DK:] = et_ref[...]
        vs_ref[0, :, NSA_DV:] = jnp.ones((seq, NSA_DV), BF16)
        vw_ref[0, :, NSA_DV:] = jnp.ones((seq, NSA_DV), BF16)
        for kind in range(2):
            bkt = bkt_ref[kind]
            for hp in range(hp_n):
                hd = g * hp_n + hp
                acc = jnp.zeros((t, t), F32)
                for r in range(REL_BUCKETS - 1):
                    acc = jnp.where(
                        bkt == r, (rb_ref[hd, r] - rb_ref[hd, REL_BUCKETS - 1]) * LOG2E, acc)
                bias_ref[hp * t:(hp + 1) * t, (1 - kind) * t:(2 - kind) * t] = acc
        bias_ref[:, 2 * t:] = jnp.zeros((hp_n * t, t), F32)

    kaug_ref[:, :NSA_DK] = ks_ref[0]
    vs_ref[0, :, :NSA_DV] = vsin_ref[0]
    vw_ref[0, :, :NSA_DV] = vwin_ref[0]

    rows = hp_n * t
    r2 = lax.broadcasted_iota(jnp.int32, (rows, 2 * t), 0) & (t - 1)
    c2 = lax.broadcasted_iota(jnp.int32, (rows, 2 * t), 1)
    r1 = lax.broadcasted_iota(jnp.int32, (rows, t), 0) & (t - 1)
    c1 = lax.broadcasted_iota(jnp.int32, (rows, t), 1)
    near_ok = c2 <= r2 + t
    first_ok = c2 <= r2
    edge_ok = c1 > r1
    lane = lax.broadcasted_iota(jnp.int32, (t, LANE), 1)

    def near(i):
        if i == 0:
            return 0, 2 * t, bias_ref[:, t:], first_ok
        return (i - 1) * t, (i + 1) * t, bias_ref[:, :2 * t], near_ok

    def scores_sel(i):
        lo, hi = i * t, (i + 1) * t
        n_lo, n_hi, bias, ok = near(i)
        sel = sel_ref[0, 0, lo:hi, :]
        q_sel = jnp.concatenate(
            [jnp.concatenate([q_ref[0, lo:hi, hp * NSA_DK:(hp + 1) * NSA_DK], sel], axis=1)
             for hp in range(hp_n)], axis=0)
        out = [jnp.where(ok, _dot_nt(q_sel, kaug_ref[n_lo:n_hi, :]) + bias, NEG)]
        if n_lo > 0:
            out.append(_dot_nt(q_sel, kaug_ref[:n_lo, :]))
        return out

    def scores_win(i):
        lo, hi = i * t, (i + 1) * t
        n_lo, n_hi, bias, ok = near(i)
        q_win = jnp.concatenate(
            [q_ref[0, lo:hi, hp * NSA_DK:(hp + 1) * NSA_DK] for hp in range(hp_n)], axis=0)
        out = [jnp.where(ok, _dot_nt(q_win, kw_ref[0, n_lo:n_hi, :]) + bias, NEG)]
        w_lo = max(lo - WINDOW, 0)
        if n_lo > w_lo:
            s_m = _dot_nt(q_win, kw_ref[0, w_lo:n_lo, :])
            if lo - WINDOW >= 0:
                s_m = jnp.concatenate([jnp.where(edge_ok, s_m[:, :t], NEG), s_m[:, t:]], axis=1)
            out.append(s_m)
        return out

    def pieces(v_ref, i, far_lo):
        n_lo, n_hi, _, _ = near(i)
        return [v_ref[0, n_lo:n_hi, :]] + ([v_ref[0, far_lo:n_lo, :]] if n_lo > far_lo else [])

    n_tile = seq // t
    nxt_sel = scores_sel(0)
    for i in range(n_tile):
        lo, hi = i * t, (i + 1) * t
        cur_sel = nxt_sel
        cur_win = scores_win(i)
        o_slc = _softmax_pv(cur_sel, pieces(vs_ref, i, 0))
        if i + 1 < n_tile:
            nxt_sel = scores_sel(i + 1)
        o_win = _softmax_pv(cur_win, pieces(vw_ref, i, max(lo - WINDOW, 0)))

        gate = gate_ref[0, lo:hi, :]
        for hp in range(hp_n):
            hd = g * hp_n + hp

            def gcol(branch):
                return jnp.sum(jnp.where(lane == branch * NSA_HEADS + hd, gate, 0.0),
                               axis=1, keepdims=True)
            sl = slice(hp * t, (hp + 1) * t)
            o = (gcol(0) * ocmp_ref[0, lo:hi, hp * NSA_DV:(hp + 1) * NSA_DV].astype(F32)
                 + gcol(1) * o_slc[sl] + gcol(2) * o_win[sl])
            o_ref[0, lo:hi, hp * NSA_DV:(hp + 1) * NSA_DV] = o.astype(BF16)


def _nsa_mixer(x, mod, g1, w_in, pe_k, w1_k, w2_k, pe_v, w1_v, w2_v, rel_bias):
    bsz, seq, d = x.shape
    grp, hpg = NSA_GROUPS, NSA_HPG
    nq, nkv = NSA_HEADS * NSA_DK, NSA_GROUPS * NSA_DK
    n_gate = 3 * NSA_HEADS
    w = jnp.pad(w_in, ((0, 0), (0, LANE - n_gate))).astype(BF16)
    tm = TOKEN_TILE
    row = lambda i, b: (b, i, 0)
    assert CMP_RATIO == 2 and NSA_DK == NSA_DV
    n_chunk = seq // CMP_STRIDE
    n_cmp = n_chunk - CMP_RATIO + 1
    kv_shape = jax.ShapeDtypeStruct((bsz, seq, nkv), BF16)
    cmp_shape = jax.ShapeDtypeStruct((bsz, grp, CMP_STRIDE, n_chunk, NSA_DK), BF16)
    cmp_spec = pl.BlockSpec((1, grp, CMP_STRIDE, tm // CMP_STRIDE, NSA_DK),
                            lambda i, b: (b, 0, 0, i, 0))
    q, kc, vc, ks, vs, kw, vw, gates = pl.pallas_call(
        _nsa_proj_kernel,
        out_shape=[jax.ShapeDtypeStruct((bsz, seq, nq), BF16)] + [cmp_shape] * 2 + [kv_shape] * 4
                  + [jax.ShapeDtypeStruct((bsz, seq, LANE), F32)],
        grid=(seq // tm, bsz),
        in_specs=[pl.BlockSpec((1, tm, d), row),
                  pl.BlockSpec((1, N_MOD, d), lambda i, b: (b, 0, 0)),
                  _const_spec((1, d)),
                  _const_spec(w.shape)],
        out_specs=[pl.BlockSpec((1, tm, nq), row)] + [cmp_spec] * 2
                  + [pl.BlockSpec((1, tm, nkv), row)] * 4 + [pl.BlockSpec((1, tm, LANE), row)],
        scratch_shapes=[pltpu.VMEM((2 * grp, tm, NSA_DK), F32)],
        compiler_params=_cparams(2),
        name="nsa_proj",
    )(x, mod, g1.reshape(1, d), w)

    nb = math.gcd(bsz, CMP_BATCH)
    wide = CMP_BLOCK * NSA_DK
    chunk_spec = pl.BlockSpec((nb, grp, CMP_STRIDE, n_chunk, NSA_DK), lambda i: (i, 0, 0, 0, 0))
    out_spec = pl.BlockSpec((nb, grp, n_chunk, NSA_DK), lambda i: (i, 0, 0, 0))
    kcmp, vcmp = pl.pallas_call(
        _nsa_compress_kernel,
        out_shape=[jax.ShapeDtypeStruct((bsz, grp, n_chunk, NSA_DK), BF16)] * 2,
        grid=(bsz // nb,),
        in_specs=[chunk_spec, chunk_spec,
                  _const_spec((1, wide)), _const_spec((1, wide)),
                  _const_spec((wide, CMP_HIDDEN)), _const_spec((wide, CMP_HIDDEN)),
                  _const_spec((CMP_HIDDEN, NSA_DK)), _const_spec((CMP_HIDDEN, NSA_DV))],
        out_specs=[out_spec, out_spec],
        compiler_params=_cparams(1),
        name="nsa_compress",
    )(kc, vc, pe_k.reshape(1, wide), pe_v.reshape(1, wide), w1_k.astype(BF16), w1_v.astype(BF16),
      w2_k.astype(BF16), w2_v.astype(BF16))

    pos = jnp.arange(seq)
    cmp_start = jnp.arange(n_chunk) * CMP_STRIDE
    dist_c = pos[:, None] - (cmp_start + CMP_BLOCK - 1)[None, :]
    ok_c = (dist_c >= 0) & (jnp.arange(n_chunk) < n_cmp)[None, :]
    bucket_c = jnp.where(ok_c, _t5_bucket(dist_c), -1).astype(jnp.int32)
    n_blk = seq // SLC_BLOCK
    blk = jnp.arange(n_blk)
    blk_start = blk * SLC_BLOCK
    overlap_t = ((cmp_start[None, :] < blk_start[:, None] + SLC_BLOCK)
                 & (cmp_start[None, :] + CMP_BLOCK > blk_start[:, None])
                 & (jnp.arange(n_chunk) < n_cmp)[None, :]).astype(F32)
    back = (pos // SLC_BLOCK)[None, :] - blk[:, None]
    forced = (blk[:, None] == 0) | ((back >= 0) & (back < N_LOCAL))
    causal_blk = blk_start[:, None] <= pos[None, :]
    bonus_t = jnp.where(causal_blk, jnp.where(forced, FORCE_BONUS, 0.0), NEG).astype(F32)

    ocmp, selneg = pl.pallas_call(
        _nsa_select_kernel,
        out_shape=[jax.ShapeDtypeStruct((bsz, seq, nq), BF16),
                   jax.ShapeDtypeStruct((bsz, grp, seq, LANE), BF16)],
        grid=(grp, bsz),
        in_specs=[pl.BlockSpec(memory_space=pltpu.SMEM),
                  pl.BlockSpec((1, seq, hpg * NSA_DK), lambda g, b: (b, 0, g)),
                  pl.BlockSpec((1, 1, n_chunk, NSA_DK), lambda g, b: (b, g, 0, 0)),
                  pl.BlockSpec((1, 1, n_chunk, NSA_DV), lambda g, b: (b, g, 0, 0)),
                  _const_spec(bucket_c.shape),
                  _const_spec(overlap_t.shape),
                  _const_spec(bonus_t.shape)],
        out_specs=[pl.BlockSpec((1, seq, hpg * NSA_DV), lambda g, b: (b, 0, g)),
                   pl.BlockSpec((1, 1, seq, LANE), lambda g, b: (b, g, 0, 0))],
        scratch_shapes=[pltpu.VMEM((hpg, seq, n_chunk), F32)],
        compiler_params=_cparams(2),
        name="nsa_select",
    )(rel_bias, q, kcmp, vcmp, bucket_c, overlap_t, bonus_t)

    t = ATT_TILE
    rr = jnp.arange(t)
    near = jnp.stack([_t5_bucket(k * t + rr[:, None] - rr[None, :]) for k in range(2)]).astype(jnp.int32)
    e_t = (jnp.arange(seq)[:, None] // SLC_BLOCK == jnp.arange(LANE)[None, :]).astype(BF16)
    grp_blk = lambda g, b: (b, 0, g)
    return pl.pallas_call(
        _nsa_attn_kernel,
        out_shape=jax.ShapeDtypeStruct((bsz, seq, nq), BF16),
        grid=(grp, bsz),
        in_specs=[pl.BlockSpec(memory_space=pltpu.SMEM),
                  pl.BlockSpec((1, seq, hpg * NSA_DK), grp_blk),
                  pl.BlockSpec((1, 1, seq, LANE), lambda g, b: (b, g, 0, 0)),
                  pl.BlockSpec((1, seq, NSA_DK), grp_blk),
                  pl.BlockSpec((1, seq, NSA_DV), grp_blk),
                  pl.BlockSpec((1, seq, NSA_DK), grp_blk),
                  pl.BlockSpec((1, seq, NSA_DV), grp_blk),
                  _const_spec(e_t.shape),
                  pl.BlockSpec((1, seq, hpg * NSA_DV), grp_blk),
                  pl.BlockSpec((1, seq, LANE), lambda g, b: (b, 0, 0)),
                  _const_spec(near.shape)],
        out_specs=pl.BlockSpec((1, seq, hpg * NSA_DV), grp_blk),
        scratch_shapes=[pltpu.VMEM((seq, NSA_DK + LANE), BF16),
                        pltpu.VMEM((hpg * t, 3 * t), F32),
                        pltpu.VMEM((1, seq, 2 * NSA_DV), BF16),
                        pltpu.VMEM((1, seq, 2 * NSA_DV), BF16)],
        compiler_params=_cparams(2),
        name="nsa_attn",
    )(rel_bias, q, selneg, ks, vs, kw, vw, e_t, ocmp, gates, near)


def _s5_kernel(x_ref, mod_ref, g1_ref, perm_ref, permt_ref, bm_ref, cm_ref, are_ref, aim_ref,
               dskip_ref, y_ref, bu_ref, state_ref):
    bb, ts, d = x_ref.shape
    rows = bb * ts
    n_slab = d // LANE
    wid = S5_SLAB * S5_STATE

    @pl.when(pl.program_id(1) == 0)
    def _():
        state_ref[...] = jnp.zeros_like(state_ref)

    mod = mod_ref[...]
    u = (_rms(x_ref[...], g1_ref[...]) * (1.0 + mod[:, 1:2, :]) + mod[:, 0:1, :]).reshape(rows, d)
    u_tb = _dot(perm_ref[...], u.astype(BF16)).astype(BF16)
    def project_in(k):
        bu_ref[:, 2 * k * wid:2 * (k + 1) * wid] = _dot(u_tb[:, k * LANE:(k + 1) * LANE], bm_ref[k])

    def scan(k):
        c_re, c_im = 2 * k * wid, (2 * k + 1) * wid
        a_re = jnp.broadcast_to(are_ref[k], (bb, wid))
        a_im = jnp.broadcast_to(aim_ref[k], (bb, wid))
        xr, xi = state_ref[:, c_re:c_re + wid], state_ref[:, c_im:c_im + wid]
        for t in range(ts):
            r0 = t * bb
            nr = a_re * xr - a_im * xi + bu_ref[r0:r0 + bb, c_re:c_re + wid]
            ni = a_re * xi + a_im * xr + bu_ref[r0:r0 + bb, c_im:c_im + wid]
            bu_ref[r0:r0 + bb, c_re:c_re + wid] = nr
            bu_ref[r0:r0 + bb, c_im:c_im + wid] = ni
            xr, xi = nr, ni
        state_ref[:, c_re:c_re + wid] = xr
        state_ref[:, c_im:c_im + wid] = xi

    def project_out(k):
        return _dot(bu_ref[:, 2 * k * wid:2 * (k + 1) * wid].astype(BF16), cm_ref[k]).astype(BF16)

    ys = []
    project_in(0)
    for k in range(n_slab):
        if k + 1 < n_slab:
            project_in(k + 1)
        scan(k)
        ys.append(project_out(k))
    y_tb = jnp.concatenate(ys, axis=1)
    y = _dot(permt_ref[...], y_tb) + dskip_ref[...] * u
    y_ref[...] = _gelu_tanh(y).astype(BF16).reshape(bb, ts, d)


def _s5_mixer(x, mod, g1, lam_re, lam_im, log_dt, b_re, b_im, c_re, c_im, d_skip):
    bsz, seq, d = x.shape
    n_slab = d // LANE
    dt = jnp.exp(log_dt)[:, None]
    mag = jnp.exp(lam_re * dt)
    lb_re = mag * jnp.cos(lam_im * dt)
    lb_im = mag * jnp.sin(lam_im * dt)
    den = lam_re * lam_re + lam_im * lam_im
    f_re = ((lb_re - 1.0) * lam_re + lb_im * lam_im) / den
    f_im = (lb_im * lam_re - (lb_re - 1.0) * lam_im) / den
    bb_re = f_re[..., None] * b_re - f_im[..., None] * b_im
    bb_im = f_re[..., None] * b_im + f_im[..., None] * b_re
    eye = jnp.eye(S5_SLAB, dtype=F32)
    wid = S5_SLAB * S5_STATE

    def in_mat(b):
        b = b.reshape(n_slab, S5_SLAB, S5_STATE, S5_GROUP)
        return jnp.einsum('kgpi,gh->kgihp', b, eye).reshape(n_slab, LANE, wid)

    def out_mat(c):
        c = c.reshape(n_slab, S5_SLAB, S5_GROUP, S5_STATE)
        return jnp.einsum('kgip,gh->kgphi', c, eye).reshape(n_slab, wid, LANE)

    bm = jnp.concatenate([in_mat(bb_re), in_mat(bb_im)], axis=2).astype(BF16)
    cm = jnp.concatenate([out_mat(c_re), -out_mat(c_im)], axis=1).astype(BF16)
    a_re = lb_re.reshape(n_slab, 1, wid)
    a_im = lb_im.reshape(n_slab, 1, wid)

    bb, ts = S5_BB, S5_TS
    rows = bb * ts
    r = jnp.arange(rows)
    perm = ((r % bb)[:, None] * ts + (r // bb)[:, None] == r[None, :]).astype(BF16)
    blk = lambda i, t: (i, t, 0)
    return pl.pallas_call(
        _s5_kernel,
        out_shape=jax.ShapeDtypeStruct((bsz, seq, d), BF16),
        grid=(bsz // bb, seq // ts),
        in_specs=[pl.BlockSpec((bb, ts, d), blk),
                  pl.BlockSpec((bb, N_MOD, d), lambda i, t: (i, 0, 0)),
                  _const_spec((1, d)),
                  _const_spec((rows, rows)), _const_spec((rows, rows)),
                  _const_spec(bm.shape), _const_spec(cm.shape),
                  _const_spec(a_re.shape), _const_spec(a_im.shape),
                  _const_spec((1, d))],
        out_specs=pl.BlockSpec((bb, ts, d), blk),
        scratch_shapes=[pltpu.VMEM((rows, 2 * wid * n_slab), F32),
                        pltpu.VMEM((bb, 2 * wid * n_slab), F32)],
        compiler_params=_cparams(2),
        name="s5_scan",
    )(x, mod, g1.reshape(1, d), perm, perm.T, bm, cm, a_re, a_im, d_skip.reshape(1, d))


def kernel(x, c, ada_w, ada_b, norm1_g, norm2_g, final_g, rel_bias, mla_w_dkv, mla_g_q, mla_g_kv, mla_w_uq, mla_w_ukv, mla_w_o, nsa_w_in, nsa_pe_k, nsa_w1_k, nsa_w2_k, nsa_pe_v, nsa_w1_v, nsa_w2_v, nsa_w_o, s5_lam_re, s5_lam_im, s5_log_dt, s5_b_re, s5_b_im, s5_c_re, s5_c_im, s5_d, s5_w_glu, ffn_w_up, ffn_conv_w, ffn_conv_b, ffn_w_down):
    depth = ada_w.shape[0]
    mods = _ada_mod(c, ada_w, ada_b)
    for i in range(depth):
        mod = mods[i]
        kind, j = i % N_MIXERS, i // N_MIXERS
        if kind == 0:
            o = _mla_mixer(x, mod, norm1_g[i], mla_w_dkv[j], mla_g_q[j], mla_g_kv[j],
                           mla_w_uq[j], mla_w_ukv[j])
            w_post, glu = mla_w_o[j], False
        elif kind == 1:
            o = _nsa_mixer(x, mod, norm1_g[i], nsa_w_in[j], nsa_pe_k[j], nsa_w1_k[j], nsa_w2_k[j],
                           nsa_pe_v[j], nsa_w1_v[j], nsa_w2_v[j], rel_bias)
            w_post, glu = nsa_w_o[j], False
        else:
            o = _s5_mixer(x, mod, norm1_g[i], s5_lam_re[j], s5_lam_im[j], s5_log_dt[j],
                          s5_b_re[j], s5_b_im[j], s5_c_re[j], s5_c_im[j], s5_d[j])
            w_post, glu = s5_w_glu[j], True
        x = _post_ffn(x, o, mod, w_post.astype(BF16), norm2_g[i], ffn_w_up[i].astype(BF16),
                      ffn_conv_w[i], ffn_conv_b[i], ffn_w_down[i].astype(BF16), final_g,
                      glu=glu, final=(i == depth - 1))
    return x
```

```python
import functools
import math

import jax
import jax.numpy as jnp
from jax import lax
from jax.experimental import pallas as pl
from jax.experimental.pallas import tpu as pltpu

F32 = jnp.float32
BF16 = jnp.bfloat16

NORM_EPS = 1e-6
N_MOD = 6
N_MIXERS = 3

MLA_HEADS = 8
MLA_Q_LORA = 512
MLA_KV_LORA = 256
MLA_NOPE = 128
MLA_ROPE = 64
MLA_V = 128
MLA_QK = MLA_NOPE + MLA_ROPE
ROPE_THETA = 10000.0

NSA_HEADS = 8
NSA_GROUPS = 2
NSA_HPG = NSA_HEADS // NSA_GROUPS
NSA_DK = 128
NSA_DV = 128
CMP_BLOCK = 32
CMP_STRIDE = 16
CMP_RATIO = CMP_BLOCK // CMP_STRIDE
CMP_HIDDEN = 256
SLC_BLOCK = 64
SLC_TOPN = 16
N_LOCAL = 2
FORCE_BONUS = 1e4
WINDOW = 512
REL_BUCKETS = 32
REL_MAX_DIST = 128

S5_GROUP = 16
S5_STATE = 64
CONV_WIDTH = 3

LANE = 128
SUBLANE = 8
MXU_WIDTH = 256
V7X_VMEM_BYTES = 64 * 1024 * 1024
VMEM_LIMIT = V7X_VMEM_BYTES - 8 * 1024 * 1024
NEG = -1e30
LOG2E = math.log2(math.e)

TOKEN_TILE = 512
FFN_CHUNK = MXU_WIDTH
ADA_COLS = 1024
ATT_TILE = 128
MLA_TQ = 256
CMP_BATCH = 4
S5_BB = SUBLANE
S5_TS = 64
S5_SLAB = LANE // S5_GROUP


def _cparams(n_axes):
    return pltpu.CompilerParams(
        dimension_semantics=("arbitrary",) * n_axes,
        vmem_limit_bytes=VMEM_LIMIT)


def _const_spec(shape):
    zeros = (0,) * len(shape)
    return pl.BlockSpec(shape, lambda *_: zeros, pipeline_mode=pl.Buffered(1))


def _dot(a, b):
    return jnp.dot(a, b, preferred_element_type=F32)


def _dot_nt(a, b):
    return lax.dot_general(a, b, (((1,), (1,)), ((), ())), preferred_element_type=F32)


def _rms(x, g):
    return x * lax.rsqrt(jnp.mean(x * x, axis=-1, keepdims=True) + NORM_EPS) * g


def _norm_mod(x, g, shift, scale):
    return _rms(x, g) * (1.0 + scale) + shift


def _gelu_tanh(x):
    c = math.sqrt(2.0 / math.pi)
    return 0.5 * x * (1.0 + jnp.tanh(c * (x + 0.044715 * (x * x * x))))


def _sigmoid(x):
    return 1.0 / (1.0 + jnp.exp(-x))


def _softmax_pv(scores, values):
    m = functools.reduce(jnp.maximum, [jnp.max(s, axis=1, keepdims=True) for s in scores])
    acc = functools.reduce(
        jnp.add, [_dot(jnp.exp2(s - m).astype(BF16), v) for s, v in zip(scores, values)])
    width = acc.shape[1] // 2
    return acc[:, :width] / acc[:, width:]


def _t5_bucket(dist):
    n = jnp.maximum(dist, 0)
    max_exact = REL_BUCKETS // 2
    nf = jnp.maximum(n, 1).astype(F32)
    large = max_exact + (jnp.log(nf / max_exact) / math.log(REL_MAX_DIST / max_exact)
                         * (REL_BUCKETS - max_exact)).astype(jnp.int32)
    large = jnp.minimum(large, REL_BUCKETS - 1)
    return jnp.where(n < max_exact, n, large)


def _ada_kernel(c_ref, w_ref, b_ref, o_ref):
    c = c_ref[...]
    c_act = c * _sigmoid(c)
    o_ref[0] = jnp.dot(c_act, w_ref[0], preferred_element_type=F32,
                       precision=lax.Precision.HIGHEST) + b_ref[0]


def _ada_mod(c, ada_w, ada_b):
    depth, d, n = ada_w.shape
    bsz = c.shape[0]
    tn = ADA_COLS
    out = pl.pallas_call(
        _ada_kernel,
        out_shape=jax.ShapeDtypeStruct((depth, bsz, n), F32),
        grid=(depth, n // tn),
        in_specs=[pl.BlockSpec((bsz, d), lambda i, j: (0, 0)),
                  pl.BlockSpec((1, d, tn), lambda i, j: (i, 0, j)),
                  pl.BlockSpec((1, 1, tn), lambda i, j: (i, 0, j))],
        out_specs=pl.BlockSpec((1, bsz, tn), lambda i, j: (i, 0, j)),
        compiler_params=_cparams(2),
        name="ada_mod",
    )(c, ada_w, ada_b.reshape(depth, 1, n))
    return out.reshape(depth, bsz, N_MOD, d)


def _post_ffn_kernel(x_ref, o_ref, mod_ref, wpost_ref, g2_ref, wup_ref, cw_ref, cb_ref,
                     wdown_ref, gf_ref, out_ref, act_ref, carry_ref,
                     *, glu, final, fc):
    tm = x_ref.shape[1]
    d_ff = wdown_ref.shape[0]

    @pl.when(pl.program_id(1) == 0)
    def _():
        carry_ref[...] = jnp.zeros_like(carry_ref)

    x = x_ref[0]
    mod = mod_ref[0]
    y = _dot(o_ref[0], wpost_ref[...])
    if glu:
        d = x.shape[-1]
        y = y[:, :d] * _sigmoid(y[:, d:])
    x1 = x + mod[2:3] * y
    h = _norm_mod(x1, g2_ref[...], mod[3:4], mod[4:5]).astype(BF16)

    sub = SUBLANE
    row = lax.broadcasted_iota(jnp.int32, (sub, fc // 2), 0)

    def shifted(z, c, k):
        r = pltpu.roll(z, k, 0)
        head = jnp.where(row < k, pltpu.roll(c, k, 0), r[:sub])
        return jnp.concatenate([head, r[sub:]], axis=0)

    def conv(z, col):
        n = z.shape[1]
        c = carry_ref[:, col:col + n]
        carry_ref[:, col:col + n] = z[tm - sub:, :]
        w = cw_ref[:, col:col + n]
        return (w[0:1] * shifted(z, c, 2) + w[1:2] * shifted(z, c, 1) + w[2:3] * z
                + cb_ref[:, col:col + n])

    hw = fc // 2
    for f in range(0, d_ff, fc):
        zv = _dot(h, wup_ref[:, f:f + fc])
        zg = _dot(h, wup_ref[:, d_ff + f:d_ff + f + fc])
        for c in (0, hw):
            val = conv(zv[:, c:c + hw], f + c)
            gate = conv(zg[:, c:c + hw], d_ff + f + c)
            act_ref[:, f + c:f + c + hw] = (val * (gate * _sigmoid(gate))).astype(BF16)

    x2 = x1 + mod[5:6] * _dot(act_ref[...], wdown_ref[...])
    if final:
        x2 = _rms(x2, gf_ref[...])
    out_ref[0] = x2


def _post_ffn(x, o, mod, w_post, g2, w_up, conv_w, conv_b, w_down, final_g, *, glu, final):
    bsz, seq, d = x.shape
    d_ff = w_down.shape[0]
    tm = TOKEN_TILE
    fc = FFN_CHUNK
    row = lambda b, i: (b, i, 0)
    return pl.pallas_call(
        functools.partial(_post_ffn_kernel, glu=glu, final=final, fc=fc),
        out_shape=jax.ShapeDtypeStruct((bsz, seq, d), F32),
        grid=(bsz, seq // tm),
        in_specs=[pl.BlockSpec((1, tm, d), row),
                  pl.BlockSpec((1, tm, d), row),
                  pl.BlockSpec((1, N_MOD, d), lambda b, i: (b, 0, 0)),
                  _const_spec(w_post.shape),
                  _const_spec((1, d)),
                  _const_spec(w_up.shape),
                  _const_spec(conv_w.shape),
                  _const_spec((1, 2 * d_ff)),
                  _const_spec(w_down.shape),
                  _const_spec((1, d))],
        out_specs=pl.BlockSpec((1, tm, d), row),
        scratch_shapes=[pltpu.VMEM((tm, d_ff), BF16),
                        pltpu.VMEM((SUBLANE, 2 * d_ff), F32)],
        compiler_params=_cparams(2),
        name="post_ffn",
    )(x, o, mod, w_post, g2.reshape(1, d), w_up, conv_w, conv_b.reshape(1, -1), w_down,
      final_g.reshape(1, d))


def _rope_tables(seq, reps):
    half = MLA_ROPE // 2
    inv = ROPE_THETA ** (-jnp.arange(half, dtype=F32) / half)
    ang = jnp.arange(seq).astype(F32)[:, None] * inv[None, :]
    cos, sin = jnp.cos(ang), jnp.sin(ang)
    cos64 = jnp.concatenate([cos, cos], axis=1)
    sin64 = jnp.concatenate([-sin, sin], axis=1)
    return jnp.tile(cos64, (1, reps)), jnp.tile(sin64, (1, reps))


def _mla_proj_kernel(x_ref, mod_ref, g1_ref, wdkv_ref, gq_ref, gkv_ref, wuq_ref, wukv_ref,
                     cos_ref, sin_ref, qn_ref, qr_ref, kn_ref, kr_ref, v_ref):
    nq = MLA_HEADS * MLA_NOPE
    nr = MLA_HEADS * MLA_ROPE
    scale = MLA_QK ** -0.5 * LOG2E
    mod = mod_ref[0]
    h = _norm_mod(x_ref[0], g1_ref[...], mod[0:1], mod[1:2]).astype(BF16)
    dkv = _dot(h, wdkv_ref[...])
    c_q = _rms(dkv[:, :MLA_Q_LORA], gq_ref[...]).astype(BF16)
    c_kv = _rms(dkv[:, MLA_Q_LORA:MLA_Q_LORA + MLA_KV_LORA], gkv_ref[...]).astype(BF16)
    cos = cos_ref[...]
    sin = sin_ref[...]
    r0 = MLA_Q_LORA + MLA_KV_LORA
    kr_ref[0] = (dkv[:, r0:r0 + LANE] * cos[:, :LANE]
                 + dkv[:, r0 + LANE:r0 + 2 * LANE] * sin[:, :LANE]).astype(BF16)
    q = _dot(c_q, wuq_ref[...])
    qn_ref[0] = (q[:, :nq] * scale).astype(BF16)
    t = q[:, nq:]
    lane = lax.broadcasted_iota(jnp.int32, t.shape, 1)
    half = MLA_ROPE // 2
    swapped = jnp.where((lane & (MLA_ROPE - 1)) < half, pltpu.roll(t, nr - half, 1),
                        pltpu.roll(t, half, 1))
    qr_ref[0] = ((t * cos + swapped * sin) * scale).astype(BF16)
    kv = _dot(c_kv, wukv_ref[...])
    kn_ref[0] = kv[:, :nq].astype(BF16)
    v_ref[0] = kv[:, nq:].astype(BF16)


def _mla_attn_kernel(qn_ref, qr_ref, kn_ref, kr_ref, vin_ref, o_ref, v_ref):
    seq = qn_ref.shape[1]
    tq = MLA_TQ
    v_ref[0, :, :MLA_V] = vin_ref[0]
    v_ref[0, :, MLA_V:] = jnp.ones((seq, MLA_V), BF16)
    odd = pl.program_id(1) % 2
    lane = lax.broadcasted_iota(jnp.int32, (tq, LANE), 1)
    keep = (lane >= MLA_ROPE) == (odd == 1)
    row = lax.broadcasted_iota(jnp.int32, (tq, tq), 0)
    col = lax.broadcasted_iota(jnp.int32, (tq, tq), 1)
    def scores(i):
        lo, hi = i * tq, (i + 1) * tq
        qr = jnp.where(keep, qr_ref[0, lo:hi, :], jnp.zeros((), BF16))
        q = jnp.concatenate([qn_ref[0, lo:hi, :], qr], axis=1)
        k_d = jnp.concatenate([kn_ref[0, lo:hi, :], kr_ref[0, lo:hi, :]], axis=1)
        out = [jnp.where(col <= row, _dot_nt(q, k_d), NEG)]
        if i > 0:
            k_o = jnp.concatenate([kn_ref[0, :lo, :], kr_ref[0, :lo, :]], axis=1)
            out.append(_dot_nt(q, k_o))
        return out

    n_tile = seq // tq
    nxt = scores(0)
    for i in range(n_tile):
        lo, hi = i * tq, (i + 1) * tq
        cur = nxt
        if i + 1 < n_tile:
            nxt = scores(i + 1)
        values = [v_ref[0, lo:hi, :]] + ([v_ref[0, :lo, :]] if i > 0 else [])
        o_ref[0, lo:hi, :] = _softmax_pv(cur, values).astype(BF16)


def _mla_mixer(x, mod, g1, w_dkv, g_q, g_kv, w_uq, w_ukv):
    bsz, seq, d = x.shape
    hds = MLA_HEADS
    half = MLA_ROPE // 2
    swap = jnp.concatenate([jnp.arange(half, MLA_ROPE), jnp.arange(half)])
    r0 = MLA_Q_LORA + MLA_KV_LORA
    kr = w_dkv[:, r0:]
    wdkv = jnp.concatenate([w_dkv[:, :r0], kr, kr, kr[:, swap], kr[:, swap]], axis=1).astype(BF16)
    wq = w_uq.reshape(MLA_Q_LORA, hds, MLA_QK)
    wq_r = wq[:, :, MLA_NOPE:]
    wuq = jnp.concatenate([wq[:, :, :MLA_NOPE].reshape(MLA_Q_LORA, -1),
                           wq_r.reshape(MLA_Q_LORA, -1)], axis=1).astype(BF16)
    wkv = w_ukv.reshape(MLA_KV_LORA, hds, MLA_NOPE + MLA_V)
    wukv = jnp.concatenate([wkv[:, :, :MLA_NOPE].reshape(MLA_KV_LORA, -1),
                            wkv[:, :, MLA_NOPE:].reshape(MLA_KV_LORA, -1)], axis=1).astype(BF16)
    cos, sin = _rope_tables(seq, hds)
    nq, nr = hds * MLA_NOPE, hds * MLA_ROPE
    tm = TOKEN_TILE
    row = lambda i, b: (b, i, 0)
    qn, qr, kn, kr2, v = pl.pallas_call(
        _mla_proj_kernel,
        out_shape=[jax.ShapeDtypeStruct((bsz, seq, nq), BF16),
                   jax.ShapeDtypeStruct((bsz, seq, nr), BF16),
                   jax.ShapeDtypeStruct((bsz, seq, nq), BF16),
                   jax.ShapeDtypeStruct((bsz, seq, LANE), BF16),
                   jax.ShapeDtypeStruct((bsz, seq, nq), BF16)],
        grid=(seq // tm, bsz),
        in_specs=[pl.BlockSpec((1, tm, d), row),
                  pl.BlockSpec((1, N_MOD, d), lambda i, b: (b, 0, 0)),
                  _const_spec((1, d)),
                  _const_spec(wdkv.shape),
                  _const_spec((1, MLA_Q_LORA)),
                  _const_spec((1, MLA_KV_LORA)),
                  _const_spec(wuq.shape),
                  _const_spec(wukv.shape),
                  pl.BlockSpec((tm, nr), lambda i, b: (i, 0)),
                  pl.BlockSpec((tm, nr), lambda i, b: (i, 0))],
        out_specs=[pl.BlockSpec((1, tm, nq), row),
                   pl.BlockSpec((1, tm, nr), row),
                   pl.BlockSpec((1, tm, nq), row),
                   pl.BlockSpec((1, tm, LANE), row),
                   pl.BlockSpec((1, tm, nq), row)],
        compiler_params=_cparams(2),
        name="mla_proj",
    )(x, mod, g1.reshape(1, d), wdkv, g_q.reshape(1, -1), g_kv.reshape(1, -1), wuq, wukv, cos, sin)

    head = lambda b, h: (b, 0, h)
    return pl.pallas_call(
        _mla_attn_kernel,
        out_shape=jax.ShapeDtypeStruct((bsz, seq, hds * MLA_V), BF16),
        grid=(bsz, hds),
        in_specs=[pl.BlockSpec((1, seq, LANE), head),
                  pl.BlockSpec((1, seq, LANE), lambda b, h: (b, 0, h // 2)),
                  pl.BlockSpec((1, seq, LANE), head),
                  pl.BlockSpec((1, seq, LANE), lambda b, h: (b, 0, 0)),
                  pl.BlockSpec((1, seq, LANE), head)],
        out_specs=pl.BlockSpec((1, seq, LANE), head),
        scratch_shapes=[pltpu.VMEM((1, seq, 2 * MLA_V), BF16)],
        compiler_params=_cparams(2),
        name="mla_attn",
    )(qn, qr, kn, kr2, v)


def _nsa_proj_kernel(x_ref, mod_ref, g1_ref, w_ref, q_ref, kc_ref, vc_ref, ks_ref, vs_ref,
                     kw_ref, vw_ref, gate_ref, cmp_ref):
    nq = NSA_HEADS * NSA_DK
    nkv = NSA_GROUPS * NSA_DK
    tm = x_ref.shape[1]
    mod = mod_ref[0]
    h = _norm_mod(x_ref[0], g1_ref[...], mod[0:1], mod[1:2]).astype(BF16)
    p = _dot(h, w_ref[...])
    q_ref[0] = (p[:, :nq] * (NSA_DK ** -0.5 * LOG2E)).astype(BF16)
    for n, ref in enumerate((ks_ref, vs_ref, kw_ref, vw_ref)):
        ref[0] = p[:, nq + (n + 2) * nkv:nq + (n + 3) * nkv].astype(BF16)
    gate_ref[0] = _sigmoid(p[:, nq + 6 * nkv:])
    for n, ref in enumerate((kc_ref, vc_ref)):
        for g in range(NSA_GROUPS):
            c0 = nq + n * nkv + g * NSA_DK
            stage = cmp_ref.at[n * NSA_GROUPS + g]
            stage[...] = p[:, c0:c0 + NSA_DK]
            for r in range(CMP_STRIDE):
                ref[0, g, r] = stage[pl.ds(r, tm // CMP_STRIDE, stride=CMP_STRIDE), :].astype(BF16)


def _nsa_compress_kernel(ck_ref, cv_ref, pek_ref, pev_ref, w1k_ref, w1v_ref, w2k_ref, w2v_ref,
                         ok_ref, ov_ref):
    nb, grp, _, n_chunk, dk = ck_ref.shape
    half = CMP_STRIDE * dk
    rows = nb * grp * n_chunk
    for c_ref, pe_ref, w1_ref, w2_ref, o_ref in ((ck_ref, pek_ref, w1k_ref, w2k_ref, ok_ref),
                                                 (cv_ref, pev_ref, w1v_ref, w2v_ref, ov_ref)):
        chunks = jnp.concatenate(
            [jnp.concatenate([c_ref[b, g, r] for r in range(CMP_STRIDE)], axis=1)
             for b in range(nb) for g in range(grp)], axis=0)
        first = _dot(chunks, w1_ref[:half, :])
        second = _dot(chunks, w1_ref[half:, :])
        pe = jnp.broadcast_to(pe_ref[...], (SUBLANE, 2 * half)).astype(BF16)
        const = _dot(pe, w1_ref[...])[0:1]
        hid = _gelu_tanh(first + pltpu.roll(second, rows - 1, 0) + const).astype(BF16)
        o_ref[...] = _dot(hid, w2_ref[...]).astype(BF16).reshape(nb, grp, n_chunk, dk)


def _nsa_select_kernel(rb_ref, q_ref, kc_ref, vc_ref, bucket_ref, ovl_ref, bonus_ref,
                       ocmp_ref, sel_ref, bias_ref):
    g = pl.program_id(0)
    seq = q_ref.shape[1]
    n_blk = bonus_ref.shape[0]
    bucket = bucket_ref[...]

    @pl.when(pl.program_id(1) == 0)
    def _():
        for hp in range(NSA_HPG):
            acc = jnp.zeros(bucket.shape, F32)
            for r in range(REL_BUCKETS):
                acc = jnp.where(bucket == r, rb_ref[g * NSA_HPG + hp, r] * LOG2E, acc)
            bias_ref[hp] = acc

    valid = bucket >= 0
    kc = kc_ref[0, 0]
    vc = vc_ref[0, 0]
    psum = jnp.zeros(bucket.shape, F32)
    for hp in range(NSA_HPG):
        s = _dot_nt(q_ref[0, :, hp * NSA_DK:(hp + 1) * NSA_DK], kc) + bias_ref[hp]
        s = jnp.where(valid, s, NEG)
        m = jnp.max(s, axis=1, keepdims=True)
        m = jnp.where(m > 0.5 * NEG, m, 0.0)
        p = jnp.exp2(s - m)
        p = p / jnp.maximum(jnp.sum(p, axis=1, keepdims=True), 1e-30)
        ocmp_ref[0, :, hp * NSA_DV:(hp + 1) * NSA_DV] = _dot(p.astype(BF16), vc).astype(BF16)
        psum = psum + p

    imp = lax.dot_general(ovl_ref[...], psum, (((1,), (1,)), ((), ())),
                          preferred_element_type=F32, precision=lax.Precision.HIGHEST)
    bonus = bonus_ref[...]
    score = jnp.where(bonus > 0.5 * NEG, imp + bonus, NEG)
    blk = lax.broadcasted_iota(jnp.int32, (n_blk, seq), 0)
    rank = jnp.zeros((n_blk, seq), F32)
    for i in range(n_blk):
        row = score[i:i + 1, :]
        rank = rank + jnp.where(blk > i, jnp.where(row >= score, 1.0, 0.0),
                                jnp.where(row > score, 1.0, 0.0))
    selneg = jnp.where(rank < float(SLC_TOPN), jnp.where(score > 0.5 * NEG, 0.0, NEG), NEG)
    pad = jnp.zeros((LANE - n_blk, seq), F32)
    sel_ref[0, 0] = jnp.concatenate([selneg, pad], axis=0).T.astype(BF16)


def _nsa_attn_kernel(rb_ref, q_ref, sel_ref, ks_ref, vsin_ref, kw_ref, vwin_ref, et_ref, ocmp_ref,
                     gate_ref, bkt_ref, o_ref, kaug_ref, bias_ref, vs_ref, vw_ref):
    g = pl.program_id(0)
    t = ATT_TILE
    hp_n = NSA_HPG
    seq = q_ref.shape[1]

    @pl.when(pl.program_id(1) == 0)
    def _():
        kaug_ref[:, NSA_DK:] = et_ref[...]
        vs_ref[0, :, NSA_DV:] = jnp.ones((seq, NSA_DV), BF16)
        vw_ref[0, :, NSA_DV:] = jnp.ones((seq, NSA_DV), BF16)
        for kind in range(2):
            bkt = bkt_ref[kind]
            for hp in range(hp_n):
                hd = g * hp_n + hp
                acc = jnp.zeros((t, t), F32)
                for r in range(REL_BUCKETS - 1):
                    acc = jnp.where(
                        bkt == r, (rb_ref[hd, r] - rb_ref[hd, REL_BUCKETS - 1]) * LOG2E, acc)
                bias_ref[hp * t:(hp + 1) * t, (1 - kind) * t:(2 - kind) * t] = acc
        bias_ref[:, 2 * t:] = jnp.zeros((hp_n * t, t), F32)

    kaug_ref[:, :NSA_DK] = ks_ref[0]
    vs_ref[0, :, :NSA_DV] = vsin_ref[0]
    vw_ref[0, :, :NSA_DV] = vwin_ref[0]

    rows = hp_n * t
    r2 = lax.broadcasted_iota(jnp.int32, (rows, 2 * t), 0) & (t - 1)
    c2 = lax.broadcasted_iota(jnp.int32, (rows, 2 * t), 1)
    r1 = lax.broadcasted_iota(jnp.int32, (rows, t), 0) & (t - 1)
    c1 = lax.broadcasted_iota(jnp.int32, (rows, t), 1)
    near_ok = c2 <= r2 + t
    first_ok = c2 <= r2
    edge_ok = c1 > r1
    lane = lax.broadcasted_iota(jnp.int32, (t, LANE), 1)

    def near(i):
        if i == 0:
            return 0, 2 * t, bias_ref[:, t:], first_ok
        return (i - 1) * t, (i + 1) * t, bias_ref[:, :2 * t], near_ok

    def scores_sel(i):
        lo, hi = i * t, (i + 1) * t
        n_lo, n_hi, bias, ok = near(i)
        sel = sel_ref[0, 0, lo:hi, :]
        q_sel = jnp.concatenate(
            [jnp.concatenate([q_ref[0, lo:hi, hp * NSA_DK:(hp + 1) * NSA_DK], sel], axis=1)
             for hp in range(hp_n)], axis=0)
        out = [jnp.where(ok, _dot_nt(q_sel, kaug_ref[n_lo:n_hi, :]) + bias, NEG)]
        if n_lo > 0:
            out.append(_dot_nt(q_sel, kaug_ref[:n_lo, :]))
        return out

    def scores_win(i):
        lo, hi = i * t, (i + 1) * t
        n_lo, n_hi, bias, ok = near(i)
        q_win = jnp.concatenate(
            [q_ref[0, lo:hi, hp * NSA_DK:(hp + 1) * NSA_DK] for hp in range(hp_n)], axis=0)
        out = [jnp.where(ok, _dot_nt(q_win, kw_ref[0, n_lo:n_hi, :]) + bias, NEG)]
        w_lo = max(lo - WINDOW, 0)
        if n_lo > w_lo:
            s_m = _dot_nt(q_win, kw_ref[0, w_lo:n_lo, :])
            if lo - WINDOW >= 0:
                s_m = jnp.concatenate([jnp.where(edge_ok, s_m[:, :t], NEG), s_m[:, t:]], axis=1)
            out.append(s_m)
        return out

    def pieces(v_ref, i, far_lo):
        n_lo, n_hi, _, _ = near(i)
        return [v_ref[0, n_lo:n_hi, :]] + ([v_ref[0, far_lo:n_lo, :]] if n_lo > far_lo else [])

    n_tile = seq // t
    nxt_sel = scores_sel(0)
    for i in range(n_tile):
        lo, hi = i * t, (i + 1) * t
        cur_sel = nxt_sel
        cur_win = scores_win(i)
        o_slc = _softmax_pv(cur_sel, pieces(vs_ref, i, 0))
        if i + 1 < n_tile:
            nxt_sel = scores_sel(i + 1)
        o_win = _softmax_pv(cur_win, pieces(vw_ref, i, max(lo - WINDOW, 0)))

        gate = gate_ref[0, lo:hi, :]
        for hp in range(hp_n):
            hd = g * hp_n + hp

            def gcol(branch):
                return jnp.sum(jnp.where(lane == branch * NSA_HEADS + hd, gate, 0.0),
                               axis=1, keepdims=True)
            sl = slice(hp * t, (hp + 1) * t)
            o = (gcol(0) * ocmp_ref[0, lo:hi, hp * NSA_DV:(hp + 1) * NSA_DV].astype(F32)
                 + gcol(1) * o_slc[sl] + gcol(2) * o_win[sl])
            o_ref[0, lo:hi, hp * NSA_DV:(hp + 1) * NSA_DV] = o.astype(BF16)


def _nsa_mixer(x, mod, g1, w_in, pe_k, w1_k, w2_k, pe_v, w1_v, w2_v, rel_bias):
    bsz, seq, d = x.shape
    grp, hpg = NSA_GROUPS, NSA_HPG
    nq, nkv = NSA_HEADS * NSA_DK, NSA_GROUPS * NSA_DK
    n_gate = 3 * NSA_HEADS
    w = jnp.pad(w_in, ((0, 0), (0, LANE - n_gate))).astype(BF16)
    tm = TOKEN_TILE
    row = lambda i, b: (b, i, 0)
    assert CMP_RATIO == 2 and NSA_DK == NSA_DV
    n_chunk = seq // CMP_STRIDE
    n_cmp = n_chunk - CMP_RATIO + 1
    kv_shape = jax.ShapeDtypeStruct((bsz, seq, nkv), BF16)
    cmp_shape = jax.ShapeDtypeStruct((bsz, grp, CMP_STRIDE, n_chunk, NSA_DK), BF16)
    cmp_spec = pl.BlockSpec((1, grp, CMP_STRIDE, tm // CMP_STRIDE, NSA_DK),
                            lambda i, b: (b, 0, 0, i, 0))
    q, kc, vc, ks, vs, kw, vw, gates = pl.pallas_call(
        _nsa_proj_kernel,
        out_shape=[jax.ShapeDtypeStruct((bsz, seq, nq), BF16)] + [cmp_shape] * 2 + [kv_shape] * 4
                  + [jax.ShapeDtypeStruct((bsz, seq, LANE), F32)],
        grid=(seq // tm, bsz),
        in_specs=[pl.BlockSpec((1, tm, d), row),
                  pl.BlockSpec((1, N_MOD, d), lambda i, b: (b, 0, 0)),
                  _const_spec((1, d)),
                  _const_spec(w.shape)],
        out_specs=[pl.BlockSpec((1, tm, nq), row)] + [cmp_spec] * 2
                  + [pl.BlockSpec((1, tm, nkv), row)] * 4 + [pl.BlockSpec((1, tm, LANE), row)],
        scratch_shapes=[pltpu.VMEM((2 * grp, tm, NSA_DK), F32)],
        compiler_params=_cparams(2),
        name="nsa_proj",
    )(x, mod, g1.reshape(1, d), w)

    nb = math.gcd(bsz, CMP_BATCH)
    wide = CMP_BLOCK * NSA_DK
    chunk_spec = pl.BlockSpec((nb, grp, CMP_STRIDE, n_chunk, NSA_DK), lambda i: (i, 0, 0, 0, 0))
    out_spec = pl.BlockSpec((nb, grp, n_chunk, NSA_DK), lambda i: (i, 0, 0, 0))
    kcmp, vcmp = pl.pallas_call(
        _nsa_compress_kernel,
        out_shape=[jax.ShapeDtypeStruct((bsz, grp, n_chunk, NSA_DK), BF16)] * 2,
        grid=(bsz // nb,),
        in_specs=[chunk_spec, chunk_spec,
                  _const_spec((1, wide)), _const_spec((1, wide)),
                  _const_spec((wide, CMP_HIDDEN)), _const_spec((wide, CMP_HIDDEN)),
                  _const_spec((CMP_HIDDEN, NSA_DK)), _const_spec((CMP_HIDDEN, NSA_DV))],
        out_specs=[out_spec, out_spec],
        compiler_params=_cparams(1),
        name="nsa_compress",
    )(kc, vc, pe_k.reshape(1, wide), pe_v.reshape(1, wide), w1_k.astype(BF16), w1_v.astype(BF16),
      w2_k.astype(BF16), w2_v.astype(BF16))

    pos = jnp.arange(seq)
    cmp_start = jnp.arange(n_chunk) * CMP_STRIDE
    dist_c = pos[:, None] - (cmp_start + CMP_BLOCK - 1)[None, :]
    ok_c = (dist_c >= 0) & (jnp.arange(n_chunk) < n_cmp)[None, :]
    bucket_c = jnp.where(ok_c, _t5_bucket(dist_c), -1).astype(jnp.int32)
    n_blk = seq // SLC_BLOCK
    blk = jnp.arange(n_blk)
    blk_start = blk * SLC_BLOCK
    overlap_t = ((cmp_start[None, :] < blk_start[:, None] + SLC_BLOCK)
                 & (cmp_start[None, :] + CMP_BLOCK > blk_start[:, None])
                 & (jnp.arange(n_chunk) < n_cmp)[None, :]).astype(F32)
    back = (pos // SLC_BLOCK)[None, :] - blk[:, None]
    forced = (blk[:, None] == 0) | ((back >= 0) & (back < N_LOCAL))
    causal_blk = blk_start[:, None] <= pos[None, :]
    bonus_t = jnp.where(causal_blk, jnp.where(forced, FORCE_BONUS, 0.0), NEG).astype(F32)

    ocmp, selneg = pl.pallas_call(
        _nsa_select_kernel,
        out_shape=[jax.ShapeDtypeStruct((bsz, seq, nq), BF16),
                   jax.ShapeDtypeStruct((bsz, grp, seq, LANE), BF16)],
        grid=(grp, bsz),
        in_specs=[pl.BlockSpec(memory_space=pltpu.SMEM),
                  pl.BlockSpec((1, seq, hpg * NSA_DK), lambda g, b: (b, 0, g)),
                  pl.BlockSpec((1, 1, n_chunk, NSA_DK), lambda g, b: (b, g, 0, 0)),
                  pl.BlockSpec((1, 1, n_chunk, NSA_DV), lambda g, b: (b, g, 0, 0)),
                  _const_spec(bucket_c.shape),
                  _const_spec(overlap_t.shape),
                  _const_spec(bonus_t.shape)],
        out_specs=[pl.BlockSpec((1, seq, hpg * NSA_DV), lambda g, b: (b, 0, g)),
                   pl.BlockSpec((1, 1, seq, LANE), lambda g, b: (b, g, 0, 0))],
        scratch_shapes=[pltpu.VMEM((hpg, seq, n_chunk), F32)],
        compiler_params=_cparams(2),
        name="nsa_select",
    )(rel_bias, q, kcmp, vcmp, bucket_c, overlap_t, bonus_t)

    t = ATT_TILE
    rr = jnp.arange(t)
    near = jnp.stack([_t5_bucket(k * t + rr[:, None] - rr[None, :]) for k in range(2)]).astype(jnp.int32)
    e_t = (jnp.arange(seq)[:, None] // SLC_BLOCK == jnp.arange(LANE)[None, :]).astype(BF16)
    grp_blk = lambda g, b: (b, 0, g)
    return pl.pallas_call(
        _nsa_attn_kernel,
        out_shape=jax.ShapeDtypeStruct((bsz, seq, nq), BF16),
        grid=(grp, bsz),
        in_specs=[pl.BlockSpec(memory_space=pltpu.SMEM),
                  pl.BlockSpec((1, seq, hpg * NSA_DK), grp_blk),
                  pl.BlockSpec((1, 1, seq, LANE), lambda g, b: (b, g, 0, 0)),
                  pl.BlockSpec((1, seq, NSA_DK), grp_blk),
                  pl.BlockSpec((1, seq, NSA_DV), grp_blk),
                  pl.BlockSpec((1, seq, NSA_DK), grp_blk),
                  pl.BlockSpec((1, seq, NSA_DV), grp_blk),
                  _const_spec(e_t.shape),
                  pl.BlockSpec((1, seq, hpg * NSA_DV), grp_blk),
                  pl.BlockSpec((1, seq, LANE), lambda g, b: (b, 0, 0)),
                  _const_spec(near.shape)],
        out_specs=pl.BlockSpec((1, seq, hpg * NSA_DV), grp_blk),
        scratch_shapes=[pltpu.VMEM((seq, NSA_DK + LANE), BF16),
                        pltpu.VMEM((hpg * t, 3 * t), F32),
                        pltpu.VMEM((1, seq, 2 * NSA_DV), BF16),
                        pltpu.VMEM((1, seq, 2 * NSA_DV), BF16)],
        compiler_params=_cparams(2),
        name="nsa_attn",
    )(rel_bias, q, selneg, ks, vs, kw, vw, e_t, ocmp, gates, near)


def _s5_kernel(x_ref, mod_ref, g1_ref, perm_ref, permt_ref, bm_ref, cm_ref, are_ref, aim_ref,
               dskip_ref, y_ref, bu_ref, state_ref):
    bb, ts, d = x_ref.shape
    rows = bb * ts
    n_slab = d // LANE
    wid = S5_SLAB * S5_STATE

    @pl.when(pl.program_id(1) == 0)
    def _():
        state_ref[...] = jnp.zeros_like(state_ref)

    mod = mod_ref[...]
    u = (_rms(x_ref[...], g1_ref[...]) * (1.0 + mod[:, 1:2, :]) + mod[:, 0:1, :]).reshape(rows, d)
    u_tb = _dot(perm_ref[...], u.astype(BF16)).astype(BF16)
    def project_in(k):
        bu_ref[:, 2 * k * wid:2 * (k + 1) * wid] = _dot(u_tb[:, k * LANE:(k + 1) * LANE], bm_ref[k])

    def scan(k):
        c_re, c_im = 2 * k * wid, (2 * k + 1) * wid
        a_re = jnp.broadcast_to(are_ref[k], (bb, wid))
        a_im = jnp.broadcast_to(aim_ref[k], (bb, wid))
        xr, xi = state_ref[:, c_re:c_re + wid], state_ref[:, c_im:c_im + wid]
        for t in range(ts):
            r0 = t * bb
            nr = a_re * xr - a_im * xi + bu_ref[r0:r0 + bb, c_re:c_re + wid]
            ni = a_re * xi + a_im * xr + bu_ref[r0:r0 + bb, c_im:c_im + wid]
            bu_ref[r0:r0 + bb, c_re:c_re + wid] = nr
            bu_ref[r0:r0 + bb, c_im:c_im + wid] = ni
            xr, xi = nr, ni
        state_ref[:, c_re:c_re + wid] = xr
        state_ref[:, c_im:c_im + wid] = xi

    def project_out(k):
        return _dot(bu_ref[:, 2 * k * wid:2 * (k + 1) * wid].astype(BF16), cm_ref[k]).astype(BF16)

    ys = []
    project_in(0)
    for k in range(n_slab):
        if k + 1 < n_slab:
            project_in(k + 1)
        scan(k)
        ys.append(project_out(k))
    y_tb = jnp.concatenate(ys, axis=1)
    y = _dot(permt_ref[...], y_tb) + dskip_ref[...] * u
    y_ref[...] = _gelu_tanh(y).astype(BF16).reshape(bb, ts, d)


def _s5_mixer(x, mod, g1, lam_re, lam_im, log_dt, b_re, b_im, c_re, c_im, d_skip):
    bsz, seq, d = x.shape
    n_slab = d // LANE
    dt = jnp.exp(log_dt)[:, None]
    mag = jnp.exp(lam_re * dt)
    lb_re = mag * jnp.cos(lam_im * dt)
    lb_im = mag * jnp.sin(lam_im * dt)
    den = lam_re * lam_re + lam_im * lam_im
    f_re = ((lb_re - 1.0) * lam_re + lb_im * lam_im) / den
    f_im = (lb_im * lam_re - (lb_re - 1.0) * lam_im) / den
    bb_re = f_re[..., None] * b_re - f_im[..., None] * b_im
    bb_im = f_re[..., None] * b_im + f_im[..., None] * b_re
    eye = jnp.eye(S5_SLAB, dtype=F32)
    wid = S5_SLAB * S5_STATE

    def in_mat(b):
        b = b.reshape(n_slab, S5_SLAB, S5_STATE, S5_GROUP)
        return jnp.einsum('kgpi,gh->kgihp', b, eye).reshape(n_slab, LANE, wid)

    def out_mat(c):
        c = c.reshape(n_slab, S5_SLAB, S5_GROUP, S5_STATE)
        return jnp.einsum('kgip,gh->kgphi', c, eye).reshape(n_slab, wid, LANE)

    bm = jnp.concatenate([in_mat(bb_re), in_mat(bb_im)], axis=2).astype(BF16)
    cm = jnp.concatenate([out_mat(c_re), -out_mat(c_im)], axis=1).astype(BF16)
    a_re = lb_re.reshape(n_slab, 1, wid)
    a_im = lb_im.reshape(n_slab, 1, wid)

    bb, ts = S5_BB, S5_TS
    rows = bb * ts
    r = jnp.arange(rows)
    perm = ((r % bb)[:, None] * ts + (r // bb)[:, None] == r[None, :]).astype(BF16)
    blk = lambda i, t: (i, t, 0)
    return pl.pallas_call(
        _s5_kernel,
        out_shape=jax.ShapeDtypeStruct((bsz, seq, d), BF16),
        grid=(bsz // bb, seq // ts),
        in_specs=[pl.BlockSpec((bb, ts, d), blk),
                  pl.BlockSpec((bb, N_MOD, d), lambda i, t: (i, 0, 0)),
                  _const_spec((1, d)),
                  _const_spec((rows, rows)), _const_spec((rows, rows)),
                  _const_spec(bm.shape), _const_spec(cm.shape),
                  _const_spec(a_re.shape), _const_spec(a_im.shape),
                  _const_spec((1, d))],
        out_specs=pl.BlockSpec((bb, ts, d), blk),
        scratch_shapes=[pltpu.VMEM((rows, 2 * wid * n_slab), F32),
                        pltpu.VMEM((bb, 2 * wid * n_slab), F32)],
        compiler_params=_cparams(2),
        name="s5_scan",
    )(x, mod, g1.reshape(1, d), perm, perm.T, bm, cm, a_re, a_im, d_skip.reshape(1, d))


def kernel(x, c, ada_w, ada_b, norm1_g, norm2_g, final_g, rel_bias, mla_w_dkv, mla_g_q, mla_g_kv, mla_w_uq, mla_w_ukv, mla_w_o, nsa_w_in, nsa_pe_k, nsa_w1_k, nsa_w2_k, nsa_pe_v, nsa_w1_v, nsa_w2_v, nsa_w_o, s5_lam_re, s5_lam_im, s5_log_dt, s5_b_re, s5_b_im, s5_c_re, s5_c_im, s5_d, s5_w_glu, ffn_w_up, ffn_conv_w, ffn_conv_b, ffn_w_down):
    depth = ada_w.shape[0]
    mods = _ada_mod(c, ada_w, ada_b)
    for i in range(depth):
        mod = mods[i]
        kind, j = i % N_MIXERS, i // N_MIXERS
        if kind == 0:
            o = _mla_mixer(x, mod, norm1_g[i], mla_w_dkv[j], mla_g_q[j], mla_g_kv[j],
                           mla_w_uq[j], mla_w_ukv[j])
            w_post, glu = mla_w_o[j], False
        elif kind == 1:
            o = _nsa_mixer(x, mod, norm1_g[i], nsa_w_in[j], nsa_pe_k[j], nsa_w1_k[j], nsa_w2_k[j],
                           nsa_pe_v[j], nsa_w1_v[j], nsa_w2_v[j], rel_bias)
            w_post, glu = nsa_w_o[j], False
        else:
            o = _s5_mixer(x, mod, norm1_g[i], s5_lam_re[j], s5_lam_im[j], s5_log_dt[j],
                          s5_b_re[j], s5_b_im[j], s5_c_re[j], s5_c_im[j], s5_d[j])
            w_post, glu = s5_w_glu[j], True
        x = _post_ffn(x, o, mod, w_post.astype(BF16), norm2_g[i], ffn_w_up[i].astype(BF16),
                      ffn_conv_w[i], ffn_conv_b[i], ffn_w_down[i].astype(BF16), final_g,
                      glu=glu, final=(i == depth - 1))
    return x
```
